```python
import math
import jax, jax.numpy as jnp
from jax import lax
import numpy as np

D_MODEL = 1024
BATCH = 8
SEQ = 4096
DEPTH = 1

EPS = 1e-6
SSM_EXPAND = 2
SSM_D_INNER = SSM_EXPAND * D_MODEL
SSM_HEADDIM = 64
SSM_HEADS = SSM_D_INNER // SSM_HEADDIM
SSM_GROUPS = 4
SSM_HEADS_PER_GROUP = SSM_HEADS // SSM_GROUPS
SSM_STATE = 128
SSM_CONV = 4
SSM_CHUNK = 128
SSM_CONV_DIM = SSM_D_INNER + 2 * SSM_GROUPS * SSM_STATE
RET_HEADS = 4
RET_QK_DIM = 256
RET_V_DIM = 512
RET_QK_WIDTH = RET_HEADS * RET_QK_DIM
RET_V_WIDTH = RET_HEADS * RET_V_DIM
RET_CHUNK = 128
ROPE_BASE = 10000.0
N_EXPERTS = 256
TOP_K = 8
N_ROUTE_GROUPS = 8
TOPK_ROUTE_GROUPS = 4
EXPERT_DIM = 256
SHARED_DIM = 256
ROUTED_SCALE = 2.5
MOE_BLOCK = 128
IN_WIDTHS = (SSM_D_INNER, SSM_CONV_DIM, SSM_HEADS, RET_QK_WIDTH, RET_QK_WIDTH,
             RET_V_WIDTH, RET_V_WIDTH, D_MODEL, D_MODEL)
IN_DIM = sum(IN_WIDTHS)

kernel_name = "hybrid_ssd_retention_moe_adaln"


def rms_norm(x, w):
    xf = x.astype(jnp.float32)
    y = xf * lax.rsqrt(jnp.mean(xf * xf, axis=-1, keepdims=True) + EPS)
    return (y * w.astype(jnp.float32)).astype(x.dtype)


def causal_depthwise_conv(u, w, b):
    k = w.shape[0]
    out = lax.conv_general_dilated(u, w[:, None, :].astype(u.dtype), window_strides=(1,),
                                   padding=[(k - 1, 0)],
                                   dimension_numbers=('NWC', 'WIO', 'NWC'),
                                   feature_group_count=u.shape[-1])
    return out + b


def to_chunks(t, chunk):
    b, s = t.shape[:2]
    return jnp.moveaxis(t.reshape(b, s // chunk, chunk, *t.shape[2:]), 1, 0)


def from_chunks(t):
    t = jnp.moveaxis(t, 0, 1)
    return t.reshape(t.shape[0], t.shape[1] * t.shape[2], *t.shape[3:])


def ssd_chunked(xh, dt, A, bm, cm):
    bsz = xh.shape[0]
    L = SSM_CHUNK
    xh, dt, bm, cm = (t.astype(jnp.float32) for t in (xh, dt, bm, cm))
    a = dt * A.astype(jnp.float32)
    causal = jnp.tril(jnp.ones((L, L), dtype=bool))

    def step(state, inp):
        xc, dtc, ac, bc, cc = inp
        acum = jnp.cumsum(ac, axis=1)
        acum_t = jnp.moveaxis(acum, 1, -1)
        seg = acum_t[..., :, None] - acum_t[..., None, :]
        decay = jnp.exp(jnp.where(causal, seg, -jnp.inf))
        cb = jnp.einsum('blgn,bsgn->bgls', cc, bc)
        w = cb[:, :, None] * decay * jnp.moveaxis(dtc, 1, -1)[..., None, :]
        y_intra = jnp.einsum('bghls,bsghp->blghp', w, xc)
        y_inter = jnp.einsum('blgn,bghpn->blghp', cc, state) * jnp.exp(acum)[..., None]
        to_end = jnp.exp(acum[:, -1:] - acum) * dtc
        new_state = (state * jnp.exp(acum[:, -1])[..., None, None]
                     + jnp.einsum('bsgn,bsgh,bsghp->bghpn', bc, to_end, xc))
        return new_state, y_intra + y_inter

    state0 = jnp.zeros((bsz, SSM_GROUPS, SSM_HEADS_PER_GROUP, SSM_HEADDIM, SSM_STATE), jnp.float32)
    xs = tuple(to_chunks(t, L) for t in (xh, dt, a, bm, cm))
    _, ys = lax.scan(step, state0, xs)
    return from_chunks(ys)


def rotary(t, positions):
    half = t.shape[-1] // 2
    inv = 1.0 / (ROPE_BASE ** (jnp.arange(half, dtype=jnp.float32) / half))
    ang = positions.astype(jnp.float32)[..., None] * inv
    cos, sin = jnp.cos(ang)[:, :, None], jnp.sin(ang)[:, :, None]
    t1, t2 = t[..., :half], t[..., half:]
    return jnp.concatenate([t1 * cos - t2 * sin, t1 * sin + t2 * cos], axis=-1)


def retention_chunked(q, k, v):
    bsz = q.shape[0]
    L = RET_CHUNK
    log_gamma = jnp.log1p(-(2.0 ** (-5.0 - jnp.arange(RET_HEADS, dtype=jnp.float32))))
    idx = jnp.arange(L, dtype=jnp.float32)
    causal = jnp.tril(jnp.ones((L, L), dtype=bool))
    rel = jnp.where(causal, idx[:, None] - idx[None, :], 0.0)
    intra_decay = jnp.where(causal, jnp.exp(rel[None] * log_gamma[:, None, None]), 0.0)
    q_decay = jnp.exp((idx + 1.0)[:, None] * log_gamma[None])
    k_decay = jnp.exp((L - 1.0 - idx)[:, None] * log_gamma[None])
    chunk_decay = jnp.exp(L * log_gamma)

    def step(state, inp):
        qc, kc, vc = inp
        s = jnp.einsum('blhd,bshd->bhls', qc, kc) * intra_decay
        y = (jnp.einsum('bhls,bshv->blhv', s, vc)
             + jnp.einsum('blhd,bhdv->blhv', qc * q_decay[:, :, None], state))
        new_state = (state * chunk_decay[:, None, None]
                     + jnp.einsum('bshd,bshv->bhdv', kc * k_decay[:, :, None], vc))
        return new_state, y

    state0 = jnp.zeros((bsz, RET_HEADS, RET_QK_DIM, RET_V_DIM), jnp.float32)
    xs = tuple(to_chunks(t.astype(jnp.float32), L) for t in (q, k, v))
    _, ys = lax.scan(step, state0, xs)
    return from_chunks(ys)


def hybrid_mixer(h, positions, w_in, conv_w, conv_b, dt_bias, a_log, d_skip, ssm_norm_w,
                 w_ssm_out, w_ret_out, w_out):
    bsz, s, _ = h.shape
    split_points = np.cumsum(IN_WIDTHS)[:-1].tolist()
    z, xbc, dt, q, k, v, g, gate_s, gate_r = jnp.split(h @ w_in, split_points, axis=-1)

    xbc = jax.nn.silu(causal_depthwise_conv(xbc, conv_w, conv_b))
    xs, bm, cm = jnp.split(xbc, [SSM_D_INNER, SSM_D_INNER + SSM_GROUPS * SSM_STATE], axis=-1)
    xs = xs.reshape(bsz, s, SSM_GROUPS, SSM_HEADS_PER_GROUP, SSM_HEADDIM)
    bm = bm.reshape(bsz, s, SSM_GROUPS, SSM_STATE)
    cm = cm.reshape(bsz, s, SSM_GROUPS, SSM_STATE)
    dt = jax.nn.softplus(dt.astype(jnp.float32) + dt_bias.astype(jnp.float32))
    dt = dt.reshape(bsz, s, SSM_GROUPS, SSM_HEADS_PER_GROUP)
    A = -jnp.exp(a_log.astype(jnp.float32)).reshape(SSM_GROUPS, SSM_HEADS_PER_GROUP)
    y = ssd_chunked(xs, dt, A, bm, cm) + xs.astype(jnp.float32) * d_skip.reshape(
        SSM_GROUPS, SSM_HEADS_PER_GROUP, 1).astype(jnp.float32)
    y = y.reshape(bsz, s, SSM_D_INNER) * jax.nn.silu(z.astype(jnp.float32))
    yg = y.reshape(bsz, s, SSM_GROUPS, SSM_D_INNER // SSM_GROUPS)
    yg = yg * lax.rsqrt(jnp.mean(yg * yg, axis=-1, keepdims=True) + EPS)
    y = (yg.reshape(bsz, s, SSM_D_INNER) * ssm_norm_w.astype(jnp.float32)).astype(h.dtype)
    y_ssm = y @ w_ssm_out

    q = rotary(q.reshape(bsz, s, RET_HEADS, RET_QK_DIM).astype(jnp.float32), positions)
    k = rotary(k.reshape(bsz, s, RET_HEADS, RET_QK_DIM).astype(jnp.float32), positions) * (RET_QK_DIM ** -0.5)
    v = v.reshape(bsz, s, RET_HEADS, RET_V_DIM)
    yr = retention_chunked(q, k, v)
    mu = jnp.mean(yr, axis=-1, keepdims=True)
    var = jnp.mean(jnp.square(yr - mu), axis=-1, keepdims=True)
    yr = ((yr - mu) * lax.rsqrt(var + EPS)).reshape(bsz, s, RET_V_WIDTH)
    yr = (jax.nn.silu(g.astype(jnp.float32)) * yr).astype(h.dtype)
    y_ret = yr @ w_ret_out

    merged = jax.nn.sigmoid(gate_s) * y_ssm + jax.nn.sigmoid(gate_r) * y_ret
    return merged @ w_out


def swiglu(x, w_gate, w_up, w_down):
    return (jax.nn.silu(x @ w_gate) * (x @ w_up)) @ w_down


def moe(h, w_router, router_bias, w_exp_gate, w_exp_up, w_exp_down, w_sh_gate, w_sh_up, w_sh_down):
    bsz, s, d = h.shape
    T = bsz * s
    xf = h.reshape(T, d)
    scores = jax.nn.sigmoid(xf.astype(jnp.float32) @ w_router.astype(jnp.float32))
    sel = scores + router_bias.astype(jnp.float32)
    grp = sel.reshape(T, N_ROUTE_GROUPS, N_EXPERTS // N_ROUTE_GROUPS)
    grp_score = jnp.sum(lax.top_k(grp, 2)[0], axis=-1)
    _, top_groups = lax.top_k(grp_score, TOPK_ROUTE_GROUPS)
    gmask = jnp.any(top_groups[..., None] == jnp.arange(N_ROUTE_GROUPS), axis=-2)
    gmask = jnp.repeat(gmask, N_EXPERTS // N_ROUTE_GROUPS, axis=-1)
    _, idx = lax.top_k(jnp.where(gmask, sel, -jnp.inf), TOP_K)
    wts = jnp.take_along_axis(scores, idx, axis=-1)
    wts = wts / (jnp.sum(wts, axis=-1, keepdims=True) + 1e-20) * ROUTED_SCALE

    n_assign = T * TOP_K
    flat_e = idx.reshape(-1).astype(jnp.int32)
    flat_tok = jnp.arange(n_assign, dtype=jnp.int32) // TOP_K
    flat_w = wts.reshape(-1)
    order = jnp.argsort(flat_e, stable=True)
    e_sorted, tok_sorted, w_sorted = flat_e[order], flat_tok[order], flat_w[order]
    counts = jnp.zeros((N_EXPERTS,), jnp.int32).at[flat_e].add(1)
    padded = (counts + MOE_BLOCK - 1) // MOE_BLOCK * MOE_BLOCK
    group_start = jnp.cumsum(counts) - counts
    padded_end = jnp.cumsum(padded)
    padded_start = padded_end - padded
    dest = padded_start[e_sorted] + (jnp.arange(n_assign, dtype=jnp.int32) - group_start[e_sorted])
    n_blocks = -(-n_assign // MOE_BLOCK) + N_EXPERTS
    slot_tok = jnp.full((n_blocks * MOE_BLOCK,), T, jnp.int32).at[dest].set(tok_sorted)
    slot_w = jnp.zeros((n_blocks * MOE_BLOCK,), jnp.float32).at[dest].set(w_sorted)
    block_expert = jnp.minimum(
        jnp.searchsorted(padded_end, jnp.arange(n_blocks, dtype=jnp.int32) * MOE_BLOCK, side='right'),
        N_EXPERTS - 1).astype(jnp.int32)
    xpad = jnp.concatenate([xf, jnp.zeros((1, d), xf.dtype)], axis=0)

    def block_step(acc, inp):
        toks, ws, e = inp
        yb = swiglu(xpad[toks], w_exp_gate[e], w_exp_up[e], w_exp_down[e])
        return acc.at[toks].add(yb.astype(jnp.float32) * ws[:, None]), None

    acc0 = jnp.zeros((T + 1, d), jnp.float32)
    routed, _ = lax.scan(block_step, acc0,
                         (slot_tok.reshape(n_blocks, MOE_BLOCK), slot_w.reshape(n_blocks, MOE_BLOCK), block_expert))
    shared = swiglu(xf, w_sh_gate, w_sh_up, w_sh_down).astype(jnp.float32)
    return (routed[:T] + shared).reshape(bsz, s, d).astype(h.dtype)


def setup_inputs(seed: int = 0) -> dict:
    key = jax.random.key(seed)
    ks = jax.random.split(key, 32)
    f32 = jnp.float32
    nrm = lambda k, shape, scale: jax.random.normal(k, shape, f32) * scale
    L_ = DEPTH
    dt0 = jnp.exp(jax.random.uniform(ks[9], (L_, SSM_HEADS), f32) * (math.log(0.1) - math.log(0.001)) + math.log(0.001))
    offs = jax.random.randint(ks[2], (BATCH, 1), 0, 1024, dtype=jnp.int32)
    return {
        "x": nrm(ks[0], (BATCH, SEQ, D_MODEL), 1.0),
        "c": nrm(ks[1], (BATCH, D_MODEL), 1.0),
        "positions": offs + jnp.arange(SEQ, dtype=jnp.int32)[None, :],
        "w_ada": nrm(ks[3], (L_, D_MODEL, 6 * D_MODEL), 0.5 * D_MODEL ** -0.5),
        "b_ada": nrm(ks[4], (L_, 6 * D_MODEL), 0.01),
        "norm1_w": 1.0 + nrm(ks[5], (L_, D_MODEL), 0.01),
        "w_in": nrm(ks[6], (L_, D_MODEL, IN_DIM), D_MODEL ** -0.5),
        "conv_w": nrm(ks[7], (L_, SSM_CONV, SSM_CONV_DIM), SSM_CONV ** -0.5),
        "conv_b": nrm(ks[8], (L_, SSM_CONV_DIM), 0.01),
        "dt_bias": dt0 + jnp.log(-jnp.expm1(-dt0)),
        "a_log": jnp.log(jax.random.uniform(ks[10], (L_, SSM_HEADS), f32, minval=1.0, maxval=16.0)),
        "d_skip": 1.0 + nrm(ks[11], (L_, SSM_HEADS), 0.01),
        "ssm_norm_w": 1.0 + nrm(ks[12], (L_, SSM_D_INNER), 0.01),
        "w_ssm_out": nrm(ks[13], (L_, SSM_D_INNER, D_MODEL), SSM_D_INNER ** -0.5),
        "w_ret_out": nrm(ks[14], (L_, RET_V_WIDTH, D_MODEL), RET_V_WIDTH ** -0.5),
        "w_out": nrm(ks[15], (L_, D_MODEL, D_MODEL), D_MODEL ** -0.5),
        "norm2_w": 1.0 + nrm(ks[16], (L_, D_MODEL), 0.01),
        "w_router": nrm(ks[17], (L_, D_MODEL, N_EXPERTS), D_MODEL ** -0.5),
        "router_bias": nrm(ks[18], (L_, N_EXPERTS), 0.01),
        "w_exp_gate": nrm(ks[19], (L_, N_EXPERTS, D_MODEL, EXPERT_DIM), D_MODEL ** -0.5),
        "w_exp_up": nrm(ks[20], (L_, N_EXPERTS, D_MODEL, EXPERT_DIM), D_MODEL ** -0.5),
        "w_exp_down": nrm(ks[21], (L_, N_EXPERTS, EXPERT_DIM, D_MODEL), EXPERT_DIM ** -0.5),
        "w_sh_gate": nrm(ks[22], (L_, D_MODEL, SHARED_DIM), D_MODEL ** -0.5),
        "w_sh_up": nrm(ks[23], (L_, D_MODEL, SHARED_DIM), D_MODEL ** -0.5),
        "w_sh_down": nrm(ks[24], (L_, SHARED_DIM, D_MODEL), SHARED_DIM ** -0.5),
        "final_norm_w": 1.0 + nrm(ks[25], (D_MODEL,), 0.01),
    }


def reference(x, c, positions, w_ada, b_ada, norm1_w, w_in, conv_w, conv_b, dt_bias, a_log, d_skip,
              ssm_norm_w, w_ssm_out, w_ret_out, w_out, norm2_w, w_router, router_bias,
              w_exp_gate, w_exp_up, w_exp_down, w_sh_gate, w_sh_up, w_sh_down, final_norm_w):
    for l in range(DEPTH):
        mod = jax.nn.silu(c) @ w_ada[l] + b_ada[l]
        shift1, scale1, gate1, shift2, scale2, gate2 = (m[:, None, :] for m in jnp.split(mod, 6, axis=-1))
        h = rms_norm(x, norm1_w[l]) * (1.0 + scale1) + shift1
        x = x + gate1 * hybrid_mixer(h, positions, w_in[l], conv_w[l], conv_b[l], dt_bias[l], a_log[l],
                                     d_skip[l], ssm_norm_w[l], w_ssm_out[l], w_ret_out[l], w_out[l])
        h = rms_norm(x, norm2_w[l]) * (1.0 + scale2) + shift2
        x = x + gate2 * moe(h, w_router[l], router_bias[l], w_exp_gate[l], w_exp_up[l], w_exp_down[l],
                            w_sh_gate[l], w_sh_up[l], w_sh_down[l])
    return rms_norm(x, final_norm_w)
```

```python
import functools

import numpy as np
import jax
import jax.numpy as jnp
from jax import lax
from jax.experimental import pallas as pl
from jax.experimental.pallas import tpu as pltpu

F32 = jnp.float32
BF16 = jnp.bfloat16

EPS = 1e-6
D_MODEL = 1024
SSM_D_INNER = 2048
SSM_HEADDIM = 64
SSM_HEADS = 32
SSM_GROUPS = 4
SSM_STATE = 128
SSM_CONV = 4
SSM_CONV_DIM = SSM_D_INNER + 2 * SSM_GROUPS * SSM_STATE
GROUP_WIDTH = SSM_D_INNER // SSM_GROUPS
RET_HEADS = 4
RET_QK_DIM = 256
RET_V_DIM = 512
RET_QK_WIDTH = RET_HEADS * RET_QK_DIM
RET_V_WIDTH = RET_HEADS * RET_V_DIM
ROPE_BASE = 10000.0
ROPE_HALF = RET_QK_DIM // 2
N_EXPERTS = 256
TOP_K = 8
N_ROUTE_GROUPS = 8
TOPK_ROUTE_GROUPS = 4
ROUTE_GROUP_SIZE = N_EXPERTS // N_ROUTE_GROUPS
EXPERT_DIM = 256
ROUTED_SCALE = 2.5

CHUNK = 128
LANES = 128
SUBLANES = 8
DT_PAD = LANES
EXPERT_BLOCK = 128
ROUTE_TILE = 256
VMEM_LIMIT = 56 * 1024 * 1024


def _silu(x):
    return x / (1.0 + jnp.exp(-x))


def _sigmoid(x):
    return 1.0 / (1.0 + jnp.exp(-x))


def _split3(a):
    hi = a.astype(BF16)
    r1 = a - hi.astype(F32)
    mid = r1.astype(BF16)
    lo = (r1 - mid.astype(F32)).astype(BF16)
    return hi, mid, lo


def _dot(a, b):
    return jnp.dot(a, b, preferred_element_type=F32)


def _dot_nt(a, b):
    return lax.dot_general(a, b, (((1,), (1,)), ((), ())), preferred_element_type=F32)


def _params(sem):
    return pltpu.CompilerParams(dimension_semantics=sem, vmem_limit_bytes=VMEM_LIMIT)


def _mod_kernel(c_ref, w_ref, b_ref, o_ref):
    o_ref[...] = _dot(_silu(c_ref[...]), w_ref[...]) + b_ref[...]


def _mod(c, w_ada, b_ada):
    bsz, d = c.shape
    n = w_ada.shape[1]
    return pl.pallas_call(
        _mod_kernel,
        grid=(n // d,),
        in_specs=[pl.BlockSpec((bsz, d), lambda j: (0, 0)),
                  pl.BlockSpec((d, d), lambda j: (0, j)),
                  pl.BlockSpec((1, d), lambda j: (0, j))],
        out_specs=pl.BlockSpec((bsz, d), lambda j: (0, j)),
        out_shape=jax.ShapeDtypeStruct((bsz, n), F32),
        compiler_params=_params(("arbitrary",)),
        name="mod",
    )(c, w_ada, b_ada.reshape(1, n))


def _rms_mod(x, nw, scale, shift):
    ms = jnp.mean(x * x, axis=-1, keepdims=True)
    return (x * lax.rsqrt(ms + EPS) * nw) * (1.0 + scale) + shift


def _inproj_kernel(x_ref, nw_ref, sc_ref, sh_ref, w_ref, *o_refs, widths):
    hb = _rms_mod(x_ref[0], nw_ref[...], sc_ref[0], sh_ref[0]).astype(BF16)
    off = 0
    for o_ref, wd in zip(o_refs, widths):
        for c0 in range(0, wd, 512):
            cw = min(512, wd - c0)
            o_ref[0, :, c0:c0 + cw] = _dot(hb, w_ref[:, off + c0:off + c0 + cw]).astype(o_ref.dtype)
        off += wd


def _inproj(x, nw, scale, shift, w, widths, dtypes, tm):
    bsz, s, d = x.shape
    vec = pl.BlockSpec((1, 1, d), lambda b, i: (b, 0, 0))
    return pl.pallas_call(
        functools.partial(_inproj_kernel, widths=widths),
        grid=(bsz, s // tm),
        in_specs=[pl.BlockSpec((1, tm, d), lambda b, i: (b, i, 0)),
                  pl.BlockSpec((1, d), lambda b, i: (0, 0)),
                  vec, vec,
                  pl.BlockSpec(w.shape, lambda b, i: (0, 0))],
        out_specs=[pl.BlockSpec((1, tm, wd), lambda b, i: (b, i, 0)) for wd in widths],
        out_shape=[jax.ShapeDtypeStruct((bsz, s, wd), dt) for wd, dt in zip(widths, dtypes)],
        compiler_params=_params(("parallel", "parallel")),
        name="inproj",
    )(x, nw, scale, shift, w)


def _rope_kernel(pos_ref, inv_ref, cos_ref, sin_ref):
    ang = pos_ref[0].astype(F32) * inv_ref[...]
    cos_ref[0] = jnp.cos(ang)
    sin_ref[0] = jnp.sin(ang)


def _rope(positions, tm):
    bsz, s = positions.shape
    inv = (1.0 / (ROPE_BASE ** (jnp.arange(ROPE_HALF, dtype=F32) / ROPE_HALF))).reshape(1, ROPE_HALF)
    spec = pl.BlockSpec((1, tm, ROPE_HALF), lambda b, i: (b, i, 0))
    return pl.pallas_call(
        _rope_kernel,
        grid=(bsz, s // tm),
        in_specs=[pl.BlockSpec((1, tm, 1), lambda b, i: (b, i, 0)),
                  pl.BlockSpec((1, ROPE_HALF), lambda b, i: (0, 0))],
        out_specs=[spec, spec],
        out_shape=[jax.ShapeDtypeStruct((bsz, s, ROPE_HALF), F32)] * 2,
        compiler_params=_params(("parallel", "parallel")),
        name="rope",
    )(positions.reshape(bsz, s, 1), inv)


def _ssd_kernel(xbc_ref, z_ref, dt_ref, cw_ref, cb_ref, dtb_ref, a_ref, dx_ref, nw_ref, r_ref, tril_ref,
                o_ref, tail_ref, state_ref, y_ref):
    L = CHUNK

    @pl.when(pl.program_id(1) == 0)
    def _():
        tail_ref[...] = jnp.zeros_like(tail_ref)
        state_ref[...] = jnp.zeros_like(state_ref)

    u = xbc_ref[0].astype(F32)
    tail = tail_ref[...]
    row8 = lax.broadcasted_iota(jnp.int32, (SUBLANES, 1), 0)
    acc = u * cw_ref[SSM_CONV - 1:SSM_CONV, :] + cb_ref[...]
    for d in range(1, SSM_CONV):
        sh = pltpu.roll(u, d, axis=0)
        head = jnp.where(row8 < d, pltpu.roll(tail, d, axis=0), sh[0:SUBLANES])
        sh = jnp.concatenate([head, sh[SUBLANES:]], axis=0)
        acc = acc + sh * cw_ref[SSM_CONV - 1 - d:SSM_CONV - d, :]
    tail_ref[...] = u[L - SUBLANES:L]
    act = _silu(acc)
    xs = act[:, :SSM_D_INNER]
    xs_b = xs.astype(BF16)
    bm = act[:, SSM_D_INNER:SSM_D_INNER + SSM_GROUPS * SSM_STATE]
    cm_b = act[:, SSM_D_INNER + SSM_GROUPS * SSM_STATE:].astype(BF16)

    dt_raw = dt_ref[0] + dtb_ref[...]
    dtv = jnp.maximum(dt_raw, 0.0) + jnp.log1p(jnp.exp(-jnp.abs(dt_raw)))
    a = dtv * a_ref[...]
    tril = tril_ref[...]
    a_hi, a_mid, a_lo = _split3(a)
    acum = _dot(tril, a_hi) + _dot(tril, a_mid) + _dot(tril, a_lo)
    acum_t = acum.T
    dt_t = dtv.T
    last = acum[L - 1:L, :]
    r = r_ref[...]
    e1x = _dot(jnp.exp(acum).astype(BF16), r)
    e2x = _dot((jnp.exp(last - acum) * dtv).astype(BF16), r)
    l_hi, l_mid, l_lo = _split3(jnp.broadcast_to(jnp.exp(last), (SUBLANES, LANES)))
    decay_x = (_dot(l_hi, r) + _dot(l_mid, r) + _dot(l_lo, r))[0:1, :]

    causal = (lax.broadcasted_iota(jnp.int32, (L, L), 0) >= lax.broadcasted_iota(jnp.int32, (L, L), 1))
    lane_lo = lax.broadcasted_iota(jnp.int32, (L, LANES), 1) < SSM_HEADDIM
    heads_per_group = SSM_HEADS // SSM_GROUPS
    for g in range(SSM_GROUPS):
        gs = slice(g * GROUP_WIDTH, (g + 1) * GROUP_WIDTH)
        bg = bm[:, g * SSM_STATE:(g + 1) * SSM_STATE]
        cg_b = cm_b[:, g * SSM_STATE:(g + 1) * SSM_STATE]
        cb = _dot_nt(cg_b, bg.astype(BF16))
        for pair in range(heads_per_group // 2):
            p = g * (heads_per_group // 2) + pair
            xp = xs_b[:, p * LANES:(p + 1) * LANES]
            y_pair = None
            for sub in range(2):
                h = 2 * p + sub
                seg = jnp.broadcast_to(acum[:, h:h + 1], (L, L)) - acum_t[h:h + 1, :]
                w = cb * jnp.exp(jnp.where(causal, seg, -jnp.inf)) * dt_t[h:h + 1, :]
                xh = jnp.where(lane_lo if sub == 0 else jnp.logical_not(lane_lo), xp, jnp.zeros_like(xp))
                yh = _dot(w.astype(BF16), xh)
                y_pair = yh if y_pair is None else y_pair + yh
            y_ref[:, p * LANES:(p + 1) * LANES] = y_pair
        st = state_ref[g]
        y_inter = _dot(cg_b, st.astype(BF16)) * e1x[:, gs]
        y_ref[:, gs] = y_ref[:, gs] + y_inter
        upd = _dot(bg.T.astype(BF16), (xs[:, gs] * e2x[:, gs]).astype(BF16))
        state_ref[g] = st * decay_x[:, gs] + upd

    y = (y_ref[...] + xs * dx_ref[...]) * _silu(z_ref[0].astype(F32))
    for g in range(SSM_GROUPS):
        gs = slice(g * GROUP_WIDTH, (g + 1) * GROUP_WIDTH)
        yg = y[:, gs]
        ms = jnp.mean(yg * yg, axis=-1, keepdims=True)
        o_ref[0, :, gs] = (yg * lax.rsqrt(ms + EPS) * nw_ref[:, gs]).astype(o_ref.dtype)


def _ssd(xbc, z, dt, conv_w, conv_b, dt_bias, a_log, d_skip, ssm_norm_w):
    bsz, s, _ = xbc.shape
    pad = DT_PAD - SSM_HEADS
    dtb = jnp.pad(dt_bias.astype(F32), (0, pad)).reshape(1, DT_PAD)
    a_neg = jnp.pad(-jnp.exp(a_log.astype(F32)), (0, pad)).reshape(1, DT_PAD)
    dx = jnp.repeat(d_skip.astype(F32), SSM_HEADDIM).reshape(1, SSM_D_INNER)
    expand = (jnp.arange(DT_PAD)[:, None] == (jnp.arange(SSM_D_INNER)[None, :] // SSM_HEADDIM)).astype(BF16)
    tril = (jnp.arange(CHUNK)[:, None] >= jnp.arange(CHUNK)[None, :]).astype(BF16)

    def seq(width):
        return pl.BlockSpec((1, CHUNK, width), lambda b, c: (b, c, 0))

    def const(shape):
        return pl.BlockSpec(shape, lambda b, c: (0,) * len(shape))

    return pl.pallas_call(
        _ssd_kernel,
        grid=(bsz, s // CHUNK),
        in_specs=[seq(SSM_CONV_DIM), seq(SSM_D_INNER), seq(DT_PAD),
                  const((SSM_CONV, SSM_CONV_DIM)), const((1, SSM_CONV_DIM)), const((1, DT_PAD)), const((1, DT_PAD)),
                  const((1, SSM_D_INNER)), const((1, SSM_D_INNER)), const((DT_PAD, SSM_D_INNER)),
                  const((CHUNK, CHUNK))],
        out_specs=seq(SSM_D_INNER),
        out_shape=jax.ShapeDtypeStruct((bsz, s, SSM_D_INNER), BF16),
        scratch_shapes=[pltpu.VMEM((SUBLANES, SSM_CONV_DIM), F32),
                        pltpu.VMEM((SSM_GROUPS, SSM_STATE, GROUP_WIDTH), F32),
                        pltpu.VMEM((CHUNK, SSM_D_INNER), F32)],
        compiler_params=_params(("parallel", "arbitrary")),
        name="ssd",
    )(xbc, z, dt, conv_w.astype(F32), conv_b.astype(F32).reshape(1, SSM_CONV_DIM), dtb, a_neg, dx,
      ssm_norm_w.astype(F32).reshape(1, SSM_D_INNER), expand, tril)


def _ret_consts():
    lg = np.log1p(-(2.0 ** (-5.0 - np.arange(RET_HEADS, dtype=np.float64))))
    idx = np.arange(CHUNK, dtype=np.float64)
    rel = idx[:, None] - idx[None, :]
    intra = np.where(rel >= 0, np.exp(np.maximum(rel, 0.0)[None] * lg[:, None, None]), 0.0)
    qd = np.exp((idx + 1.0)[None, :] * lg[:, None])
    kd = np.exp((CHUNK - 1.0 - idx)[None, :] * lg[:, None]) * (RET_QK_DIM ** -0.5)
    qd = np.broadcast_to(qd[:, :, None], (RET_HEADS, CHUNK, RET_QK_DIM))
    kd = np.broadcast_to(kd[:, :, None], (RET_HEADS, CHUNK, RET_QK_DIM))
    chunk_decay = tuple(float(v) for v in np.exp(CHUNK * lg))
    return (jnp.asarray(intra, F32), jnp.asarray(qd, F32), jnp.asarray(kd, F32), chunk_decay)


def _ret_kernel(q_ref, k_ref, v_ref, g_ref, cos_ref, sin_ref, intra_ref, qd_ref, kd_ref, o_ref, state_ref, *,
                chunk_decay):
    @pl.when(pl.program_id(1) == 0)
    def _():
        state_ref[...] = jnp.zeros_like(state_ref)

    cos = cos_ref[0]
    sin = sin_ref[0]

    def rot(t):
        t1, t2 = t[:, :ROPE_HALF], t[:, ROPE_HALF:]
        return jnp.concatenate([t1 * cos - t2 * sin, t1 * sin + t2 * cos], axis=1)

    for h in range(RET_HEADS):
        qs = slice(h * RET_QK_DIM, (h + 1) * RET_QK_DIM)
        vs = slice(h * RET_V_DIM, (h + 1) * RET_V_DIM)
        qr = rot(q_ref[0, :, qs].astype(F32))
        kr = rot(k_ref[0, :, qs].astype(F32))
        vh = v_ref[0, :, vs]
        s = _dot_nt(qr.astype(BF16), (kr * (RET_QK_DIM ** -0.5)).astype(BF16)) * intra_ref[h]
        st = state_ref[h]
        y = _dot(s.astype(BF16), vh) + _dot((qr * qd_ref[h]).astype(BF16), st.astype(BF16))
        state_ref[h] = st * chunk_decay[h] + _dot((kr * kd_ref[h]).T.astype(BF16), vh)
        mu = jnp.mean(y, axis=-1, keepdims=True)
        yc = y - mu
        var = jnp.mean(yc * yc, axis=-1, keepdims=True)
        o_ref[0, :, vs] = (_silu(g_ref[0, :, vs].astype(F32)) * (yc * lax.rsqrt(var + EPS))).astype(o_ref.dtype)


def _ret(q, k, v, g, cos, sin):
    bsz, s, _ = q.shape
    intra, qd, kd, chunk_decay = _ret_consts()

    def seq(width):
        return pl.BlockSpec((1, CHUNK, width), lambda b, c: (b, c, 0))

    def const(shape):
        return pl.BlockSpec(shape, lambda b, c: (0,) * len(shape))

    return pl.pallas_call(
        functools.partial(_ret_kernel, chunk_decay=chunk_decay),
        grid=(bsz, s // CHUNK),
        in_specs=[seq(RET_QK_WIDTH), seq(RET_QK_WIDTH), seq(RET_V_WIDTH), seq(RET_V_WIDTH),
                  seq(ROPE_HALF), seq(ROPE_HALF),
                  const(intra.shape), const(qd.shape), const(kd.shape)],
        out_specs=seq(RET_V_WIDTH),
        out_shape=jax.ShapeDtypeStruct((bsz, s, RET_V_WIDTH), BF16),
        scratch_shapes=[pltpu.VMEM((RET_HEADS, RET_QK_DIM, RET_V_DIM), F32)],
        compiler_params=_params(("parallel", "arbitrary")),
        name="ret",
    )(q, k, v, g, cos, sin, intra, qd, kd)


def _merge_kernel(ys_ref, yr_ref, gs_ref, gr_ref, x_ref, g1_ref, sc2_ref, sh2_ref, n2w_ref,
                  wso_ref, wro_ref, wo_ref, wrh_ref, wrl_ref, x1_ref, h2_ref, sct_ref):
    y_ssm = _dot(ys_ref[0], wso_ref[...])
    y_ret = _dot(yr_ref[0], wro_ref[...])
    merged = _sigmoid(gs_ref[0].astype(F32)) * y_ssm + _sigmoid(gr_ref[0].astype(F32)) * y_ret
    x1 = x_ref[0] + g1_ref[0] * _dot(merged.astype(BF16), wo_ref[...])
    x1_ref[0] = x1
    h2 = _rms_mod(x1, n2w_ref[...], sc2_ref[0], sh2_ref[0])
    h_hi = h2.astype(BF16)
    h2_ref[0] = h_hi
    h_lo = (h2 - h_hi.astype(F32)).astype(BF16)
    wrh = wrh_ref[...]
    logits_t = _dot_nt(wrh, h_hi) + _dot_nt(wrh, h_lo) + _dot_nt(wrl_ref[...], h_hi)
    sct_ref[...] = _sigmoid(logits_t)


def _merge(ys, yr, gs, gr, x, gate1, scale2, shift2, norm2_w, w_ssm_out, w_ret_out, w_out, w_router, tm):
    bsz, s, d = x.shape
    nt = s // tm
    wrt = w_router.astype(F32).T
    wrh = wrt.astype(BF16)
    wrl = (wrt - wrh.astype(F32)).astype(BF16)

    def seq(width):
        return pl.BlockSpec((1, tm, width), lambda b, i: (b, i, 0))

    def const(shape):
        return pl.BlockSpec(shape, lambda b, i: (0,) * len(shape))

    vec = pl.BlockSpec((1, 1, d), lambda b, i: (b, 0, 0))
    return pl.pallas_call(
        _merge_kernel,
        grid=(bsz, nt),
        in_specs=[seq(SSM_D_INNER), seq(RET_V_WIDTH), seq(d), seq(d), seq(d), vec, vec, vec, const((1, d)),
                  const((SSM_D_INNER, d)), const((RET_V_WIDTH, d)), const((d, d)),
                  const((N_EXPERTS, d)), const((N_EXPERTS, d))],
        out_specs=[seq(d), seq(d), pl.BlockSpec((N_EXPERTS, tm), lambda b, i: (0, b * nt + i))],
        out_shape=[jax.ShapeDtypeStruct((bsz, s, d), F32), jax.ShapeDtypeStruct((bsz, s, d), BF16),
                   jax.ShapeDtypeStruct((N_EXPERTS, bsz * s), F32)],
        compiler_params=_params(("parallel", "parallel")),
        name="merge",
    )(ys, yr, gs, gr, x, gate1, scale2, shift2, norm2_w.astype(F32).reshape(1, d),
      w_ssm_out.astype(BF16), w_ret_out.astype(BF16), w_out.astype(BF16), wrh, wrl)


def _route_kernel(sc_ref, bias_ref, upper_ref, ones_ref, idx_ref, wts_ref, rank_ref, cnt_ref, carry_ref):
    @pl.when(pl.program_id(0) == 0)
    def _():
        carry_ref[...] = jnp.zeros_like(carry_ref)

    tl = sc_ref.shape[1]
    neg = -jnp.inf
    scores = sc_ref[...]
    sel = scores + bias_ref[...]

    def first_argmax(vals, iota, n):
        m = jnp.max(vals, axis=0, keepdims=True)
        return jnp.min(jnp.where(vals == m, iota, float(n)), axis=0, keepdims=True), m

    io_g = lax.broadcasted_iota(jnp.int32, (ROUTE_GROUP_SIZE, tl), 0).astype(F32)
    rows = []
    for g in range(N_ROUTE_GROUPS):
        blk = sel[g * ROUTE_GROUP_SIZE:(g + 1) * ROUTE_GROUP_SIZE]
        i1, m1 = first_argmax(blk, io_g, ROUTE_GROUP_SIZE)
        m2 = jnp.max(jnp.where(io_g == i1, neg, blk), axis=0, keepdims=True)
        rows.append(m1 + m2)
    gsc = jnp.concatenate(rows, axis=0)
    io_8 = lax.broadcasted_iota(jnp.int32, (N_ROUTE_GROUPS, tl), 0).astype(F32)
    chosen = jnp.zeros((N_ROUTE_GROUPS, tl), F32)
    for _ in range(TOPK_ROUTE_GROUPS):
        i, _m = first_argmax(gsc, io_8, N_ROUTE_GROUPS)
        hit = io_8 == i
        chosen = jnp.where(hit, 1.0, chosen)
        gsc = jnp.where(hit, neg, gsc)
    msel = jnp.concatenate(
        [jnp.where(chosen[g:g + 1] > 0.5, sel[g * ROUTE_GROUP_SIZE:(g + 1) * ROUTE_GROUP_SIZE], neg)
         for g in range(N_ROUTE_GROUPS)], axis=0)

    io_e = lax.broadcasted_iota(jnp.int32, (N_EXPERTS, tl), 0).astype(F32)
    multi = jnp.zeros((N_EXPERTS, tl), F32)
    idx_rows, sc_rows = [], []
    for _ in range(TOP_K):
        i, _m = first_argmax(msel, io_e, N_EXPERTS)
        hit = io_e == i
        sc_rows.append(jnp.sum(jnp.where(hit, scores, 0.0), axis=0, keepdims=True))
        msel = jnp.where(hit, neg, msel)
        multi = jnp.where(hit, 1.0, multi)
        idx_rows.append(i)
    den = sc_rows[0]
    for r in sc_rows[1:]:
        den = den + r
    den = den + 1e-20
    wts_ref[...] = jnp.concatenate([r / den * ROUTED_SCALE for r in sc_rows], axis=0)
    idx_ref[...] = jnp.concatenate(idx_rows, axis=0).astype(jnp.int32)

    multi_b = multi.astype(BF16)
    carry = carry_ref[...]
    rank_full = _dot(multi_b, upper_ref[...]) + carry
    rank_ref[...] = jnp.concatenate(
        [jnp.sum(jnp.where(io_e == i, rank_full, 0.0), axis=0, keepdims=True) for i in idx_rows],
        axis=0).astype(jnp.int32)
    carry = carry + _dot(multi_b, ones_ref[...])
    carry_ref[...] = carry
    cnt_ref[...] = carry[:, :LANES]


def _route(scores_t, router_bias):
    e, t = scores_t.shape
    tl = min(ROUTE_TILE, t)
    bias = jnp.broadcast_to(router_bias.astype(F32).reshape(e, 1), (e, tl))
    upper = (jnp.arange(tl)[:, None] < jnp.arange(tl)[None, :]).astype(BF16)
    ones = jnp.ones((tl, tl), BF16)
    tok = pl.BlockSpec((TOP_K, tl), lambda i: (0, i))

    def const(shape):
        return pl.BlockSpec(shape, lambda i: (0,) * len(shape))

    return pl.pallas_call(
        _route_kernel,
        grid=(t // tl,),
        in_specs=[pl.BlockSpec((e, tl), lambda i: (0, i)), const((e, tl)), const((tl, tl)), const((tl, tl))],
        out_specs=[tok, tok, tok, const((e, LANES))],
        out_shape=[jax.ShapeDtypeStruct((TOP_K, t), jnp.int32), jax.ShapeDtypeStruct((TOP_K, t), F32),
                   jax.ShapeDtypeStruct((TOP_K, t), jnp.int32), jax.ShapeDtypeStruct((e, LANES), F32)],
        scratch_shapes=[pltpu.VMEM((e, tl), F32)],
        compiler_params=_params(("arbitrary",)),
        name="route",
    )(scores_t, bias, upper, ones)


def _expert_kernel(be_ref, x_ref, wg_ref, wu_ref, wd_ref, o_ref):
    del be_ref
    x = x_ref[...]
    g = _dot(x, wg_ref[0].astype(BF16))
    u = _dot(x, wu_ref[0].astype(BF16))
    o_ref[...] = _dot((_silu(g) * u).astype(BF16), wd_ref[0].astype(BF16)).astype(o_ref.dtype)


def _experts(block_expert, xs, w_gate, w_up, w_down):
    n_slots, d = xs.shape
    n_blocks = n_slots // EXPERT_BLOCK
    f = w_gate.shape[-1]
    grid_spec = pltpu.PrefetchScalarGridSpec(
        num_scalar_prefetch=1,
        grid=(n_blocks,),
        in_specs=[pl.BlockSpec((EXPERT_BLOCK, d), lambda b, be: (b, 0)),
                  pl.BlockSpec((1, d, f), lambda b, be: (be[b], 0, 0)),
                  pl.BlockSpec((1, d, f), lambda b, be: (be[b], 0, 0)),
                  pl.BlockSpec((1, f, d), lambda b, be: (be[b], 0, 0))],
        out_specs=pl.BlockSpec((EXPERT_BLOCK, d), lambda b, be: (b, 0)),
    )
    return pl.pallas_call(
        _expert_kernel,
        grid_spec=grid_spec,
        out_shape=jax.ShapeDtypeStruct((n_slots, d), BF16),
        compiler_params=_params(("arbitrary",)),
        name="experts",
    )(block_expert, xs, w_gate, w_up, w_down)


def _final_kernel(x1_ref, h2_ref, routed_ref, g2_ref, wsg_ref, wsu_ref, wsd_ref, fnw_ref, o_ref):
    h = h2_ref[0]
    act = (_silu(_dot(h, wsg_ref[...])) * _dot(h, wsu_ref[...])).astype(BF16)
    shared = _dot(act, wsd_ref[...])
    x2 = x1_ref[0] + g2_ref[0] * (routed_ref[0] + shared)
    ms = jnp.mean(x2 * x2, axis=-1, keepdims=True)
    o_ref[0] = x2 * lax.rsqrt(ms + EPS) * fnw_ref[...]


def _final(x1, h2, routed, gate2, w_sh_gate, w_sh_up, w_sh_down, final_norm_w, tm):
    bsz, s, d = x1.shape
    f = w_sh_gate.shape[-1]
    seq = pl.BlockSpec((1, tm, d), lambda b, i: (b, i, 0))

    def const(shape):
        return pl.BlockSpec(shape, lambda b, i: (0,) * len(shape))

    return pl.pallas_call(
        _final_kernel,
        grid=(bsz, s // tm),
        in_specs=[seq, seq, seq, pl.BlockSpec((1, 1, d), lambda b, i: (b, 0, 0)),
                  const((d, f)), const((d, f)), const((f, d)), const((1, d))],
        out_specs=seq,
        out_shape=jax.ShapeDtypeStruct((bsz, s, d), F32),
        compiler_params=_params(("parallel", "parallel")),
        name="final",
    )(x1, h2, routed, gate2, w_sh_gate.astype(BF16), w_sh_up.astype(BF16), w_sh_down.astype(BF16),
      final_norm_w.astype(F32).reshape(1, d))


def _layer(x, c, positions, w_ada, b_ada, norm1_w, w_in, conv_w, conv_b, dt_bias, a_log, d_skip, ssm_norm_w,
           w_ssm_out, w_ret_out, w_out, norm2_w, w_router, router_bias, w_exp_gate, w_exp_up, w_exp_down,
           w_sh_gate, w_sh_up, w_sh_down):
    bsz, s, d = x.shape
    t = bsz * s
    tm = min(256, s)

    mod = _mod(c, w_ada, b_ada)
    shift1, scale1, gate1, shift2, scale2, gate2 = (mod[:, i * d:(i + 1) * d].reshape(bsz, 1, d) for i in range(6))

    bounds = np.cumsum((0, SSM_D_INNER, SSM_CONV_DIM, SSM_HEADS, RET_QK_WIDTH, RET_QK_WIDTH, RET_V_WIDTH,
                        RET_V_WIDTH, d, d))
    wz, wxbc, wdt, wq, wk, wv, wg, wgs, wgr = (w_in[:, bounds[i]:bounds[i + 1]] for i in range(9))
    nw1 = norm1_w.astype(F32).reshape(1, d)
    w_a = jnp.concatenate([wxbc, wz, jnp.pad(wdt, ((0, 0), (0, DT_PAD - SSM_HEADS)))], axis=1).astype(BF16)
    w_b = jnp.concatenate([wq, wk, wv, wg, wgs, wgr], axis=1).astype(BF16)
    xbc, z, dt = _inproj(x, nw1, scale1, shift1, w_a, (SSM_CONV_DIM, SSM_D_INNER, DT_PAD), (BF16, BF16, F32), tm)
    q, k, v, g, gs, gr = _inproj(x, nw1, scale1, shift1, w_b,
                                 (RET_QK_WIDTH, RET_QK_WIDTH, RET_V_WIDTH, RET_V_WIDTH, d, d), (BF16,) * 6, tm)

    ys = _ssd(xbc, z, dt, conv_w, conv_b, dt_bias, a_log, d_skip, ssm_norm_w)
    cos, sin = _rope(positions, tm)
    yr = _ret(q, k, v, g, cos, sin)
    x1, h2, scores_t = _merge(ys, yr, gs, gr, x, gate1, scale2, shift2, norm2_w, w_ssm_out, w_ret_out, w_out,
                              w_router, tm)

    idx_t, wts_t, rank_t, cnt = _route(scores_t, router_bias)

    counts = cnt[:, 0].astype(jnp.int32)
    padded = (counts + EXPERT_BLOCK - 1) // EXPERT_BLOCK * EXPERT_BLOCK
    padded_end = jnp.cumsum(padded)
    padded_start = padded_end - padded
    n_blocks = -(-(t * TOP_K) // EXPERT_BLOCK) + N_EXPERTS
    n_slots = n_blocks * EXPERT_BLOCK
    dest_t = padded_start[idx_t] + rank_t
    block_expert = jnp.minimum(
        jnp.searchsorted(padded_end, jnp.arange(n_blocks, dtype=jnp.int32) * EXPERT_BLOCK, side='right'),
        N_EXPERTS - 1).astype(jnp.int32)
    tok = jnp.broadcast_to(jnp.arange(t, dtype=jnp.int32)[None, :], (TOP_K, t))
    slot_tok = jnp.full((n_slots,), t, jnp.int32).at[dest_t.reshape(-1)].set(tok.reshape(-1))
    h2f = h2.reshape(t, d)
    xs = jnp.concatenate([h2f, jnp.zeros((1, d), h2f.dtype)], axis=0)[slot_tok]
    y_sorted = _experts(block_expert, xs, w_exp_gate, w_exp_up, w_exp_down)
    routed = jnp.zeros((t, d), F32)
    for kk in range(TOP_K):
        routed = routed + y_sorted[dest_t[kk]].astype(F32) * wts_t[kk][:, None]

    return x1, h2, routed.reshape(bsz, s, d), gate2


def kernel(x, c, positions, w_ada, b_ada, norm1_w, w_in, conv_w, conv_b, dt_bias, a_log, d_skip, ssm_norm_w,
           w_ssm_out, w_ret_out, w_out, norm2_w, w_router, router_bias, w_exp_gate, w_exp_up, w_exp_down,
           w_sh_gate, w_sh_up, w_sh_down, final_norm_w):
    assert w_ada.shape[0] == 1, "the final rmsnorm is fused into the single layer's last kernel"
    tm = min(256, x.shape[1])
    x1, h2, routed, gate2 = _layer(
        x, c, positions, w_ada[0], b_ada[0], norm1_w[0], w_in[0], conv_w[0], conv_b[0], dt_bias[0], a_log[0],
        d_skip[0], ssm_norm_w[0], w_ssm_out[0], w_ret_out[0], w_out[0], norm2_w[0], w_router[0],
        router_bias[0], w_exp_gate[0], w_exp_up[0], w_exp_down[0], w_sh_gate[0], w_sh_up[0], w_sh_down[0])
    return _final(x1, h2, routed, gate2, w_sh_gate[0], w_sh_up[0], w_sh_down[0], final_norm_w, tm)
```

```python
import functools

import numpy as np
import jax
import jax.numpy as jnp
from jax import lax
from jax.experimental import pallas as pl
from jax.experimental.pallas import tpu as pltpu
from jax.experimental.pallas import tpu_sc as plsc

F32 = jnp.float32
BF16 = jnp.bfloat16

EPS = 1e-6
D_MODEL = 1024
SSM_D_INNER = 2048
SSM_HEADDIM = 64
SSM_HEADS = 32
SSM_GROUPS = 4
SSM_STATE = 128
SSM_CONV = 4
SSM_CONV_DIM = SSM_D_INNER + 2 * SSM_GROUPS * SSM_STATE
GROUP_WIDTH = SSM_D_INNER // SSM_GROUPS
RET_HEADS = 4
RET_QK_DIM = 256
RET_V_DIM = 512
RET_QK_WIDTH = RET_HEADS * RET_QK_DIM
RET_V_WIDTH = RET_HEADS * RET_V_DIM
ROPE_BASE = 10000.0
ROPE_HALF = RET_QK_DIM // 2
N_EXPERTS = 256
TOP_K = 8
N_ROUTE_GROUPS = 8
TOPK_ROUTE_GROUPS = 4
ROUTE_GROUP_SIZE = N_EXPERTS // N_ROUTE_GROUPS
EXPERT_DIM = 256
ROUTED_SCALE = 2.5

CHUNK = 128
LANES = 128
SUBLANES = 8
DT_PAD = LANES
EXPERT_BLOCK = 128
ROUTE_TILE = 256
VMEM_LIMIT = 56 * 1024 * 1024
PACKED = D_MODEL // 2
SC_CORES = 2
SC_SUBCORES = 16
SC_WINDOW = 64


def _silu(x):
    return x / (1.0 + jnp.exp(-x))


def _sigmoid(x):
    return 1.0 / (1.0 + jnp.exp(-x))


def _split3(a):
    hi = a.astype(BF16)
    r1 = a - hi.astype(F32)
    mid = r1.astype(BF16)
    lo = (r1 - mid.astype(F32)).astype(BF16)
    return hi, mid, lo


def _pack_rows(x):
    w = x.shape[1] // 2
    hi = lax.bitcast_convert_type(x[:, :w].astype(BF16).astype(F32), jnp.uint32)
    lo = lax.bitcast_convert_type(x[:, w:].astype(BF16).astype(F32), jnp.uint32)
    return lax.bitcast_convert_type(hi | (lo >> 16), jnp.int32)


def _unpack_rows(p):
    u = lax.bitcast_convert_type(p, jnp.uint32)
    hi = lax.bitcast_convert_type(u & jnp.uint32(0xFFFF0000), F32)
    lo = lax.bitcast_convert_type(u << 16, F32)
    return jnp.concatenate([hi, lo], axis=1).astype(BF16)


def _dot(a, b):
    return jnp.dot(a, b, preferred_element_type=F32)


def _dot_nt(a, b):
    return lax.dot_general(a, b, (((1,), (1,)), ((), ())), preferred_element_type=F32)


def _params(sem):
    return pltpu.CompilerParams(dimension_semantics=sem, vmem_limit_bytes=VMEM_LIMIT)


def _mod_kernel(c_ref, w_ref, b_ref, o_ref):
    o_ref[...] = _dot(_silu(c_ref[...]), w_ref[...]) + b_ref[...]


def _mod(c, w_ada, b_ada):
    bsz, d = c.shape
    n = w_ada.shape[1]
    return pl.pallas_call(
        _mod_kernel,
        grid=(n // d,),
        in_specs=[pl.BlockSpec((bsz, d), lambda j: (0, 0)),
                  pl.BlockSpec((d, d), lambda j: (0, j)),
                  pl.BlockSpec((1, d), lambda j: (0, j))],
        out_specs=pl.BlockSpec((bsz, d), lambda j: (0, j)),
        out_shape=jax.ShapeDtypeStruct((bsz, n), F32),
        compiler_params=_params(("arbitrary",)),
        name="mod",
    )(c, w_ada, b_ada.reshape(1, n))


def _rms_mod(x, nw, scale, shift):
    ms = jnp.mean(x * x, axis=-1, keepdims=True)
    return (x * lax.rsqrt(ms + EPS) * nw) * (1.0 + scale) + shift


def _inproj_kernel(x_ref, nw_ref, sc_ref, sh_ref, w_ref, *o_refs, widths):
    hb = _rms_mod(x_ref[0], nw_ref[...], sc_ref[0], sh_ref[0]).astype(BF16)
    off = 0
    for o_ref, wd in zip(o_refs, widths):
        for c0 in range(0, wd, 512):
            cw = min(512, wd - c0)
            o_ref[0, :, c0:c0 + cw] = _dot(hb, w_ref[:, off + c0:off + c0 + cw]).astype(o_ref.dtype)
        off += wd


def _inproj(x, nw, scale, shift, w, widths, dtypes, tm):
    bsz, s, d = x.shape
    vec = pl.BlockSpec((1, 1, d), lambda b, i: (b, 0, 0))
    return pl.pallas_call(
        functools.partial(_inproj_kernel, widths=widths),
        grid=(bsz, s // tm),
        in_specs=[pl.BlockSpec((1, tm, d), lambda b, i: (b, i, 0)),
                  pl.BlockSpec((1, d), lambda b, i: (0, 0)),
                  vec, vec,
                  pl.BlockSpec(w.shape, lambda b, i: (0, 0))],
        out_specs=[pl.BlockSpec((1, tm, wd), lambda b, i: (b, i, 0)) for wd in widths],
        out_shape=[jax.ShapeDtypeStruct((bsz, s, wd), dt) for wd, dt in zip(widths, dtypes)],
        compiler_params=_params(("parallel", "parallel")),
        name="inproj",
    )(x, nw, scale, shift, w)


def _rope_kernel(pos_ref, inv_ref, cos_ref, sin_ref):
    ang = pos_ref[0].astype(F32) * inv_ref[...]
    cos_ref[0] = jnp.cos(ang)
    sin_ref[0] = jnp.sin(ang)


def _rope(positions, tm):
    bsz, s = positions.shape
    inv = (1.0 / (ROPE_BASE ** (jnp.arange(ROPE_HALF, dtype=F32) / ROPE_HALF))).reshape(1, ROPE_HALF)
    spec = pl.BlockSpec((1, tm, ROPE_HALF), lambda b, i: (b, i, 0))
    return pl.pallas_call(
        _rope_kernel,
        grid=(bsz, s // tm),
        in_specs=[pl.BlockSpec((1, tm, 1), lambda b, i: (b, i, 0)),
                  pl.BlockSpec((1, ROPE_HALF), lambda b, i: (0, 0))],
        out_specs=[spec, spec],
        out_shape=[jax.ShapeDtypeStruct((bsz, s, ROPE_HALF), F32)] * 2,
        compiler_params=_params(("parallel", "parallel")),
        name="rope",
    )(positions.reshape(bsz, s, 1), inv)


def _ssd_kernel(xbc_ref, z_ref, dt_ref, cw_ref, cb_ref, dtb_ref, a_ref, dx_ref, nw_ref, r_ref, tril_ref,
                o_ref, tail_ref, state_ref, y_ref):
    L = CHUNK

    @pl.when(pl.program_id(1) == 0)
    def _():
        tail_ref[...] = jnp.zeros_like(tail_ref)
        state_ref[...] = jnp.zeros_like(state_ref)

    u = xbc_ref[0].astype(F32)
    tail = tail_ref[...]
    row8 = lax.broadcasted_iota(jnp.int32, (SUBLANES, 1), 0)
    acc = u * cw_ref[SSM_CONV - 1:SSM_CONV, :] + cb_ref[...]
    for d in range(1, SSM_CONV):
        sh = pltpu.roll(u, d, axis=0)
        head = jnp.where(row8 < d, pltpu.roll(tail, d, axis=0), sh[0:SUBLANES])
        sh = jnp.concatenate([head, sh[SUBLANES:]], axis=0)
        acc = acc + sh * cw_ref[SSM_CONV - 1 - d:SSM_CONV - d, :]
    tail_ref[...] = u[L - SUBLANES:L]
    act = _silu(acc)
    xs = act[:, :SSM_D_INNER]
    xs_b = xs.astype(BF16)
    bm = act[:, SSM_D_INNER:SSM_D_INNER + SSM_GROUPS * SSM_STATE]
    cm_b = act[:, SSM_D_INNER + SSM_GROUPS * SSM_STATE:].astype(BF16)

    dt_raw = dt_ref[0] + dtb_ref[...]
    dtv = jnp.maximum(dt_raw, 0.0) + jnp.log1p(jnp.exp(-jnp.abs(dt_raw)))
    a = dtv * a_ref[...]
    tril = tril_ref[...]
    a_hi, a_mid, a_lo = _split3(a)
    acum = _dot(tril, a_hi) + _dot(tril, a_mid) + _dot(tril, a_lo)
    acum_t = acum.T
    dt_t = dtv.T
    last = acum[L - 1:L, :]
    r = r_ref[...]
    e1x = _dot(jnp.exp(acum).astype(BF16), r)
    e2x = _dot((jnp.exp(last - acum) * dtv).astype(BF16), r)
    l_hi, l_mid, l_lo = _split3(jnp.broadcast_to(jnp.exp(last), (SUBLANES, LANES)))
    decay_x = (_dot(l_hi, r) + _dot(l_mid, r) + _dot(l_lo, r))[0:1, :]

    causal = (lax.broadcasted_iota(jnp.int32, (L, L), 0) >= lax.broadcasted_iota(jnp.int32, (L, L), 1))
    lane_lo = lax.broadcasted_iota(jnp.int32, (L, LANES), 1) < SSM_HEADDIM
    heads_per_group = SSM_HEADS // SSM_GROUPS
    for g in range(SSM_GROUPS):
        gs = slice(g * GROUP_WIDTH, (g + 1) * GROUP_WIDTH)
        bg = bm[:, g * SSM_STATE:(g + 1) * SSM_STATE]
        cg_b = cm_b[:, g * SSM_STATE:(g + 1) * SSM_STATE]
        cb = _dot_nt(cg_b, bg.astype(BF16))
        for pair in range(heads_per_group // 2):
            p = g * (heads_per_group // 2) + pair
            xp = xs_b[:, p * LANES:(p + 1) * LANES]
            y_pair = None
            for sub in range(2):
                h = 2 * p + sub
                seg = jnp.broadcast_to(acum[:, h:h + 1], (L, L)) - acum_t[h:h + 1, :]
                w = cb * jnp.exp(jnp.where(causal, seg, -jnp.inf)) * dt_t[h:h + 1, :]
                xh = jnp.where(lane_lo if sub == 0 else jnp.logical_not(lane_lo), xp, jnp.zeros_like(xp))
                yh = _dot(w.astype(BF16), xh)
                y_pair = yh if y_pair is None else y_pair + yh
            y_ref[:, p * LANES:(p + 1) * LANES] = y_pair
        st = state_ref[g]
        y_inter = _dot(cg_b, st.astype(BF16)) * e1x[:, gs]
        y_ref[:, gs] = y_ref[:, gs] + y_inter
        upd = _dot(bg.T.astype(BF16), (xs[:, gs] * e2x[:, gs]).astype(BF16))
        state_ref[g] = st * decay_x[:, gs] + upd

    y = (y_ref[...] + xs * dx_ref[...]) * _silu(z_ref[0].astype(F32))
    for g in range(SSM_GROUPS):
        gs = slice(g * GROUP_WIDTH, (g + 1) * GROUP_WIDTH)
        yg = y[:, gs]
        ms = jnp.mean(yg * yg, axis=-1, keepdims=True)
        o_ref[0, :, gs] = (yg * lax.rsqrt(ms + EPS) * nw_ref[:, gs]).astype(o_ref.dtype)


def _ssd(xbc, z, dt, conv_w, conv_b, dt_bias, a_log, d_skip, ssm_norm_w):
    bsz, s, _ = xbc.shape
    pad = DT_PAD - SSM_HEADS
    dtb = jnp.pad(dt_bias.astype(F32), (0, pad)).reshape(1, DT_PAD)
    a_neg = jnp.pad(-jnp.exp(a_log.astype(F32)), (0, pad)).reshape(1, DT_PAD)
    dx = jnp.repeat(d_skip.astype(F32), SSM_HEADDIM).reshape(1, SSM_D_INNER)
    expand = (jnp.arange(DT_PAD)[:, None] == (jnp.arange(SSM_D_INNER)[None, :] // SSM_HEADDIM)).astype(BF16)
    tril = (jnp.arange(CHUNK)[:, None] >= jnp.arange(CHUNK)[None, :]).astype(BF16)

    def seq(width):
        return pl.BlockSpec((1, CHUNK, width), lambda b, c: (b, c, 0))

    def const(shape):
        return pl.BlockSpec(shape, lambda b, c: (0,) * len(shape))

    return pl.pallas_call(
        _ssd_kernel,
        grid=(bsz, s // CHUNK),
        in_specs=[seq(SSM_CONV_DIM), seq(SSM_D_INNER), seq(DT_PAD),
                  const((SSM_CONV, SSM_CONV_DIM)), const((1, SSM_CONV_DIM)), const((1, DT_PAD)), const((1, DT_PAD)),
                  const((1, SSM_D_INNER)), const((1, SSM_D_INNER)), const((DT_PAD, SSM_D_INNER)),
                  const((CHUNK, CHUNK))],
        out_specs=seq(SSM_D_INNER),
        out_shape=jax.ShapeDtypeStruct((bsz, s, SSM_D_INNER), BF16),
        scratch_shapes=[pltpu.VMEM((SUBLANES, SSM_CONV_DIM), F32),
                        pltpu.VMEM((SSM_GROUPS, SSM_STATE, GROUP_WIDTH), F32),
                        pltpu.VMEM((CHUNK, SSM_D_INNER), F32)],
        compiler_params=_params(("parallel", "arbitrary")),
        name="ssd",
    )(xbc, z, dt, conv_w.astype(F32), conv_b.astype(F32).reshape(1, SSM_CONV_DIM), dtb, a_neg, dx,
      ssm_norm_w.astype(F32).reshape(1, SSM_D_INNER), expand, tril)


def _ret_consts():
    lg = np.log1p(-(2.0 ** (-5.0 - np.arange(RET_HEADS, dtype=np.float64))))
    idx = np.arange(CHUNK, dtype=np.float64)
    rel = idx[:, None] - idx[None, :]
    intra = np.where(rel >= 0, np.exp(np.maximum(rel, 0.0)[None] * lg[:, None, None]), 0.0)
    qd = np.exp((idx + 1.0)[None, :] * lg[:, None])
    kd = np.exp((CHUNK - 1.0 - idx)[None, :] * lg[:, None]) * (RET_QK_DIM ** -0.5)
    qd = np.broadcast_to(qd[:, :, None], (RET_HEADS, CHUNK, RET_QK_DIM))
    kd = np.broadcast_to(kd[:, :, None], (RET_HEADS, CHUNK, RET_QK_DIM))
    chunk_decay = tuple(float(v) for v in np.exp(CHUNK * lg))
    return (jnp.asarray(intra, F32), jnp.asarray(qd, F32), jnp.asarray(kd, F32), chunk_decay)


def _ret_kernel(q_ref, k_ref, v_ref, g_ref, cos_ref, sin_ref, intra_ref, qd_ref, kd_ref, o_ref, state_ref, *,
                chunk_decay):
    @pl.when(pl.program_id(1) == 0)
    def _():
        state_ref[...] = jnp.zeros_like(state_ref)

    cos = cos_ref[0]
    sin = sin_ref[0]

    def rot(t):
        t1, t2 = t[:, :ROPE_HALF], t[:, ROPE_HALF:]
        return jnp.concatenate([t1 * cos - t2 * sin, t1 * sin + t2 * cos], axis=1)

    for h in range(RET_HEADS):
        qs = slice(h * RET_QK_DIM, (h + 1) * RET_QK_DIM)
        vs = slice(h * RET_V_DIM, (h + 1) * RET_V_DIM)
        qr = rot(q_ref[0, :, qs].astype(F32))
        kr = rot(k_ref[0, :, qs].astype(F32))
        vh = v_ref[0, :, vs]
        s = _dot_nt(qr.astype(BF16), (kr * (RET_QK_DIM ** -0.5)).astype(BF16)) * intra_ref[h]
        st = state_ref[h]
        y = _dot(s.astype(BF16), vh) + _dot((qr * qd_ref[h]).astype(BF16), st.astype(BF16))
        state_ref[h] = st * chunk_decay[h] + _dot((kr * kd_ref[h]).T.astype(BF16), vh)
        mu = jnp.mean(y, axis=-1, keepdims=True)
        yc = y - mu
        var = jnp.mean(yc * yc, axis=-1, keepdims=True)
        o_ref[0, :, vs] = (_silu(g_ref[0, :, vs].astype(F32)) * (yc * lax.rsqrt(var + EPS))).astype(o_ref.dtype)


def _ret(q, k, v, g, cos, sin):
    bsz, s, _ = q.shape
    intra, qd, kd, chunk_decay = _ret_consts()

    def seq(width):
        return pl.BlockSpec((1, CHUNK, width), lambda b, c: (b, c, 0))

    def const(shape):
        return pl.BlockSpec(shape, lambda b, c: (0,) * len(shape))

    return pl.pallas_call(
        functools.partial(_ret_kernel, chunk_decay=chunk_decay),
        grid=(bsz, s // CHUNK),
        in_specs=[seq(RET_QK_WIDTH), seq(RET_QK_WIDTH), seq(RET_V_WIDTH), seq(RET_V_WIDTH),
                  seq(ROPE_HALF), seq(ROPE_HALF),
                  const(intra.shape), const(qd.shape), const(kd.shape)],
        out_specs=seq(RET_V_WIDTH),
        out_shape=jax.ShapeDtypeStruct((bsz, s, RET_V_WIDTH), BF16),
        scratch_shapes=[pltpu.VMEM((RET_HEADS, RET_QK_DIM, RET_V_DIM), F32)],
        compiler_params=_params(("parallel", "arbitrary")),
        name="ret",
    )(q, k, v, g, cos, sin, intra, qd, kd)


def _merge_kernel(ys_ref, yr_ref, gs_ref, gr_ref, x_ref, g1_ref, sc2_ref, sh2_ref, n2w_ref,
                  wso_ref, wro_ref, wo_ref, wrh_ref, wrl_ref, x1_ref, h2_ref, sct_ref):
    y_ssm = _dot(ys_ref[0], wso_ref[...])
    y_ret = _dot(yr_ref[0], wro_ref[...])
    merged = _sigmoid(gs_ref[0].astype(F32)) * y_ssm + _sigmoid(gr_ref[0].astype(F32)) * y_ret
    x1 = x_ref[0] + g1_ref[0] * _dot(merged.astype(BF16), wo_ref[...])
    x1_ref[0] = x1
    h2 = _rms_mod(x1, n2w_ref[...], sc2_ref[0], sh2_ref[0])
    h_hi = h2.astype(BF16)
    h2_ref[0] = _pack_rows(h2)
    h_lo = (h2 - h_hi.astype(F32)).astype(BF16)
    wrh = wrh_ref[...]
    logits_t = _dot_nt(wrh, h_hi) + _dot_nt(wrh, h_lo) + _dot_nt(wrl_ref[...], h_hi)
    sct_ref[...] = _sigmoid(logits_t)


def _merge(ys, yr, gs, gr, x, gate1, scale2, shift2, norm2_w, w_ssm_out, w_ret_out, w_out, w_router, tm):
    bsz, s, d = x.shape
    nt = s // tm
    wrt = w_router.astype(F32).T
    wrh = wrt.astype(BF16)
    wrl = (wrt - wrh.astype(F32)).astype(BF16)

    def seq(width):
        return pl.BlockSpec((1, tm, width), lambda b, i: (b, i, 0))

    def const(shape):
        return pl.BlockSpec(shape, lambda b, i: (0,) * len(shape))

    vec = pl.BlockSpec((1, 1, d), lambda b, i: (b, 0, 0))
    return pl.pallas_call(
        _merge_kernel,
        grid=(bsz, nt),
        in_specs=[seq(SSM_D_INNER), seq(RET_V_WIDTH), seq(d), seq(d), seq(d), vec, vec, vec, const((1, d)),
                  const((SSM_D_INNER, d)), const((RET_V_WIDTH, d)), const((d, d)),
                  const((N_EXPERTS, d)), const((N_EXPERTS, d))],
        out_specs=[seq(d), seq(PACKED), pl.BlockSpec((N_EXPERTS, tm), lambda b, i: (0, b * nt + i))],
        out_shape=[jax.ShapeDtypeStruct((bsz, s, d), F32), jax.ShapeDtypeStruct((bsz, s, PACKED), jnp.int32),
                   jax.ShapeDtypeStruct((N_EXPERTS, bsz * s), F32)],
        compiler_params=_params(("parallel", "parallel")),
        name="merge",
    )(ys, yr, gs, gr, x, gate1, scale2, shift2, norm2_w.astype(F32).reshape(1, d),
      w_ssm_out.astype(BF16), w_ret_out.astype(BF16), w_out.astype(BF16), wrh, wrl)


def _route_kernel(sc_ref, bias_ref, upper_ref, ones_ref, idx_ref, wts_ref, rank_ref, cnt_ref, carry_ref):
    @pl.when(pl.program_id(0) == 0)
    def _():
        carry_ref[...] = jnp.zeros_like(carry_ref)

    tl = sc_ref.shape[1]
    neg = -jnp.inf
    scores = sc_ref[...]
    sel = scores + bias_ref[...]

    def first_argmax(vals, iota, n):
        m = jnp.max(vals, axis=0, keepdims=True)
        return jnp.min(jnp.where(vals == m, iota, float(n)), axis=0, keepdims=True), m

    io_g = lax.broadcasted_iota(jnp.int32, (ROUTE_GROUP_SIZE, tl), 0).astype(F32)
    rows = []
    for g in range(N_ROUTE_GROUPS):
        blk = sel[g * ROUTE_GROUP_SIZE:(g + 1) * ROUTE_GROUP_SIZE]
        i1, m1 = first_argmax(blk, io_g, ROUTE_GROUP_SIZE)
        m2 = jnp.max(jnp.where(io_g == i1, neg, blk), axis=0, keepdims=True)
        rows.append(m1 + m2)
    gsc = jnp.concatenate(rows, axis=0)
    io_8 = lax.broadcasted_iota(jnp.int32, (N_ROUTE_GROUPS, tl), 0).astype(F32)
    chosen = jnp.zeros((N_ROUTE_GROUPS, tl), F32)
    for _ in range(TOPK_ROUTE_GROUPS):
        i, _m = first_argmax(gsc, io_8, N_ROUTE_GROUPS)
        hit = io_8 == i
        chosen = jnp.where(hit, 1.0, chosen)
        gsc = jnp.where(hit, neg, gsc)
    msel = jnp.concatenate(
        [jnp.where(chosen[g:g + 1] > 0.5, sel[g * ROUTE_GROUP_SIZE:(g + 1) * ROUTE_GROUP_SIZE], neg)
         for g in range(N_ROUTE_GROUPS)], axis=0)

    io_e = lax.broadcasted_iota(jnp.int32, (N_EXPERTS, tl), 0).astype(F32)
    multi = jnp.zeros((N_EXPERTS, tl), F32)
    idx_rows, sc_rows = [], []
    for _ in range(TOP_K):
        i, _m = first_argmax(msel, io_e, N_EXPERTS)
        hit = io_e == i
        sc_rows.append(jnp.sum(jnp.where(hit, scores, 0.0), axis=0, keepdims=True))
        msel = jnp.where(hit, neg, msel)
        multi = jnp.where(hit, 1.0, multi)
        idx_rows.append(i)
    den = sc_rows[0]
    for r in sc_rows[1:]:
        den = den + r
    den = den + 1e-20
    wts_ref[...] = jnp.concatenate([r / den * ROUTED_SCALE for r in sc_rows], axis=0)
    idx_ref[...] = jnp.concatenate(idx_rows, axis=0).astype(jnp.int32)

    multi_b = multi.astype(BF16)
    carry = carry_ref[...]
    rank_full = _dot(multi_b, upper_ref[...]) + carry
    rank_ref[...] = jnp.concatenate(
        [jnp.sum(jnp.where(io_e == i, rank_full, 0.0), axis=0, keepdims=True) for i in idx_rows],
        axis=0).astype(jnp.int32)
    carry = carry + _dot(multi_b, ones_ref[...])
    carry_ref[...] = carry
    cnt_ref[...] = carry[:, :LANES]


def _route(scores_t, router_bias):
    e, t = scores_t.shape
    tl = min(ROUTE_TILE, t)
    bias = jnp.broadcast_to(router_bias.astype(F32).reshape(e, 1), (e, tl))
    upper = (jnp.arange(tl)[:, None] < jnp.arange(tl)[None, :]).astype(BF16)
    ones = jnp.ones((tl, tl), BF16)
    tok = pl.BlockSpec((TOP_K, tl), lambda i: (0, i))

    def const(shape):
        return pl.BlockSpec(shape, lambda i: (0,) * len(shape))

    return pl.pallas_call(
        _route_kernel,
        grid=(t // tl,),
        in_specs=[pl.BlockSpec((e, tl), lambda i: (0, i)), const((e, tl)), const((tl, tl)), const((tl, tl))],
        out_specs=[tok, tok, tok, const((e, LANES))],
        out_shape=[jax.ShapeDtypeStruct((TOP_K, t), jnp.int32), jax.ShapeDtypeStruct((TOP_K, t), F32),
                   jax.ShapeDtypeStruct((TOP_K, t), jnp.int32), jax.ShapeDtypeStruct((e, LANES), F32)],
        scratch_shapes=[pltpu.VMEM((e, tl), F32)],
        compiler_params=_params(("arbitrary",)),
        name="route",
    )(scores_t, bias, upper, ones)


def _expert_kernel(be_ref, x_ref, wg_ref, wu_ref, wd_ref, o_ref):
    del be_ref
    x = _unpack_rows(x_ref[...])
    g = _dot(x, wg_ref[0].astype(BF16))
    u = _dot(x, wu_ref[0].astype(BF16))
    o_ref[...] = _pack_rows(_dot((_silu(g) * u).astype(BF16), wd_ref[0].astype(BF16)))


def _experts(block_expert, xs, w_gate, w_up, w_down):
    n_slots = xs.shape[0]
    n_blocks = n_slots // EXPERT_BLOCK
    d, f = w_gate.shape[-2:]
    grid_spec = pltpu.PrefetchScalarGridSpec(
        num_scalar_prefetch=1,
        grid=(n_blocks,),
        in_specs=[pl.BlockSpec((EXPERT_BLOCK, PACKED), lambda b, be: (b, 0)),
                  pl.BlockSpec((1, d, f), lambda b, be: (be[b], 0, 0)),
                  pl.BlockSpec((1, d, f), lambda b, be: (be[b], 0, 0)),
                  pl.BlockSpec((1, f, d), lambda b, be: (be[b], 0, 0))],
        out_specs=pl.BlockSpec((EXPERT_BLOCK, PACKED), lambda b, be: (b, 0)),
    )
    return pl.pallas_call(
        _expert_kernel,
        grid_spec=grid_spec,
        out_shape=jax.ShapeDtypeStruct((n_slots, PACKED), jnp.int32),
        compiler_params=_params(("arbitrary",)),
        name="experts",
    )(block_expert, xs, w_gate, w_up, w_down)


def _final_kernel(x1_ref, h2_ref, yk_ref, wts_ref, g2_ref, wsg_ref, wsu_ref, wsd_ref, fnw_ref, o_ref):
    h = _unpack_rows(h2_ref[0])
    act = (_silu(_dot(h, wsg_ref[...])) * _dot(h, wsu_ref[...])).astype(BF16)
    moe = _dot(act, wsd_ref[...])
    wts = wts_ref[...]
    for kk in range(TOP_K):
        moe = moe + _unpack_rows(yk_ref[kk]).astype(F32) * wts[:, kk:kk + 1]
    x2 = x1_ref[0] + g2_ref[0] * moe
    ms = jnp.mean(x2 * x2, axis=-1, keepdims=True)
    o_ref[0] = x2 * lax.rsqrt(ms + EPS) * fnw_ref[...]


def _final(x1, h2p, yk, wts, gate2, w_sh_gate, w_sh_up, w_sh_down, final_norm_w, tm):
    bsz, s, d = x1.shape
    nt = s // tm
    f = w_sh_gate.shape[-1]
    seq = pl.BlockSpec((1, tm, d), lambda b, i: (b, i, 0))

    def const(shape):
        return pl.BlockSpec(shape, lambda b, i: (0,) * len(shape))

    return pl.pallas_call(
        _final_kernel,
        grid=(bsz, nt),
        in_specs=[seq, pl.BlockSpec((1, tm, PACKED), lambda b, i: (b, i, 0)),
                  pl.BlockSpec((TOP_K, tm, PACKED), lambda b, i: (0, b * nt + i, 0)),
                  pl.BlockSpec((tm, TOP_K), lambda b, i: (b * nt + i, 0)),
                  pl.BlockSpec((1, 1, d), lambda b, i: (b, 0, 0)),
                  const((d, f)), const((d, f)), const((f, d)), const((1, d))],
        out_specs=seq,
        out_shape=jax.ShapeDtypeStruct((bsz, s, d), F32),
        compiler_params=_params(("parallel", "parallel")),
        name="final",
    )(x1, h2p, yk, wts, gate2, w_sh_gate.astype(BF16), w_sh_up.astype(BF16), w_sh_down.astype(BF16),
      final_norm_w.astype(F32).reshape(1, d))


def _gather_rows(table, idx):
    n = idx.shape[0]
    width = table.shape[1]
    workers = SC_CORES * SC_SUBCORES
    per_worker = n // workers
    steps = per_worker // SC_WINDOW
    assert per_worker * workers == n and steps * SC_WINDOW == per_worker
    mesh = plsc.VectorSubcoreMesh(core_axis_name="c", subcore_axis_name="s")

    @functools.partial(
        pl.kernel, mesh=mesh,
        out_type=jax.ShapeDtypeStruct((n, width), table.dtype),
        scratch_types=[pltpu.VMEM((SC_WINDOW,), jnp.int32),
                       pltpu.VMEM((SC_WINDOW, width), table.dtype),
                       pltpu.SemaphoreType.DMA],
        name="gather_rows",
    )
    def run(table_hbm, idx_hbm, out_hbm, idx_v, rows_v, sem):
        base = (lax.axis_index("s") * SC_CORES + lax.axis_index("c")) * per_worker

        @pl.loop(0, steps)
        def _(i):
            off = pl.multiple_of(base + i * SC_WINDOW, SC_WINDOW)
            pltpu.sync_copy(idx_hbm.at[pl.ds(off, SC_WINDOW)], idx_v)
            pltpu.async_copy(table_hbm.at[idx_v], rows_v, sem).wait()
            pltpu.sync_copy(rows_v, out_hbm.at[pl.ds(off, SC_WINDOW)])

    return run(table, idx)


def _layer(x, c, positions, w_ada, b_ada, norm1_w, w_in, conv_w, conv_b, dt_bias, a_log, d_skip, ssm_norm_w,
           w_ssm_out, w_ret_out, w_out, norm2_w, w_router, router_bias, w_exp_gate, w_exp_up, w_exp_down,
           w_sh_gate, w_sh_up, w_sh_down):
    bsz, s, d = x.shape
    t = bsz * s
    tm = min(256, s)

    mod = _mod(c, w_ada, b_ada)
    shift1, scale1, gate1, shift2, scale2, gate2 = (mod[:, i * d:(i + 1) * d].reshape(bsz, 1, d) for i in range(6))

    bounds = np.cumsum((0, SSM_D_INNER, SSM_CONV_DIM, SSM_HEADS, RET_QK_WIDTH, RET_QK_WIDTH, RET_V_WIDTH,
                        RET_V_WIDTH, d, d))
    wz, wxbc, wdt, wq, wk, wv, wg, wgs, wgr = (w_in[:, bounds[i]:bounds[i + 1]] for i in range(9))
    nw1 = norm1_w.astype(F32).reshape(1, d)
    w_a = jnp.concatenate([wxbc, wz, jnp.pad(wdt, ((0, 0), (0, DT_PAD - SSM_HEADS)))], axis=1).astype(BF16)
    w_b = jnp.concatenate([wq, wk, wv, wg, wgs, wgr], axis=1).astype(BF16)
    xbc, z, dt = _inproj(x, nw1, scale1, shift1, w_a, (SSM_CONV_DIM, SSM_D_INNER, DT_PAD), (BF16, BF16, F32), tm)
    q, k, v, g, gs, gr = _inproj(x, nw1, scale1, shift1, w_b,
                                 (RET_QK_WIDTH, RET_QK_WIDTH, RET_V_WIDTH, RET_V_WIDTH, d, d), (BF16,) * 6, tm)

    ys = _ssd(xbc, z, dt, conv_w, conv_b, dt_bias, a_log, d_skip, ssm_norm_w)
    cos, sin = _rope(positions, tm)
    yr = _ret(q, k, v, g, cos, sin)
    x1, h2, scores_t = _merge(ys, yr, gs, gr, x, gate1, scale2, shift2, norm2_w, w_ssm_out, w_ret_out, w_out,
                              w_router, tm)

    idx_t, wts_t, rank_t, cnt = _route(scores_t, router_bias)

    counts = cnt[:, 0].astype(jnp.int32)
    padded = (counts + EXPERT_BLOCK - 1) // EXPERT_BLOCK * EXPERT_BLOCK
    padded_end = jnp.cumsum(padded)
    padded_start = padded_end - padded
    n_blocks = -(-(t * TOP_K) // EXPERT_BLOCK) + N_EXPERTS
    n_slots = n_blocks * EXPERT_BLOCK
    dest_t = padded_start[idx_t] + rank_t
    block_expert = jnp.minimum(
        jnp.searchsorted(padded_end, jnp.arange(n_blocks, dtype=jnp.int32) * EXPERT_BLOCK, side='right'),
        N_EXPERTS - 1).astype(jnp.int32)
    tok = jnp.broadcast_to(jnp.arange(t, dtype=jnp.int32)[None, :], (TOP_K, t))
    slot_tok = jnp.full((n_slots,), t, jnp.int32).at[dest_t.reshape(-1)].set(tok.reshape(-1))
    table = jnp.concatenate([h2.reshape(t, PACKED), jnp.zeros((SUBLANES, PACKED), h2.dtype)], axis=0)
    xs = _gather_rows(table, slot_tok)
    y_sorted = _experts(block_expert, xs, w_exp_gate, w_exp_up, w_exp_down)
    yk = _gather_rows(y_sorted, dest_t.reshape(-1)).reshape(TOP_K, t, PACKED)
    return x1, h2, yk, wts_t.T, gate2


def kernel(x, c, positions, w_ada, b_ada, norm1_w, w_in, conv_w, conv_b, dt_bias, a_log, d_skip, ssm_norm_w,
           w_ssm_out, w_ret_out, w_out, norm2_w, w_router, router_bias, w_exp_gate, w_exp_up, w_exp_down,
           w_sh_gate, w_sh_up, w_sh_down, final_norm_w):
    assert w_ada.shape[0] == 1, "the final rmsnorm is fused into the single layer's last kernel"
    tm = min(256, x.shape[1])
    x1, h2, yk, wts, gate2 = _layer(
        x, c, positions, w_ada[0], b_ada[0], norm1_w[0], w_in[0], conv_w[0], conv_b[0], dt_bias[0], a_log[0],
        d_skip[0], ssm_norm_w[0], w_ssm_out[0], w_ret_out[0], w_out[0], norm2_w[0], w_router[0],
        router_bias[0], w_exp_gate[0], w_exp_up[0], w_exp_down[0], w_sh_gate[0], w_sh_up[0], w_sh_down[0])
    return _final(x1, h2, yk, wts, gate2, w_sh_gate[0], w_sh_up[0], w_sh_down[0], final_norm_w, tm)
```

```python
import functools

import numpy as np
import jax
import jax.numpy as jnp
from jax import lax
from jax.experimental import pallas as pl
from jax.experimental.pallas import tpu as pltpu
from jax.experimental.pallas import tpu_sc as plsc

F32 = jnp.float32
BF16 = jnp.bfloat16

EPS = 1e-6
D_MODEL = 1024
SSM_D_INNER = 2048
SSM_HEADDIM = 64
SSM_HEADS = 32
SSM_GROUPS = 4
SSM_STATE = 128
SSM_CONV = 4
SSM_CONV_DIM = SSM_D_INNER + 2 * SSM_GROUPS * SSM_STATE
GROUP_WIDTH = SSM_D_INNER // SSM_GROUPS
RET_HEADS = 4
RET_QK_DIM = 256
RET_V_DIM = 512
RET_QK_WIDTH = RET_HEADS * RET_QK_DIM
RET_V_WIDTH = RET_HEADS * RET_V_DIM
ROPE_BASE = 10000.0
ROPE_HALF = RET_QK_DIM // 2
N_EXPERTS = 256
TOP_K = 8
N_ROUTE_GROUPS = 8
TOPK_ROUTE_GROUPS = 4
ROUTE_GROUP_SIZE = N_EXPERTS // N_ROUTE_GROUPS
EXPERT_DIM = 256
ROUTED_SCALE = 2.5

CHUNK = 128
LANES = 128
SUBLANES = 8
DT_PAD = LANES
EXPERT_BLOCK = 128
ROUTE_TILE = 256
VMEM_LIMIT = 56 * 1024 * 1024
PACKED = D_MODEL // 2
SC_CORES = 2
SC_SUBCORES = 16
SC_WINDOW = 64


def _silu(x):
    return x / (1.0 + jnp.exp(-x))


def _sigmoid(x):
    return 1.0 / (1.0 + jnp.exp(-x))


def _split3(a):
    hi = a.astype(BF16)
    r1 = a - hi.astype(F32)
    mid = r1.astype(BF16)
    lo = (r1 - mid.astype(F32)).astype(BF16)
    return hi, mid, lo


def _pack_rows(x):
    w = x.shape[1] // 2
    hi = lax.bitcast_convert_type(x[:, :w].astype(BF16).astype(F32), jnp.uint32)
    lo = lax.bitcast_convert_type(x[:, w:].astype(BF16).astype(F32), jnp.uint32)
    return lax.bitcast_convert_type(hi | (lo >> 16), jnp.int32)


def _unpack_rows(p):
    u = lax.bitcast_convert_type(p, jnp.uint32)
    hi = lax.bitcast_convert_type(u & jnp.uint32(0xFFFF0000), F32)
    lo = lax.bitcast_convert_type(u << 16, F32)
    return jnp.concatenate([hi, lo], axis=1).astype(BF16)


def _dot(a, b):
    return jnp.dot(a, b, preferred_element_type=F32)


def _dot_nt(a, b):
    return lax.dot_general(a, b, (((1,), (1,)), ((), ())), preferred_element_type=F32)


def _params(sem):
    return pltpu.CompilerParams(dimension_semantics=sem, vmem_limit_bytes=VMEM_LIMIT)


def _mod_kernel(c_ref, w_ref, b_ref, o_ref):
    o_ref[...] = _dot(_silu(c_ref[...]), w_ref[...]) + b_ref[...]


def _mod(c, w_ada, b_ada):
    bsz, d = c.shape
    n = w_ada.shape[1]
    return pl.pallas_call(
        _mod_kernel,
        grid=(n // d,),
        in_specs=[pl.BlockSpec((bsz, d), lambda j: (0, 0)),
                  pl.BlockSpec((d, d), lambda j: (0, j)),
                  pl.BlockSpec((1, d), lambda j: (0, j))],
        out_specs=pl.BlockSpec((bsz, d), lambda j: (0, j)),
        out_shape=jax.ShapeDtypeStruct((bsz, n), F32),
        compiler_params=_params(("arbitrary",)),
        name="mod",
    )(c, w_ada, b_ada.reshape(1, n))


def _rms_mod(x, nw, scale, shift):
    ms = jnp.mean(x * x, axis=-1, keepdims=True)
    return (x * lax.rsqrt(ms + EPS) * nw) * (1.0 + scale) + shift


def _inproj_kernel(x_ref, nw_ref, sc_ref, sh_ref, w_ref, *o_refs, widths):
    hb = _rms_mod(x_ref[0], nw_ref[...], sc_ref[0], sh_ref[0]).astype(BF16)
    off = 0
    for o_ref, wd in zip(o_refs, widths):
        for c0 in range(0, wd, 512):
            cw = min(512, wd - c0)
            o_ref[0, :, c0:c0 + cw] = _dot(hb, w_ref[:, off + c0:off + c0 + cw]).astype(o_ref.dtype)
        off += wd


def _inproj(x, nw, scale, shift, w, widths, dtypes, tm):
    bsz, s, d = x.shape
    vec = pl.BlockSpec((1, 1, d), lambda b, i: (b, 0, 0))
    return pl.pallas_call(
        functools.partial(_inproj_kernel, widths=widths),
        grid=(bsz, s // tm),
        in_specs=[pl.BlockSpec((1, tm, d), lambda b, i: (b, i, 0)),
                  pl.BlockSpec((1, d), lambda b, i: (0, 0)),
                  vec, vec,
                  pl.BlockSpec(w.shape, lambda b, i: (0, 0))],
        out_specs=[pl.BlockSpec((1, tm, wd), lambda b, i: (b, i, 0)) for wd in widths],
        out_shape=[jax.ShapeDtypeStruct((bsz, s, wd), dt) for wd, dt in zip(widths, dtypes)],
        compiler_params=_params(("parallel", "parallel")),
        name="inproj",
    )(x, nw, scale, shift, w)


def _rope_kernel(pos_ref, inv_ref, cos_ref, sin_ref):
    ang = pos_ref[0].astype(F32) * inv_ref[...]
    cos_ref[0] = jnp.cos(ang)
    sin_ref[0] = jnp.sin(ang)


def _rope(positions, tm):
    bsz, s = positions.shape
    inv = (1.0 / (ROPE_BASE ** (jnp.arange(ROPE_HALF, dtype=F32) / ROPE_HALF))).reshape(1, ROPE_HALF)
    spec = pl.BlockSpec((1, tm, ROPE_HALF), lambda b, i: (b, i, 0))
    return pl.pallas_call(
        _rope_kernel,
        grid=(bsz, s // tm),
        in_specs=[pl.BlockSpec((1, tm, 1), lambda b, i: (b, i, 0)),
                  pl.BlockSpec((1, ROPE_HALF), lambda b, i: (0, 0))],
        out_specs=[spec, spec],
        out_shape=[jax.ShapeDtypeStruct((bsz, s, ROPE_HALF), F32)] * 2,
        compiler_params=_params(("parallel", "parallel")),
        name="rope",
    )(positions.reshape(bsz, s, 1), inv)


def _ssd_kernel(xbc_ref, z_ref, dt_ref, cw_ref, cb_ref, dtb_ref, a_ref, dx_ref, nw_ref, r_ref, tril_ref,
                o_ref, tail_ref, state_ref, y_ref):
    L = CHUNK

    @pl.when(pl.program_id(1) == 0)
    def _():
        tail_ref[...] = jnp.zeros_like(tail_ref)
        state_ref[...] = jnp.zeros_like(state_ref)

    u = xbc_ref[0].astype(F32)
    tail = tail_ref[...]
    row8 = lax.broadcasted_iota(jnp.int32, (SUBLANES, 1), 0)
    acc = u * cw_ref[SSM_CONV - 1:SSM_CONV, :] + cb_ref[...]
    for d in range(1, SSM_CONV):
        sh = pltpu.roll(u, d, axis=0)
        head = jnp.where(row8 < d, pltpu.roll(tail, d, axis=0), sh[0:SUBLANES])
        sh = jnp.concatenate([head, sh[SUBLANES:]], axis=0)
        acc = acc + sh * cw_ref[SSM_CONV - 1 - d:SSM_CONV - d, :]
    tail_ref[...] = u[L - SUBLANES:L]
    act = _silu(acc)
    xs = act[:, :SSM_D_INNER]
    xs_b = xs.astype(BF16)
    bm = act[:, SSM_D_INNER:SSM_D_INNER + SSM_GROUPS * SSM_STATE]
    cm_b = act[:, SSM_D_INNER + SSM_GROUPS * SSM_STATE:].astype(BF16)

    dt_raw = dt_ref[0] + dtb_ref[...]
    dtv = jnp.maximum(dt_raw, 0.0) + jnp.log1p(jnp.exp(-jnp.abs(dt_raw)))
    a = dtv * a_ref[...]
    tril = tril_ref[...]
    a_hi, a_mid, a_lo = _split3(a)
    acum = _dot(tril, a_hi) + _dot(tril, a_mid) + _dot(tril, a_lo)
    acum_t = acum.T
    dt_t = dtv.T
    last = acum[L - 1:L, :]
    r = r_ref[...]
    e1x = _dot(jnp.exp(acum).astype(BF16), r)
    e2x = _dot((jnp.exp(last - acum) * dtv).astype(BF16), r)
    l_hi, l_mid, l_lo = _split3(jnp.broadcast_to(jnp.exp(last), (SUBLANES, LANES)))
    decay_x = (_dot(l_hi, r) + _dot(l_mid, r) + _dot(l_lo, r))[0:1, :]

    causal = (lax.broadcasted_iota(jnp.int32, (L, L), 0) >= lax.broadcasted_iota(jnp.int32, (L, L), 1))
    lane_lo = lax.broadcasted_iota(jnp.int32, (L, LANES), 1) < SSM_HEADDIM
    heads_per_group = SSM_HEADS // SSM_GROUPS
    for g in range(SSM_GROUPS):
        gs = slice(g * GROUP_WIDTH, (g + 1) * GROUP_WIDTH)
        bg = bm[:, g * SSM_STATE:(g + 1) * SSM_STATE]
        cg_b = cm_b[:, g * SSM_STATE:(g + 1) * SSM_STATE]
        cb = _dot_nt(cg_b, bg.astype(BF16))
        for pair in range(heads_per_group // 2):
            p = g * (heads_per_group // 2) + pair
            xp = xs_b[:, p * LANES:(p + 1) * LANES]
            y_pair = None
            for sub in range(2):
                h = 2 * p + sub
                seg = jnp.broadcast_to(acum[:, h:h + 1], (L, L)) - acum_t[h:h + 1, :]
                w = cb * jnp.exp(jnp.where(causal, seg, -jnp.inf)) * dt_t[h:h + 1, :]
                xh = jnp.where(lane_lo if sub == 0 else jnp.logical_not(lane_lo), xp, jnp.zeros_like(xp))
                yh = _dot(w.astype(BF16), xh)
                y_pair = yh if y_pair is None else y_pair + yh
            y_ref[:, p * LANES:(p + 1) * LANES] = y_pair
        st = state_ref[g]
        y_inter = _dot(cg_b, st.astype(BF16)) * e1x[:, gs]
        y_ref[:, gs] = y_ref[:, gs] + y_inter
        upd = _dot(bg.T.astype(BF16), (xs[:, gs] * e2x[:, gs]).astype(BF16))
        state_ref[g] = st * decay_x[:, gs] + upd

    y = (y_ref[...] + xs * dx_ref[...]) * _silu(z_ref[0].astype(F32))
    for g in range(SSM_GROUPS):
        gs = slice(g * GROUP_WIDTH, (g + 1) * GROUP_WIDTH)
        yg = y[:, gs]
        ms = jnp.mean(yg * yg, axis=-1, keepdims=True)
        o_ref[0, :, gs] = (yg * lax.rsqrt(ms + EPS) * nw_ref[:, gs]).astype(o_ref.dtype)


def _ssd(xbc, z, dt, conv_w, conv_b, dt_bias, a_log, d_skip, ssm_norm_w):
    bsz, s, _ = xbc.shape
    pad = DT_PAD - SSM_HEADS
    dtb = jnp.pad(dt_bias.astype(F32), (0, pad)).reshape(1, DT_PAD)
    a_neg = jnp.pad(-jnp.exp(a_log.astype(F32)), (0, pad)).reshape(1, DT_PAD)
    dx = jnp.repeat(d_skip.astype(F32), SSM_HEADDIM).reshape(1, SSM_D_INNER)
    expand = (jnp.arange(DT_PAD)[:, None] == (jnp.arange(SSM_D_INNER)[None, :] // SSM_HEADDIM)).astype(BF16)
    tril = (jnp.arange(CHUNK)[:, None] >= jnp.arange(CHUNK)[None, :]).astype(BF16)

    def seq(width):
        return pl.BlockSpec((1, CHUNK, width), lambda b, c: (b, c, 0))

    def const(shape):
        return pl.BlockSpec(shape, lambda b, c: (0,) * len(shape))

    return pl.pallas_call(
        _ssd_kernel,
        grid=(bsz, s // CHUNK),
        in_specs=[seq(SSM_CONV_DIM), seq(SSM_D_INNER), seq(DT_PAD),
                  const((SSM_CONV, SSM_CONV_DIM)), const((1, SSM_CONV_DIM)), const((1, DT_PAD)), const((1, DT_PAD)),
                  const((1, SSM_D_INNER)), const((1, SSM_D_INNER)), const((DT_PAD, SSM_D_INNER)),
                  const((CHUNK, CHUNK))],
        out_specs=seq(SSM_D_INNER),
        out_shape=jax.ShapeDtypeStruct((bsz, s, SSM_D_INNER), BF16),
        scratch_shapes=[pltpu.VMEM((SUBLANES, SSM_CONV_DIM), F32),
                        pltpu.VMEM((SSM_GROUPS, SSM_STATE, GROUP_WIDTH), F32),
                        pltpu.VMEM((CHUNK, SSM_D_INNER), F32)],
        compiler_params=_params(("parallel", "arbitrary")),
        name="ssd",
    )(xbc, z, dt, conv_w.astype(F32), conv_b.astype(F32).reshape(1, SSM_CONV_DIM), dtb, a_neg, dx,
      ssm_norm_w.astype(F32).reshape(1, SSM_D_INNER), expand, tril)


def _ret_consts():
    lg = np.log1p(-(2.0 ** (-5.0 - np.arange(RET_HEADS, dtype=np.float64))))
    idx = np.arange(CHUNK, dtype=np.float64)
    rel = idx[:, None] - idx[None, :]
    intra = np.where(rel >= 0, np.exp(np.maximum(rel, 0.0)[None] * lg[:, None, None]), 0.0)
    qd = np.exp((idx + 1.0)[None, :] * lg[:, None])
    kd = np.exp((CHUNK - 1.0 - idx)[None, :] * lg[:, None]) * (RET_QK_DIM ** -0.5)
    qd = np.broadcast_to(qd[:, :, None], (RET_HEADS, CHUNK, RET_QK_DIM))
    kd = np.broadcast_to(kd[:, :, None], (RET_HEADS, CHUNK, RET_QK_DIM))
    chunk_decay = tuple(float(v) for v in np.exp(CHUNK * lg))
    return (jnp.asarray(intra, F32), jnp.asarray(qd, F32), jnp.asarray(kd, F32), chunk_decay)


def _ret_kernel(q_ref, k_ref, v_ref, g_ref, cos_ref, sin_ref, intra_ref, qd_ref, kd_ref, o_ref, state_ref, *,
                chunk_decay):
    @pl.when(pl.program_id(1) == 0)
    def _():
        state_ref[...] = jnp.zeros_like(state_ref)

    cos = cos_ref[0]
    sin = sin_ref[0]

    def rot(t):
        t1, t2 = t[:, :ROPE_HALF], t[:, ROPE_HALF:]
        return jnp.concatenate([t1 * cos - t2 * sin, t1 * sin + t2 * cos], axis=1)

    for h in range(RET_HEADS):
        qs = slice(h * RET_QK_DIM, (h + 1) * RET_QK_DIM)
        vs = slice(h * RET_V_DIM, (h + 1) * RET_V_DIM)
        qr = rot(q_ref[0, :, qs].astype(F32))
        kr = rot(k_ref[0, :, qs].astype(F32))
        vh = v_ref[0, :, vs]
        s = _dot_nt(qr.astype(BF16), (kr * (RET_QK_DIM ** -0.5)).astype(BF16)) * intra_ref[h]
        st = state_ref[h]
        y = _dot(s.astype(BF16), vh) + _dot((qr * qd_ref[h]).astype(BF16), st.astype(BF16))
        state_ref[h] = st * chunk_decay[h] + _dot((kr * kd_ref[h]).T.astype(BF16), vh)
        mu = jnp.mean(y, axis=-1, keepdims=True)
        yc = y - mu
        var = jnp.mean(yc * yc, axis=-1, keepdims=True)
        o_ref[0, :, vs] = (_silu(g_ref[0, :, vs].astype(F32)) * (yc * lax.rsqrt(var + EPS))).astype(o_ref.dtype)


def _ret(q, k, v, g, cos, sin):
    bsz, s, _ = q.shape
    intra, qd, kd, chunk_decay = _ret_consts()

    def seq(width):
        return pl.BlockSpec((1, CHUNK, width), lambda b, c: (b, c, 0))

    def const(shape):
        return pl.BlockSpec(shape, lambda b, c: (0,) * len(shape))

    return pl.pallas_call(
        functools.partial(_ret_kernel, chunk_decay=chunk_decay),
        grid=(bsz, s // CHUNK),
        in_specs=[seq(RET_QK_WIDTH), seq(RET_QK_WIDTH), seq(RET_V_WIDTH), seq(RET_V_WIDTH),
                  seq(ROPE_HALF), seq(ROPE_HALF),
                  const(intra.shape), const(qd.shape), const(kd.shape)],
        out_specs=seq(RET_V_WIDTH),
        out_shape=jax.ShapeDtypeStruct((bsz, s, RET_V_WIDTH), BF16),
        scratch_shapes=[pltpu.VMEM((RET_HEADS, RET_QK_DIM, RET_V_DIM), F32)],
        compiler_params=_params(("parallel", "arbitrary")),
        name="ret",
    )(q, k, v, g, cos, sin, intra, qd, kd)


def _merge_kernel(ys_ref, yr_ref, gs_ref, gr_ref, x_ref, g1_ref, sc2_ref, sh2_ref, n2w_ref,
                  wso_ref, wro_ref, wo_ref, wrh_ref, wrl_ref, x1_ref, h2_ref, sct_ref):
    y_ssm = _dot(ys_ref[0], wso_ref[...])
    y_ret = _dot(yr_ref[0], wro_ref[...])
    merged = _sigmoid(gs_ref[0].astype(F32)) * y_ssm + _sigmoid(gr_ref[0].astype(F32)) * y_ret
    x1 = x_ref[0] + g1_ref[0] * _dot(merged.astype(BF16), wo_ref[...])
    x1_ref[0] = x1
    h2 = _rms_mod(x1, n2w_ref[...], sc2_ref[0], sh2_ref[0])
    h_hi = h2.astype(BF16)
    h2_ref[0] = _pack_rows(h2)
    h_lo = (h2 - h_hi.astype(F32)).astype(BF16)
    wrh = wrh_ref[...]
    logits_t = _dot_nt(wrh, h_hi) + _dot_nt(wrh, h_lo) + _dot_nt(wrl_ref[...], h_hi)
    sct_ref[...] = _sigmoid(logits_t)


def _merge(ys, yr, gs, gr, x, gate1, scale2, shift2, norm2_w, w_ssm_out, w_ret_out, w_out, w_router, tm):
    bsz, s, d = x.shape
    nt = s // tm
    wrt = w_router.astype(F32).T
    wrh = wrt.astype(BF16)
    wrl = (wrt - wrh.astype(F32)).astype(BF16)

    def seq(width):
        return pl.BlockSpec((1, tm, width), lambda b, i: (b, i, 0))

    def const(shape):
        return pl.BlockSpec(shape, lambda b, i: (0,) * len(shape))

    vec = pl.BlockSpec((1, 1, d), lambda b, i: (b, 0, 0))
    return pl.pallas_call(
        _merge_kernel,
        grid=(bsz, nt),
        in_specs=[seq(SSM_D_INNER), seq(RET_V_WIDTH), seq(d), seq(d), seq(d), vec, vec, vec, const((1, d)),
                  const((SSM_D_INNER, d)), const((RET_V_WIDTH, d)), const((d, d)),
                  const((N_EXPERTS, d)), const((N_EXPERTS, d))],
        out_specs=[seq(d), seq(PACKED), pl.BlockSpec((N_EXPERTS, tm), lambda b, i: (0, b * nt + i))],
        out_shape=[jax.ShapeDtypeStruct((bsz, s, d), F32), jax.ShapeDtypeStruct((bsz, s, PACKED), jnp.int32),
                   jax.ShapeDtypeStruct((N_EXPERTS, bsz * s), F32)],
        compiler_params=_params(("parallel", "parallel")),
        name="merge",
    )(ys, yr, gs, gr, x, gate1, scale2, shift2, norm2_w.astype(F32).reshape(1, d),
      w_ssm_out.astype(BF16), w_ret_out.astype(BF16), w_out.astype(BF16), wrh, wrl)


def _route_kernel(sc_ref, bias_ref, upper_ref, ones_ref, idx_ref, wts_ref, rank_ref, cnt_ref, carry_ref):
    @pl.when(pl.program_id(0) == 0)
    def _():
        carry_ref[...] = jnp.zeros_like(carry_ref)

    tl = sc_ref.shape[1]
    neg = -jnp.inf
    scores = sc_ref[...]
    sel = scores + bias_ref[...]

    def first_argmax(vals, iota, n):
        m = jnp.max(vals, axis=0, keepdims=True)
        return jnp.min(jnp.where(vals == m, iota, float(n)), axis=0, keepdims=True), m

    io_g = lax.broadcasted_iota(jnp.int32, (ROUTE_GROUP_SIZE, tl), 0).astype(F32)
    rows = []
    for g in range(N_ROUTE_GROUPS):
        blk = sel[g * ROUTE_GROUP_SIZE:(g + 1) * ROUTE_GROUP_SIZE]
        i1, m1 = first_argmax(blk, io_g, ROUTE_GROUP_SIZE)
        m2 = jnp.max(jnp.where(io_g == i1, neg, blk), axis=0, keepdims=True)
        rows.append(m1 + m2)
    gsc = jnp.concatenate(rows, axis=0)
    io_8 = lax.broadcasted_iota(jnp.int32, (N_ROUTE_GROUPS, tl), 0).astype(F32)
    chosen = jnp.zeros((N_ROUTE_GROUPS, tl), F32)
    for _ in range(TOPK_ROUTE_GROUPS):
        i, _m = first_argmax(gsc, io_8, N_ROUTE_GROUPS)
        hit = io_8 == i
        chosen = jnp.where(hit, 1.0, chosen)
        gsc = jnp.where(hit, neg, gsc)
    msel = jnp.concatenate(
        [jnp.where(chosen[g:g + 1] > 0.5, sel[g * ROUTE_GROUP_SIZE:(g + 1) * ROUTE_GROUP_SIZE], neg)
         for g in range(N_ROUTE_GROUPS)], axis=0)

    io_e = lax.broadcasted_iota(jnp.int32, (N_EXPERTS, tl), 0).astype(F32)
    multi = jnp.zeros((N_EXPERTS, tl), F32)
    idx_rows, sc_rows = [], []
    for _ in range(TOP_K):
        i, _m = first_argmax(msel, io_e, N_EXPERTS)
        hit = io_e == i
        sc_rows.append(jnp.sum(jnp.where(hit, scores, 0.0), axis=0, keepdims=True))
        msel = jnp.where(hit, neg, msel)
        multi = jnp.where(hit, 1.0, multi)
        idx_rows.append(i)
    den = sc_rows[0]
    for r in sc_rows[1:]:
        den = den + r
    den = den + 1e-20
    wts_ref[...] = jnp.concatenate([r / den * ROUTED_SCALE for r in sc_rows], axis=0)
    idx_ref[...] = jnp.concatenate(idx_rows, axis=0).astype(jnp.int32)

    multi_b = multi.astype(BF16)
    carry = carry_ref[...]
    rank_full = _dot(multi_b, upper_ref[...]) + carry
    rank_ref[...] = jnp.concatenate(
        [jnp.sum(jnp.where(io_e == i, rank_full, 0.0), axis=0, keepdims=True) for i in idx_rows],
        axis=0).astype(jnp.int32)
    carry = carry + _dot(multi_b, ones_ref[...])
    carry_ref[...] = carry
    cnt_ref[...] = carry[:, :LANES]


def _route(scores_t, router_bias):
    e, t = scores_t.shape
    tl = min(ROUTE_TILE, t)
    bias = jnp.broadcast_to(router_bias.astype(F32).reshape(e, 1), (e, tl))
    upper = (jnp.arange(tl)[:, None] < jnp.arange(tl)[None, :]).astype(BF16)
    ones = jnp.ones((tl, tl), BF16)
    tok = pl.BlockSpec((TOP_K, tl), lambda i: (0, i))

    def const(shape):
        return pl.BlockSpec(shape, lambda i: (0,) * len(shape))

    return pl.pallas_call(
        _route_kernel,
        grid=(t // tl,),
        in_specs=[pl.BlockSpec((e, tl), lambda i: (0, i)), const((e, tl)), const((tl, tl)), const((tl, tl))],
        out_specs=[tok, tok, tok, const((e, LANES))],
        out_shape=[jax.ShapeDtypeStruct((TOP_K, t), jnp.int32), jax.ShapeDtypeStruct((TOP_K, t), F32),
                   jax.ShapeDtypeStruct((TOP_K, t), jnp.int32), jax.ShapeDtypeStruct((e, LANES), F32)],
        scratch_shapes=[pltpu.VMEM((e, tl), F32)],
        compiler_params=_params(("arbitrary",)),
        name="route",
    )(scores_t, bias, upper, ones)


def _dest_kernel(idx_ref, rank_ref, start_ref, o_ref):
    tl = idx_ref.shape[1]
    io_e = lax.broadcasted_iota(jnp.int32, (N_EXPERTS, tl), 0).astype(F32)
    idx = idx_ref[...].astype(F32)
    start = start_ref[...]
    rows = [jnp.sum(jnp.where(io_e == idx[kk:kk + 1], start, 0.0), axis=0, keepdims=True) for kk in range(TOP_K)]
    o_ref[...] = jnp.concatenate(rows, axis=0).astype(jnp.int32) + rank_ref[...]


def _dest(idx_t, rank_t, padded_start):
    k, t = idx_t.shape
    tl = min(2 * ROUTE_TILE, t)
    start = jnp.broadcast_to(padded_start.astype(F32).reshape(N_EXPERTS, 1), (N_EXPERTS, tl))
    tok = pl.BlockSpec((k, tl), lambda i: (0, i))
    return pl.pallas_call(
        _dest_kernel,
        grid=(t // tl,),
        in_specs=[tok, tok, pl.BlockSpec((N_EXPERTS, tl), lambda i: (0, 0))],
        out_specs=tok,
        out_shape=jax.ShapeDtypeStruct((k, t), jnp.int32),
        compiler_params=_params(("parallel",)),
        name="dest",
    )(idx_t, rank_t, start)


def _expert_kernel(be_ref, nv_ref, x_ref, wg_ref, wu_ref, wd_ref, o_ref, wgb_ref, wub_ref, wdb_ref):
    b = pl.program_id(0)
    nv = nv_ref[b]

    @pl.when(jnp.logical_or(b == 0, be_ref[b] != be_ref[jnp.maximum(b - 1, 0)]))
    def _():
        wgb_ref[...] = wg_ref[0].astype(BF16)
        wub_ref[...] = wu_ref[0].astype(BF16)
        wdb_ref[...] = wd_ref[0].astype(BF16)

    @pl.when(nv > 0)
    def _():
        row = lax.broadcasted_iota(jnp.int32, x_ref.shape, 0)
        x = _unpack_rows(jnp.where(row < nv, x_ref[...], 0))
        g = _dot(x, wgb_ref[...])
        u = _dot(x, wub_ref[...])
        o_ref[...] = _pack_rows(_dot((_silu(g) * u).astype(BF16), wdb_ref[...]))

    @pl.when(nv == 0)
    def _():
        o_ref[...] = jnp.zeros_like(o_ref)


def _experts(block_expert, block_valid, xs, w_gate, w_up, w_down):
    n_slots = xs.shape[0]
    n_blocks = n_slots // EXPERT_BLOCK
    d, f = w_gate.shape[-2:]
    grid_spec = pltpu.PrefetchScalarGridSpec(
        num_scalar_prefetch=2,
        grid=(n_blocks,),
        in_specs=[pl.BlockSpec((EXPERT_BLOCK, PACKED), lambda b, be, nv: (b, 0)),
                  pl.BlockSpec((1, d, f), lambda b, be, nv: (be[b], 0, 0)),
                  pl.BlockSpec((1, d, f), lambda b, be, nv: (be[b], 0, 0)),
                  pl.BlockSpec((1, f, d), lambda b, be, nv: (be[b], 0, 0))],
        out_specs=pl.BlockSpec((EXPERT_BLOCK, PACKED), lambda b, be, nv: (b, 0)),
        scratch_shapes=[pltpu.VMEM((d, f), BF16), pltpu.VMEM((d, f), BF16), pltpu.VMEM((f, d), BF16)],
    )
    return pl.pallas_call(
        _expert_kernel,
        grid_spec=grid_spec,
        out_shape=jax.ShapeDtypeStruct((n_slots, PACKED), jnp.int32),
        compiler_params=_params(("arbitrary",)),
        name="experts",
    )(block_expert, block_valid, xs, w_gate, w_up, w_down)


def _final_kernel(x1_ref, h2_ref, yk_ref, wts_ref, g2_ref, wsg_ref, wsu_ref, wsd_ref, fnw_ref, o_ref):
    h = _unpack_rows(h2_ref[0])
    act = (_silu(_dot(h, wsg_ref[...])) * _dot(h, wsu_ref[...])).astype(BF16)
    moe = _dot(act, wsd_ref[...])
    wts = wts_ref[...]
    for kk in range(TOP_K):
        moe = moe + _unpack_rows(yk_ref[kk]).astype(F32) * wts[:, kk:kk + 1]
    x2 = x1_ref[0] + g2_ref[0] * moe
    ms = jnp.mean(x2 * x2, axis=-1, keepdims=True)
    o_ref[0] = x2 * lax.rsqrt(ms + EPS) * fnw_ref[...]


def _final(x1, h2p, yk, wts, gate2, w_sh_gate, w_sh_up, w_sh_down, final_norm_w, tm):
    bsz, s, d = x1.shape
    nt = s // tm
    f = w_sh_gate.shape[-1]
    seq = pl.BlockSpec((1, tm, d), lambda b, i: (b, i, 0))

    def const(shape):
        return pl.BlockSpec(shape, lambda b, i: (0,) * len(shape))

    return pl.pallas_call(
        _final_kernel,
        grid=(bsz, nt),
        in_specs=[seq, pl.BlockSpec((1, tm, PACKED), lambda b, i: (b, i, 0)),
                  pl.BlockSpec((TOP_K, tm, PACKED), lambda b, i: (0, b * nt + i, 0)),
                  pl.BlockSpec((tm, TOP_K), lambda b, i: (b * nt + i, 0)),
                  pl.BlockSpec((1, 1, d), lambda b, i: (b, 0, 0)),
                  const((d, f)), const((d, f)), const((f, d)), const((1, d))],
        out_specs=seq,
        out_shape=jax.ShapeDtypeStruct((bsz, s, d), F32),
        compiler_params=_params(("parallel", "parallel")),
        name="final",
    )(x1, h2p, yk, wts, gate2, w_sh_gate.astype(BF16), w_sh_up.astype(BF16), w_sh_down.astype(BF16),
      final_norm_w.astype(F32).reshape(1, d))


def _gather_rows(table, idx):
    n = idx.shape[0]
    width = table.shape[1]
    workers = SC_CORES * SC_SUBCORES
    per_worker = n // workers
    steps = per_worker // SC_WINDOW
    assert per_worker * workers == n and steps * SC_WINDOW == per_worker
    mesh = plsc.VectorSubcoreMesh(core_axis_name="c", subcore_axis_name="s")

    @functools.partial(
        pl.kernel, mesh=mesh,
        out_type=jax.ShapeDtypeStruct((n, width), table.dtype),
        scratch_types=[pltpu.VMEM((SC_WINDOW,), jnp.int32),
                       pltpu.VMEM((SC_WINDOW, width), table.dtype),
                       pltpu.SemaphoreType.DMA],
        name="gather_rows",
    )
    def run(table_hbm, idx_hbm, out_hbm, idx_v, rows_v, sem):
        base = (lax.axis_index("s") * SC_CORES + lax.axis_index("c")) * per_worker

        @pl.loop(0, steps)
        def _(i):
            off = pl.multiple_of(base + i * SC_WINDOW, SC_WINDOW)
            pltpu.sync_copy(idx_hbm.at[pl.ds(off, SC_WINDOW)], idx_v)
            pltpu.async_copy(table_hbm.at[idx_v], rows_v, sem).wait()
            pltpu.sync_copy(rows_v, out_hbm.at[pl.ds(off, SC_WINDOW)])

    return run(table, idx)


def _scatter_rows(rows, dest_kt, n_slots):
    t, width = rows.shape
    top_k = dest_kt.shape[0]
    workers = SC_CORES * SC_SUBCORES
    window = LANES
    per_worker = t // workers
    steps = per_worker // window
    assert per_worker * workers == t and steps * window == per_worker
    mesh = plsc.VectorSubcoreMesh(core_axis_name="c", subcore_axis_name="s")

    @functools.partial(
        pl.kernel, mesh=mesh,
        out_type=jax.ShapeDtypeStruct((n_slots, width), rows.dtype),
        scratch_types=[pltpu.VMEM((top_k, window), jnp.int32),
                       pltpu.VMEM((window, width), rows.dtype),
                       pltpu.SemaphoreType.DMA],
        name="scatter_rows",
    )
    def run(rows_hbm, dest_hbm, out_hbm, idx_v, rows_v, sem):
        base = (lax.axis_index("s") * SC_CORES + lax.axis_index("c")) * per_worker

        @pl.loop(0, steps)
        def _(i):
            off = pl.multiple_of(base + i * window, window)
            pltpu.sync_copy(rows_hbm.at[pl.ds(off, window)], rows_v)
            pltpu.sync_copy(dest_hbm.at[:, pl.ds(off, window)], idx_v)
            copies = [pltpu.async_copy(rows_v, out_hbm.at[idx_v.at[kk]], sem) for kk in range(top_k)]
            for cp in copies:
                cp.wait()

    return run(rows, dest_kt)


def _layer(x, c, positions, w_ada, b_ada, norm1_w, w_in, conv_w, conv_b, dt_bias, a_log, d_skip, ssm_norm_w,
           w_ssm_out, w_ret_out, w_out, norm2_w, w_router, router_bias, w_exp_gate, w_exp_up, w_exp_down,
           w_sh_gate, w_sh_up, w_sh_down):
    bsz, s, d = x.shape
    t = bsz * s
    tm = min(256, s)

    mod = _mod(c, w_ada, b_ada)
    shift1, scale1, gate1, shift2, scale2, gate2 = (mod[:, i * d:(i + 1) * d].reshape(bsz, 1, d) for i in range(6))

    bounds = np.cumsum((0, SSM_D_INNER, SSM_CONV_DIM, SSM_HEADS, RET_QK_WIDTH, RET_QK_WIDTH, RET_V_WIDTH,
                        RET_V_WIDTH, d, d))
    wz, wxbc, wdt, wq, wk, wv, wg, wgs, wgr = (w_in[:, bounds[i]:bounds[i + 1]] for i in range(9))
    nw1 = norm1_w.astype(F32).reshape(1, d)
    w_a = jnp.concatenate([wxbc, wz, jnp.pad(wdt, ((0, 0), (0, DT_PAD - SSM_HEADS)))], axis=1).astype(BF16)
    w_b = jnp.concatenate([wq, wk, wv, wg, wgs, wgr], axis=1).astype(BF16)
    xbc, z, dt = _inproj(x, nw1, scale1, shift1, w_a, (SSM_CONV_DIM, SSM_D_INNER, DT_PAD), (BF16, BF16, F32), tm)
    q, k, v, g, gs, gr = _inproj(x, nw1, scale1, shift1, w_b,
                                 (RET_QK_WIDTH, RET_QK_WIDTH, RET_V_WIDTH, RET_V_WIDTH, d, d), (BF16,) * 6, tm)

    ys = _ssd(xbc, z, dt, conv_w, conv_b, dt_bias, a_log, d_skip, ssm_norm_w)
    cos, sin = _rope(positions, tm)
    yr = _ret(q, k, v, g, cos, sin)
    x1, h2, scores_t = _merge(ys, yr, gs, gr, x, gate1, scale2, shift2, norm2_w, w_ssm_out, w_ret_out, w_out,
                              w_router, tm)

    idx_t, wts_t, rank_t, cnt = _route(scores_t, router_bias)

    counts = cnt[:, 0].astype(jnp.int32)
    padded = (counts + EXPERT_BLOCK - 1) // EXPERT_BLOCK * EXPERT_BLOCK
    padded_end = jnp.cumsum(padded)
    padded_start = padded_end - padded
    n_blocks = -(-(t * TOP_K) // EXPERT_BLOCK) + N_EXPERTS
    n_slots = n_blocks * EXPERT_BLOCK
    dest_t = _dest(idx_t, rank_t, padded_start)
    block_first = jnp.arange(n_blocks, dtype=jnp.int32) * EXPERT_BLOCK
    block_expert = jnp.minimum(jnp.sum(padded_end[None, :] <= block_first[:, None], axis=1), N_EXPERTS - 1)
    owner = (block_expert[:, None] == jnp.arange(N_EXPERTS, dtype=jnp.int32)[None, :]).astype(jnp.int32)
    region_end = jnp.sum(owner * (padded_start + counts)[None, :], axis=1)
    block_valid = jnp.clip(region_end - block_first, 0, EXPERT_BLOCK).astype(jnp.int32)
    xs = _scatter_rows(h2.reshape(t, PACKED), dest_t, n_slots)
    y_sorted = _experts(block_expert.astype(jnp.int32), block_valid, xs, w_exp_gate, w_exp_up, w_exp_down)
    yk = _gather_rows(y_sorted, dest_t.reshape(-1)).reshape(TOP_K, t, PACKED)
    return x1, h2, yk, wts_t.T, gate2


def kernel(x, c, positions, w_ada, b_ada, norm1_w, w_in, conv_w, conv_b, dt_bias, a_log, d_skip, ssm_norm_w,
           w_ssm_out, w_ret_out, w_out, norm2_w, w_router, router_bias, w_exp_gate, w_exp_up, w_exp_down,
           w_sh_gate, w_sh_up, w_sh_down, final_norm_w):
    assert w_ada.shape[0] == 1, "the final rmsnorm is fused into the single layer's last kernel"
    tm = min(256, x.shape[1])
    x1, h2, yk, wts, gate2 = _layer(
        x, c, positions, w_ada[0], b_ada[0], norm1_w[0], w_in[0], conv_w[0], conv_b[0], dt_bias[0], a_log[0],
        d_skip[0], ssm_norm_w[0], w_ssm_out[0], w_ret_out[0], w_out[0], norm2_w[0], w_router[0],
        router_bias[0], w_exp_gate[0], w_exp_up[0], w_exp_down[0], w_sh_gate[0], w_sh_up[0], w_sh_down[0])
    return _final(x1, h2, yk, wts, gate2, w_sh_gate[0], w_sh_up[0], w_sh_down[0], final_norm_w, tm)
```

```python
import functools

import numpy as np
import jax
import jax.numpy as jnp
from jax import lax
from jax.experimental import pallas as pl
from jax.experimental.pallas import tpu as pltpu
from jax.experimental.pallas import tpu_sc as plsc

F32 = jnp.float32
BF16 = jnp.bfloat16

EPS = 1e-6
D_MODEL = 1024
SSM_D_INNER = 2048
SSM_HEADDIM = 64
SSM_HEADS = 32
SSM_GROUPS = 4
SSM_STATE = 128
SSM_CONV = 4
SSM_CONV_DIM = SSM_D_INNER + 2 * SSM_GROUPS * SSM_STATE
GROUP_WIDTH = SSM_D_INNER // SSM_GROUPS
RET_HEADS = 4
RET_QK_DIM = 256
RET_V_DIM = 512
RET_QK_WIDTH = RET_HEADS * RET_QK_DIM
RET_V_WIDTH = RET_HEADS * RET_V_DIM
ROPE_BASE = 10000.0
ROPE_HALF = RET_QK_DIM // 2
N_EXPERTS = 256
TOP_K = 8
N_ROUTE_GROUPS = 8
TOPK_ROUTE_GROUPS = 4
ROUTE_GROUP_SIZE = N_EXPERTS // N_ROUTE_GROUPS
EXPERT_DIM = 256
ROUTED_SCALE = 2.5

TOKEN_TILE = 512
CHUNK = 128
SSD_STRIP = 256
LANES = 128
SUBLANES = 8
DT_PAD = LANES
EXPERT_BLOCK = 512
ROUTE_TILE = 256
VMEM_LIMIT = 56 * 1024 * 1024
PACKED = D_MODEL // 2
SC_CORES = 2
SC_SUBCORES = 16
SC_WINDOW = 64


LOG2E = 1.4426950408889634


def _silu(x):
    return x / (1.0 + jnp.exp2(x * -LOG2E))


def _sigmoid(x):
    return 1.0 / (1.0 + jnp.exp2(x * -LOG2E))


def _split3(a):
    hi = a.astype(BF16)
    r1 = a - hi.astype(F32)
    mid = r1.astype(BF16)
    lo = (r1 - mid.astype(F32)).astype(BF16)
    return hi, mid, lo


def _pack_rows(x):
    w = x.shape[1] // 2
    hi = lax.bitcast_convert_type(x[:, :w].astype(BF16).astype(F32), jnp.uint32)
    lo = lax.bitcast_convert_type(x[:, w:].astype(BF16).astype(F32), jnp.uint32)
    return lax.bitcast_convert_type(hi | (lo >> 16), jnp.int32)


def _unpack_rows(p):
    u = lax.bitcast_convert_type(p, jnp.uint32)
    hi = lax.bitcast_convert_type(u & jnp.uint32(0xFFFF0000), F32)
    lo = lax.bitcast_convert_type(u << 16, F32)
    return jnp.concatenate([hi, lo], axis=1).astype(BF16)


def _dot(a, b):
    return jnp.dot(a, b, preferred_element_type=F32)


def _dot_nt(a, b):
    return lax.dot_general(a, b, (((1,), (1,)), ((), ())), preferred_element_type=F32)


def _params(sem):
    return pltpu.CompilerParams(dimension_semantics=sem, vmem_limit_bytes=VMEM_LIMIT)


def _mod_kernel(c_ref, w_ref, b_ref, o_ref):
    o_ref[...] = _dot(_silu(c_ref[...]), w_ref[...]) + b_ref[...]


def _mod(c, w_ada, b_ada):
    bsz, d = c.shape
    n = w_ada.shape[1]
    return pl.pallas_call(
        _mod_kernel,
        grid=(n // d,),
        in_specs=[pl.BlockSpec((bsz, d), lambda j: (0, 0)),
                  pl.BlockSpec((d, d), lambda j: (0, j)),
                  pl.BlockSpec((1, d), lambda j: (0, j))],
        out_specs=pl.BlockSpec((bsz, d), lambda j: (0, j)),
        out_shape=jax.ShapeDtypeStruct((bsz, n), F32),
        compiler_params=_params(("arbitrary",)),
        name="mod",
    )(c, w_ada, b_ada.reshape(1, n))


def _rms_mod(x, nw, scale, shift):
    ms = jnp.mean(x * x, axis=-1, keepdims=True)
    return (x * lax.rsqrt(ms + EPS) * nw) * (1.0 + scale) + shift


def _inproj_kernel(x_ref, nw_ref, sc_ref, sh_ref, w_ref, *o_refs, widths):
    hb = _rms_mod(x_ref[0], nw_ref[...], sc_ref[0], sh_ref[0]).astype(BF16)
    off = 0
    for o_ref, wd in zip(o_refs, widths):
        for c0 in range(0, wd, 512):
            cw = min(512, wd - c0)
            o_ref[0, :, c0:c0 + cw] = _dot(hb, w_ref[:, off + c0:off + c0 + cw]).astype(o_ref.dtype)
        off += wd


def _inproj(x, nw, scale, shift, w, widths, dtypes, tm):
    bsz, s, d = x.shape
    vec = pl.BlockSpec((1, 1, d), lambda b, i: (b, 0, 0))
    return pl.pallas_call(
        functools.partial(_inproj_kernel, widths=widths),
        grid=(bsz, s // tm),
        in_specs=[pl.BlockSpec((1, tm, d), lambda b, i: (b, i, 0)),
                  pl.BlockSpec((1, d), lambda b, i: (0, 0)),
                  vec, vec,
                  pl.BlockSpec(w.shape, lambda b, i: (0, 0))],
        out_specs=[pl.BlockSpec((1, tm, wd), lambda b, i: (b, i, 0)) for wd in widths],
        out_shape=[jax.ShapeDtypeStruct((bsz, s, wd), dt) for wd, dt in zip(widths, dtypes)],
        compiler_params=_params(("parallel", "parallel")),
        name="inproj",
    )(x, nw, scale, shift, w)


def _rope_kernel(pos_ref, inv_ref, cos_ref, sin_ref):
    ang = pos_ref[0].astype(F32) * inv_ref[...]
    cos_ref[0] = jnp.cos(ang)
    sin_ref[0] = jnp.sin(ang)


def _rope(positions, tm):
    bsz, s = positions.shape
    inv = (1.0 / (ROPE_BASE ** (jnp.arange(ROPE_HALF, dtype=F32) / ROPE_HALF))).reshape(1, ROPE_HALF)
    spec = pl.BlockSpec((1, tm, ROPE_HALF), lambda b, i: (b, i, 0))
    return pl.pallas_call(
        _rope_kernel,
        grid=(bsz, s // tm),
        in_specs=[pl.BlockSpec((1, tm, 1), lambda b, i: (b, i, 0)),
                  pl.BlockSpec((1, ROPE_HALF), lambda b, i: (0, 0))],
        out_specs=[spec, spec],
        out_shape=[jax.ShapeDtypeStruct((bsz, s, ROPE_HALF), F32)] * 2,
        compiler_params=_params(("parallel", "parallel")),
        name="rope",
    )(positions.reshape(bsz, s, 1), inv)


def _ssd_kernel(xbc_ref, z_ref, dt_ref, cw_ref, cb_ref, dtb_ref, a_ref, dx_ref, nw_ref, r_ref, tril_ref, shift_ref,
                o_ref, tail_ref, state_ref, xs_ref, bc_ref, y_ref):
    L = CHUNK

    @pl.when(pl.program_id(1) == 0)
    def _():
        tail_ref[...] = jnp.zeros_like(tail_ref)
        state_ref[...] = jnp.zeros_like(state_ref)

    row8 = lax.broadcasted_iota(jnp.int32, (SUBLANES, 1), 0)
    for c0 in range(0, SSM_CONV_DIM, SSD_STRIP):
        cs = slice(c0, c0 + SSD_STRIP)
        u_b = xbc_ref[0, :, cs]
        u = u_b.astype(F32)
        tail = tail_ref[:, cs]
        acc = u * cw_ref[SSM_CONV - 1:SSM_CONV, cs] + cb_ref[:, cs]
        for d in range(1, SSM_CONV):
            sh = _dot(shift_ref[d - 1], u_b)
            head = sh[0:SUBLANES] + jnp.where(row8 < d, pltpu.roll(tail, d, axis=0), 0.0)
            sh = jnp.concatenate([head, sh[SUBLANES:]], axis=0)
            acc = acc + sh * cw_ref[SSM_CONV - 1 - d:SSM_CONV - d, cs]
        tail_ref[:, cs] = u[L - SUBLANES:L]
        act = _silu(acc)
        if c0 < SSM_D_INNER:
            xs_ref[:, cs] = act
        else:
            bc_ref[:, c0 - SSM_D_INNER:c0 - SSM_D_INNER + SSD_STRIP] = act.astype(BF16)

    dt_raw = dt_ref[0] + dtb_ref[...]
    dtv = jnp.maximum(dt_raw, 0.0) + jnp.log1p(jnp.exp(-jnp.abs(dt_raw)))
    a = dtv * (a_ref[...] * LOG2E)
    tril = tril_ref[...]
    a_hi, a_mid, a_lo = _split3(a)
    acum = _dot(tril, a_hi) + _dot(tril, a_mid) + _dot(tril, a_lo)
    col_t = (acum - jnp.log(dtv) * LOG2E).T
    last = acum[L - 1:L, :]
    e1_b = jnp.exp2(acum).astype(BF16)
    e2_b = (jnp.exp2(last - acum) * dtv).astype(BF16)
    l_hi, l_mid, l_lo = _split3(jnp.broadcast_to(jnp.exp2(last), (SUBLANES, LANES)))
    r = r_ref[...]
    decay_x = (_dot(l_hi, r) + _dot(l_mid, r) + _dot(l_lo, r))[0:1, :]

    causal = (lax.broadcasted_iota(jnp.int32, (L, L), 0) >= lax.broadcasted_iota(jnp.int32, (L, L), 1))
    lane_lo = lax.broadcasted_iota(jnp.int32, (L, LANES), 1) < SSM_HEADDIM
    heads_per_group = SSM_HEADS // SSM_GROUPS
    bc_off = SSM_GROUPS * SSM_STATE
    for g in range(SSM_GROUPS):
        bg_b = bc_ref[:, g * SSM_STATE:(g + 1) * SSM_STATE]
        cg_b = bc_ref[:, bc_off + g * SSM_STATE:bc_off + (g + 1) * SSM_STATE]
        cb = _dot_nt(cg_b, bg_b)
        bgt_b = bg_b.astype(F32).T.astype(BF16)
        for pair in range(heads_per_group // 2):
            p = g * (heads_per_group // 2) + pair
            xp = xs_ref[:, p * LANES:(p + 1) * LANES].astype(BF16)
            y_pair = None
            for sub in range(2):
                h = 2 * p + sub
                seg = jnp.broadcast_to(acum[:, h:h + 1], (L, L)) - col_t[h:h + 1, :]
                w = cb * jnp.exp2(jnp.where(causal, seg, -jnp.inf))
                xh = jnp.where(lane_lo if sub == 0 else jnp.logical_not(lane_lo), xp, jnp.zeros_like(xp))
                yh = _dot(w.astype(BF16), xh)
                y_pair = yh if y_pair is None else y_pair + yh
            y_ref[:, p * LANES:(p + 1) * LANES] = y_pair
        sq = jnp.zeros((L, 1), F32)
        for j in range(GROUP_WIDTH // SSD_STRIP):
            ls = slice(j * SSD_STRIP, (j + 1) * SSD_STRIP)
            cs = slice(g * GROUP_WIDTH + j * SSD_STRIP, g * GROUP_WIDTH + (j + 1) * SSD_STRIP)
            xs_j = xs_ref[:, cs]
            st_j = state_ref[g, :, ls]
            y_j = y_ref[:, cs] + _dot(cg_b, st_j.astype(BF16)) * _dot(e1_b, r_ref[:, cs]) + xs_j * dx_ref[:, cs]
            y_j = y_j * _silu(z_ref[0, :, cs].astype(F32))
            sq = sq + jnp.sum(y_j * y_j, axis=-1, keepdims=True)
            y_ref[:, cs] = y_j
            upd = _dot(bgt_b, (xs_j * _dot(e2_b, r_ref[:, cs])).astype(BF16))
            state_ref[g, :, ls] = st_j * decay_x[:, cs] + upd
        scale = lax.rsqrt(sq * (1.0 / GROUP_WIDTH) + EPS)
        for j in range(GROUP_WIDTH // SSD_STRIP):
            cs = slice(g * GROUP_WIDTH + j * SSD_STRIP, g * GROUP_WIDTH + (j + 1) * SSD_STRIP)
            o_ref[0, :, cs] = (y_ref[:, cs] * scale * nw_ref[:, cs]).astype(o_ref.dtype)


def _ssd(xbc, z, dt, conv_w, conv_b, dt_bias, a_log, d_skip, ssm_norm_w):
    bsz, s, _ = xbc.shape
    pad = DT_PAD - SSM_HEADS
    dtb = jnp.pad(dt_bias.astype(F32), (0, pad)).reshape(1, DT_PAD)
    a_neg = jnp.pad(-jnp.exp(a_log.astype(F32)), (0, pad)).reshape(1, DT_PAD)
    dx = jnp.repeat(d_skip.astype(F32), SSM_HEADDIM).reshape(1, SSM_D_INNER)
    expand = (jnp.arange(DT_PAD)[:, None] == (jnp.arange(SSM_D_INNER)[None, :] // SSM_HEADDIM)).astype(BF16)
    tril = (jnp.arange(CHUNK)[:, None] >= jnp.arange(CHUNK)[None, :]).astype(BF16)
    shift = jnp.stack([(jnp.arange(CHUNK)[:, None] - d == jnp.arange(CHUNK)[None, :]).astype(BF16)
                       for d in range(1, SSM_CONV)])

    def seq(width):
        return pl.BlockSpec((1, CHUNK, width), lambda b, c: (b, c, 0))

    def const(shape):
        return pl.BlockSpec(shape, lambda b, c: (0,) * len(shape))

    return pl.pallas_call(
        _ssd_kernel,
        grid=(bsz, s // CHUNK),
        in_specs=[seq(SSM_CONV_DIM), seq(SSM_D_INNER), seq(DT_PAD),
                  const((SSM_CONV, SSM_CONV_DIM)), const((1, SSM_CONV_DIM)), const((1, DT_PAD)), const((1, DT_PAD)),
                  const((1, SSM_D_INNER)), const((1, SSM_D_INNER)), const((DT_PAD, SSM_D_INNER)),
                  const((CHUNK, CHUNK)), const((SSM_CONV - 1, CHUNK, CHUNK))],
        out_specs=seq(SSM_D_INNER),
        out_shape=jax.ShapeDtypeStruct((bsz, s, SSM_D_INNER), BF16),
        scratch_shapes=[pltpu.VMEM((SUBLANES, SSM_CONV_DIM), F32),
                        pltpu.VMEM((SSM_GROUPS, SSM_STATE, GROUP_WIDTH), F32),
                        pltpu.VMEM((CHUNK, SSM_D_INNER), F32),
                        pltpu.VMEM((CHUNK, 2 * SSM_GROUPS * SSM_STATE), BF16),
                        pltpu.VMEM((CHUNK, SSM_D_INNER), F32)],
        compiler_params=_params(("parallel", "arbitrary")),
        name="ssd",
    )(xbc, z, dt, conv_w.astype(F32), conv_b.astype(F32).reshape(1, SSM_CONV_DIM), dtb, a_neg, dx,
      ssm_norm_w.astype(F32).reshape(1, SSM_D_INNER), expand, tril, shift)


def _ret_consts():
    lg = np.log1p(-(2.0 ** (-5.0 - np.arange(RET_HEADS, dtype=np.float64))))
    idx = np.arange(CHUNK, dtype=np.float64)
    rel = idx[:, None] - idx[None, :]
    intra = np.where(rel >= 0, np.exp(np.maximum(rel, 0.0)[None] * lg[:, None, None]), 0.0)
    qd = np.exp((idx + 1.0)[None, :] * lg[:, None])
    kd = np.exp((CHUNK - 1.0 - idx)[None, :] * lg[:, None]) * (RET_QK_DIM ** -0.5)
    qd = np.broadcast_to(qd[:, :, None], (RET_HEADS, CHUNK, RET_QK_DIM))
    kd = np.broadcast_to(kd[:, :, None], (RET_HEADS, CHUNK, RET_QK_DIM))
    chunk_decay = tuple(float(v) for v in np.exp(CHUNK * lg))
    return (jnp.asarray(intra, F32), jnp.asarray(qd, F32), jnp.asarray(kd, F32), chunk_decay)


def _ret_kernel(q_ref, k_ref, v_ref, g_ref, cos_ref, sin_ref, intra_ref, qd_ref, kd_ref, o_ref, state_ref, *,
                chunk_decay):
    @pl.when(pl.program_id(1) == 0)
    def _():
        state_ref[...] = jnp.zeros_like(state_ref)

    cos = cos_ref[0]
    sin = sin_ref[0]

    def rot(t):
        t1, t2 = t[:, :ROPE_HALF], t[:, ROPE_HALF:]
        return jnp.concatenate([t1 * cos - t2 * sin, t1 * sin + t2 * cos], axis=1)

    for h in range(RET_HEADS):
        qs = slice(h * RET_QK_DIM, (h + 1) * RET_QK_DIM)
        vs = slice(h * RET_V_DIM, (h + 1) * RET_V_DIM)
        qr = rot(q_ref[0, :, qs].astype(F32))
        kr = rot(k_ref[0, :, qs].astype(F32))
        vh = v_ref[0, :, vs]
        s = _dot_nt(qr.astype(BF16), (kr * (RET_QK_DIM ** -0.5)).astype(BF16)) * intra_ref[h]
        st = state_ref[h]
        y = _dot(s.astype(BF16), vh) + _dot((qr * qd_ref[h]).astype(BF16), st.astype(BF16))
        state_ref[h] = st * chunk_decay[h] + _dot((kr * kd_ref[h]).T.astype(BF16), vh)
        mu = jnp.mean(y, axis=-1, keepdims=True)
        yc = y - mu
        var = jnp.mean(yc * yc, axis=-1, keepdims=True)
        o_ref[0, :, vs] = (_silu(g_ref[0, :, vs].astype(F32)) * (yc * lax.rsqrt(var + EPS))).astype(o_ref.dtype)


def _ret(q, k, v, g, cos, sin):
    bsz, s, _ = q.shape
    intra, qd, kd, chunk_decay = _ret_consts()

    def seq(width):
        return pl.BlockSpec((1, CHUNK, width), lambda b, c: (b, c, 0))

    def const(shape):
        return pl.BlockSpec(shape, lambda b, c: (0,) * len(shape))

    return pl.pallas_call(
        functools.partial(_ret_kernel, chunk_decay=chunk_decay),
        grid=(bsz, s // CHUNK),
        in_specs=[seq(RET_QK_WIDTH), seq(RET_QK_WIDTH), seq(RET_V_WIDTH), seq(RET_V_WIDTH),
                  seq(ROPE_HALF), seq(ROPE_HALF),
                  const(intra.shape), const(qd.shape), const(kd.shape)],
        out_specs=seq(RET_V_WIDTH),
        out_shape=jax.ShapeDtypeStruct((bsz, s, RET_V_WIDTH), BF16),
        scratch_shapes=[pltpu.VMEM((RET_HEADS, RET_QK_DIM, RET_V_DIM), F32)],
        compiler_params=_params(("parallel", "arbitrary")),
        name="ret",
    )(q, k, v, g, cos, sin, intra, qd, kd)


def _merge_kernel(ys_ref, yr_ref, gs_ref, gr_ref, x_ref, g1_ref, sc2_ref, sh2_ref, n2w_ref,
                  wso_ref, wro_ref, wo_ref, wrh_ref, wrl_ref, x1_ref, h2_ref, sct_ref):
    y_ssm = _dot(ys_ref[0], wso_ref[...])
    y_ret = _dot(yr_ref[0], wro_ref[...])
    merged = _sigmoid(gs_ref[0].astype(F32)) * y_ssm + _sigmoid(gr_ref[0].astype(F32)) * y_ret
    x1 = x_ref[0] + g1_ref[0] * _dot(merged.astype(BF16), wo_ref[...])
    x1_ref[0] = x1
    h2 = _rms_mod(x1, n2w_ref[...], sc2_ref[0], sh2_ref[0])
    h_hi = h2.astype(BF16)
    h2_ref[0] = _pack_rows(h2)
    h_lo = (h2 - h_hi.astype(F32)).astype(BF16)
    wrh = wrh_ref[...]
    logits_t = _dot_nt(wrh, h_hi) + _dot_nt(wrh, h_lo) + _dot_nt(wrl_ref[...], h_hi)
    sct_ref[...] = _sigmoid(logits_t)


def _merge(ys, yr, gs, gr, x, gate1, scale2, shift2, norm2_w, w_ssm_out, w_ret_out, w_out, w_router, tm):
    bsz, s, d = x.shape
    nt = s // tm
    wrt = w_router.astype(F32).T
    wrh = wrt.astype(BF16)
    wrl = (wrt - wrh.astype(F32)).astype(BF16)

    def seq(width):
        return pl.BlockSpec((1, tm, width), lambda b, i: (b, i, 0))

    def const(shape):
        return pl.BlockSpec(shape, lambda b, i: (0,) * len(shape))

    vec = pl.BlockSpec((1, 1, d), lambda b, i: (b, 0, 0))
    return pl.pallas_call(
        _merge_kernel,
        grid=(bsz, nt),
        in_specs=[seq(SSM_D_INNER), seq(RET_V_WIDTH), seq(d), seq(d), seq(d), vec, vec, vec, const((1, d)),
                  const((SSM_D_INNER, d)), const((RET_V_WIDTH, d)), const((d, d)),
                  const((N_EXPERTS, d)), const((N_EXPERTS, d))],
        out_specs=[seq(d), seq(PACKED), pl.BlockSpec((N_EXPERTS, tm), lambda b, i: (0, b * nt + i))],
        out_shape=[jax.ShapeDtypeStruct((bsz, s, d), F32), jax.ShapeDtypeStruct((bsz, s, PACKED), jnp.int32),
                   jax.ShapeDtypeStruct((N_EXPERTS, bsz * s), F32)],
        compiler_params=_params(("parallel", "parallel")),
        name="merge",
    )(ys, yr, gs, gr, x, gate1, scale2, shift2, norm2_w.astype(F32).reshape(1, d),
      w_ssm_out.astype(BF16), w_ret_out.astype(BF16), w_out.astype(BF16), wrh, wrl)


def _route_kernel(sc_ref, bias_ref, upper_ref, ones_ref, idx_ref, wts_ref, rank_ref, cnt_ref, carry_ref):
    @pl.when(pl.program_id(0) == 0)
    def _():
        carry_ref[...] = jnp.zeros_like(carry_ref)

    tl = sc_ref.shape[1]
    neg = -jnp.inf
    scores = sc_ref[...]
    sel = scores + bias_ref[...]

    def first_argmax(vals, iota, n):
        m = jnp.max(vals, axis=0, keepdims=True)
        return jnp.min(jnp.where(vals == m, iota, float(n)), axis=0, keepdims=True), m

    io_g = lax.broadcasted_iota(jnp.int32, (ROUTE_GROUP_SIZE, tl), 0).astype(F32)
    rows = []
    for g in range(N_ROUTE_GROUPS):
        blk = sel[g * ROUTE_GROUP_SIZE:(g + 1) * ROUTE_GROUP_SIZE]
        i1, m1 = first_argmax(blk, io_g, ROUTE_GROUP_SIZE)
        m2 = jnp.max(jnp.where(io_g == i1, neg, blk), axis=0, keepdims=True)
        rows.append(m1 + m2)
    gsc = jnp.concatenate(rows, axis=0)
    io_8 = lax.broadcasted_iota(jnp.int32, (N_ROUTE_GROUPS, tl), 0).astype(F32)
    chosen = jnp.zeros((N_ROUTE_GROUPS, tl), F32)
    for _ in range(TOPK_ROUTE_GROUPS):
        i, _m = first_argmax(gsc, io_8, N_ROUTE_GROUPS)
        hit = io_8 == i
        chosen = jnp.where(hit, 1.0, chosen)
        gsc = jnp.where(hit, neg, gsc)
    msel = jnp.concatenate(
        [jnp.where(chosen[g:g + 1] > 0.5, sel[g * ROUTE_GROUP_SIZE:(g + 1) * ROUTE_GROUP_SIZE], neg)
         for g in range(N_ROUTE_GROUPS)], axis=0)

    io_e = lax.broadcasted_iota(jnp.int32, (N_EXPERTS, tl), 0).astype(F32)
    multi = jnp.zeros((N_EXPERTS, tl), F32)
    idx_rows, sc_rows = [], []
    for _ in range(TOP_K):
        i, _m = first_argmax(msel, io_e, N_EXPERTS)
        hit = io_e == i
        sc_rows.append(jnp.sum(jnp.where(hit, scores, 0.0), axis=0, keepdims=True))
        msel = jnp.where(hit, neg, msel)
        multi = jnp.where(hit, 1.0, multi)
        idx_rows.append(i)
    den = sc_rows[0]
    for r in sc_rows[1:]:
        den = den + r
    den = den + 1e-20
    wts_ref[...] = jnp.concatenate([r / den * ROUTED_SCALE for r in sc_rows], axis=0)
    idx_ref[...] = jnp.concatenate(idx_rows, axis=0).astype(jnp.int32)

    multi_b = multi.astype(BF16)
    carry = carry_ref[...]
    rank_full = _dot(multi_b, upper_ref[...]) + carry
    rank_ref[...] = jnp.concatenate(
        [jnp.sum(jnp.where(io_e == i, rank_full, 0.0), axis=0, keepdims=True) for i in idx_rows],
        axis=0).astype(jnp.int32)
    carry = carry + _dot(multi_b, ones_ref[...])
    carry_ref[...] = carry
    cnt_ref[...] = carry[:, :LANES]


def _route(scores_t, router_bias):
    e, t = scores_t.shape
    tl = min(ROUTE_TILE, t)
    bias = jnp.broadcast_to(router_bias.astype(F32).reshape(e, 1), (e, tl))
    upper = (jnp.arange(tl)[:, None] < jnp.arange(tl)[None, :]).astype(BF16)
    ones = jnp.ones((tl, tl), BF16)
    tok = pl.BlockSpec((TOP_K, tl), lambda i: (0, i))

    def const(shape):
        return pl.BlockSpec(shape, lambda i: (0,) * len(shape))

    return pl.pallas_call(
        _route_kernel,
        grid=(t // tl,),
        in_specs=[pl.BlockSpec((e, tl), lambda i: (0, i)), const((e, tl)), const((tl, tl)), const((tl, tl))],
        out_specs=[tok, tok, tok, const((e, LANES))],
        out_shape=[jax.ShapeDtypeStruct((TOP_K, t), jnp.int32), jax.ShapeDtypeStruct((TOP_K, t), F32),
                   jax.ShapeDtypeStruct((TOP_K, t), jnp.int32), jax.ShapeDtypeStruct((e, LANES), F32)],
        scratch_shapes=[pltpu.VMEM((e, tl), F32)],
        compiler_params=_params(("arbitrary",)),
        name="route",
    )(scores_t, bias, upper, ones)


def _dest_kernel(idx_ref, rank_ref, start_ref, o_ref):
    tl = idx_ref.shape[1]
    io_e = lax.broadcasted_iota(jnp.int32, (N_EXPERTS, tl), 0).astype(F32)
    idx = idx_ref[...].astype(F32)
    start = start_ref[...]
    rows = [jnp.sum(jnp.where(io_e == idx[kk:kk + 1], start, 0.0), axis=0, keepdims=True) for kk in range(TOP_K)]
    o_ref[...] = jnp.concatenate(rows, axis=0).astype(jnp.int32) + rank_ref[...]


def _dest(idx_t, rank_t, padded_start):
    k, t = idx_t.shape
    tl = min(2 * ROUTE_TILE, t)
    start = jnp.broadcast_to(padded_start.astype(F32).reshape(N_EXPERTS, 1), (N_EXPERTS, tl))
    tok = pl.BlockSpec((k, tl), lambda i: (0, i))
    return pl.pallas_call(
        _dest_kernel,
        grid=(t // tl,),
        in_specs=[tok, tok, pl.BlockSpec((N_EXPERTS, tl), lambda i: (0, 0))],
        out_specs=tok,
        out_shape=jax.ShapeDtypeStruct((k, t), jnp.int32),
        compiler_params=_params(("parallel",)),
        name="dest",
    )(idx_t, rank_t, start)


def _expert_kernel(be_ref, nv_ref, x_ref, wg_ref, wu_ref, wd_ref, o_ref, wgb_ref, wub_ref, wdb_ref):
    b = pl.program_id(0)
    nv = nv_ref[b]

    @pl.when(jnp.logical_or(b == 0, be_ref[b] != be_ref[jnp.maximum(b - 1, 0)]))
    def _():
        wgb_ref[...] = wg_ref[0].astype(BF16)
        wub_ref[...] = wu_ref[0].astype(BF16)
        wdb_ref[...] = wd_ref[0].astype(BF16)

    @pl.when(nv > 0)
    def _():
        row = lax.broadcasted_iota(jnp.int32, x_ref.shape, 0)
        x = _unpack_rows(jnp.where(row < nv, x_ref[...], 0))
        g = _dot(x, wgb_ref[...])
        u = _dot(x, wub_ref[...])
        o_ref[...] = _pack_rows(_dot((_silu(g) * u).astype(BF16), wdb_ref[...]))

    @pl.when(nv == 0)
    def _():
        o_ref[...] = jnp.zeros_like(o_ref)


def _experts(block_expert, block_valid, xs, w_gate, w_up, w_down):
    n_slots = xs.shape[0]
    n_blocks = n_slots // EXPERT_BLOCK
    d, f = w_gate.shape[-2:]
    grid_spec = pltpu.PrefetchScalarGridSpec(
        num_scalar_prefetch=2,
        grid=(n_blocks,),
        in_specs=[pl.BlockSpec((EXPERT_BLOCK, PACKED), lambda b, be, nv: (b, 0)),
                  pl.BlockSpec((1, d, f), lambda b, be, nv: (be[b], 0, 0)),
                  pl.BlockSpec((1, d, f), lambda b, be, nv: (be[b], 0, 0)),
                  pl.BlockSpec((1, f, d), lambda b, be, nv: (be[b], 0, 0))],
        out_specs=pl.BlockSpec((EXPERT_BLOCK, PACKED), lambda b, be, nv: (b, 0)),
        scratch_shapes=[pltpu.VMEM((d, f), BF16), pltpu.VMEM((d, f), BF16), pltpu.VMEM((f, d), BF16)],
    )
    return pl.pallas_call(
        _expert_kernel,
        grid_spec=grid_spec,
        out_shape=jax.ShapeDtypeStruct((n_slots, PACKED), jnp.int32),
        compiler_params=_params(("arbitrary",)),
        name="experts",
    )(block_expert, block_valid, xs, w_gate, w_up, w_down)


def _final_kernel(x1_ref, h2_ref, yk_ref, wts_ref, g2_ref, wsg_ref, wsu_ref, wsd_ref, fnw_ref, o_ref):
    h = _unpack_rows(h2_ref[0])
    act = (_silu(_dot(h, wsg_ref[...])) * _dot(h, wsu_ref[...])).astype(BF16)
    moe = _dot(act, wsd_ref[...])
    wts = wts_ref[...]
    for kk in range(TOP_K):
        moe = moe + _unpack_rows(yk_ref[kk]).astype(F32) * wts[:, kk:kk + 1]
    x2 = x1_ref[0] + g2_ref[0] * moe
    ms = jnp.mean(x2 * x2, axis=-1, keepdims=True)
    o_ref[0] = x2 * lax.rsqrt(ms + EPS) * fnw_ref[...]


def _final(x1, h2p, yk, wts, gate2, w_sh_gate, w_sh_up, w_sh_down, final_norm_w, tm):
    bsz, s, d = x1.shape
    nt = s // tm
    f = w_sh_gate.shape[-1]
    seq = pl.BlockSpec((1, tm, d), lambda b, i: (b, i, 0))

    def const(shape):
        return pl.BlockSpec(shape, lambda b, i: (0,) * len(shape))

    return pl.pallas_call(
        _final_kernel,
        grid=(bsz, nt),
        in_specs=[seq, pl.BlockSpec((1, tm, PACKED), lambda b, i: (b, i, 0)),
                  pl.BlockSpec((TOP_K, tm, PACKED), lambda b, i: (0, b * nt + i, 0)),
                  pl.BlockSpec((tm, TOP_K), lambda b, i: (b * nt + i, 0)),
                  pl.BlockSpec((1, 1, d), lambda b, i: (b, 0, 0)),
                  const((d, f)), const((d, f)), const((f, d)), const((1, d))],
        out_specs=seq,
        out_shape=jax.ShapeDtypeStruct((bsz, s, d), F32),
        compiler_params=_params(("parallel", "parallel")),
        name="final",
    )(x1, h2p, yk, wts, gate2, w_sh_gate.astype(BF16), w_sh_up.astype(BF16), w_sh_down.astype(BF16),
      final_norm_w.astype(F32).reshape(1, d))


def _gather_rows(table, idx):
    n = idx.shape[0]
    width = table.shape[1]
    workers = SC_CORES * SC_SUBCORES
    per_worker = n // workers
    steps = per_worker // SC_WINDOW
    assert per_worker * workers == n and steps * SC_WINDOW == per_worker
    mesh = plsc.VectorSubcoreMesh(core_axis_name="c", subcore_axis_name="s")

    @functools.partial(
        pl.kernel, mesh=mesh,
        out_type=jax.ShapeDtypeStruct((n, width), table.dtype),
        scratch_types=[pltpu.VMEM((SC_WINDOW,), jnp.int32),
                       pltpu.VMEM((SC_WINDOW, width), table.dtype),
                       pltpu.SemaphoreType.DMA],
        name="gather_rows",
    )
    def run(table_hbm, idx_hbm, out_hbm, idx_v, rows_v, sem):
        base = (lax.axis_index("s") * SC_CORES + lax.axis_index("c")) * per_worker

        @pl.loop(0, steps)
        def _(i):
            off = pl.multiple_of(base + i * SC_WINDOW, SC_WINDOW)
            pltpu.sync_copy(idx_hbm.at[pl.ds(off, SC_WINDOW)], idx_v)
            pltpu.async_copy(table_hbm.at[idx_v], rows_v, sem).wait()
            pltpu.sync_copy(rows_v, out_hbm.at[pl.ds(off, SC_WINDOW)])

    return run(table, idx)


def _scatter_rows(rows, dest_kt, n_slots):
    t, width = rows.shape
    top_k = dest_kt.shape[0]
    workers = SC_CORES * SC_SUBCORES
    window = LANES
    per_worker = t // workers
    steps = per_worker // window
    assert per_worker * workers == t and steps * window == per_worker
    mesh = plsc.VectorSubcoreMesh(core_axis_name="c", subcore_axis_name="s")

    @functools.partial(
        pl.kernel, mesh=mesh,
        out_type=jax.ShapeDtypeStruct((n_slots, width), rows.dtype),
        scratch_types=[pltpu.VMEM((top_k, window), jnp.int32),
                       pltpu.VMEM((window, width), rows.dtype),
                       pltpu.SemaphoreType.DMA],
        name="scatter_rows",
    )
    def run(rows_hbm, dest_hbm, out_hbm, idx_v, rows_v, sem):
        base = (lax.axis_index("s") * SC_CORES + lax.axis_index("c")) * per_worker

        @pl.loop(0, steps)
        def _(i):
            off = pl.multiple_of(base + i * window, window)
            pltpu.sync_copy(rows_hbm.at[pl.ds(off, window)], rows_v)
            pltpu.sync_copy(dest_hbm.at[:, pl.ds(off, window)], idx_v)
            copies = [pltpu.async_copy(rows_v, out_hbm.at[idx_v.at[kk]], sem) for kk in range(top_k)]
            for cp in copies:
                cp.wait()

    return run(rows, dest_kt)


def _layer(x, c, positions, w_ada, b_ada, norm1_w, w_in, conv_w, conv_b, dt_bias, a_log, d_skip, ssm_norm_w,
           w_ssm_out, w_ret_out, w_out, norm2_w, w_router, router_bias, w_exp_gate, w_exp_up, w_exp_down,
           w_sh_gate, w_sh_up, w_sh_down):
    bsz, s, d = x.shape
    t = bsz * s
    tm = min(TOKEN_TILE, s)

    mod = _mod(c, w_ada, b_ada)
    shift1, scale1, gate1, shift2, scale2, gate2 = (mod[:, i * d:(i + 1) * d].reshape(bsz, 1, d) for i in range(6))

    bounds = np.cumsum((0, SSM_D_INNER, SSM_CONV_DIM, SSM_HEADS, RET_QK_WIDTH, RET_QK_WIDTH, RET_V_WIDTH,
                        RET_V_WIDTH, d, d))
    wz, wxbc, wdt, wq, wk, wv, wg, wgs, wgr = (w_in[:, bounds[i]:bounds[i + 1]] for i in range(9))
    nw1 = norm1_w.astype(F32).reshape(1, d)
    w_a = jnp.concatenate([wxbc, wz, jnp.pad(wdt, ((0, 0), (0, DT_PAD - SSM_HEADS)))], axis=1).astype(BF16)
    w_b = jnp.concatenate([wq, wk, wv, wg, wgs, wgr], axis=1).astype(BF16)
    xbc, z, dt = _inproj(x, nw1, scale1, shift1, w_a, (SSM_CONV_DIM, SSM_D_INNER, DT_PAD), (BF16, BF16, F32), tm)
    q, k, v, g, gs, gr = _inproj(x, nw1, scale1, shift1, w_b,
                                 (RET_QK_WIDTH, RET_QK_WIDTH, RET_V_WIDTH, RET_V_WIDTH, d, d), (BF16,) * 6, tm)

    ys = _ssd(xbc, z, dt, conv_w, conv_b, dt_bias, a_log, d_skip, ssm_norm_w)
    cos, sin = _rope(positions, tm)
    yr = _ret(q, k, v, g, cos, sin)
    x1, h2, scores_t = _merge(ys, yr, gs, gr, x, gate1, scale2, shift2, norm2_w, w_ssm_out, w_ret_out, w_out,
                              w_router, tm)

    idx_t, wts_t, rank_t, cnt = _route(scores_t, router_bias)

    counts = cnt[:, 0].astype(jnp.int32)
    padded = (counts + EXPERT_BLOCK - 1) // EXPERT_BLOCK * EXPERT_BLOCK
    padded_end = jnp.cumsum(padded)
    padded_start = padded_end - padded
    n_blocks = -(-(t * TOP_K) // EXPERT_BLOCK) + N_EXPERTS
    n_slots = n_blocks * EXPERT_BLOCK
    dest_t = _dest(idx_t, rank_t, padded_start)
    block_first = jnp.arange(n_blocks, dtype=jnp.int32) * EXPERT_BLOCK
    block_expert = jnp.minimum(jnp.sum(padded_end[None, :] <= block_first[:, None], axis=1), N_EXPERTS - 1)
    owner = (block_expert[:, None] == jnp.arange(N_EXPERTS, dtype=jnp.int32)[None, :]).astype(jnp.int32)
    region_end = jnp.sum(owner * (padded_start + counts)[None, :], axis=1)
    block_valid = jnp.clip(region_end - block_first, 0, EXPERT_BLOCK).astype(jnp.int32)
    xs = _scatter_rows(h2.reshape(t, PACKED), dest_t, n_slots)
    y_sorted = _experts(block_expert.astype(jnp.int32), block_valid, xs, w_exp_gate, w_exp_up, w_exp_down)
    yk = _gather_rows(y_sorted, dest_t.reshape(-1)).reshape(TOP_K, t, PACKED)
    return x1, h2, yk, wts_t.T, gate2


def kernel(x, c, positions, w_ada, b_ada, norm1_w, w_in, conv_w, conv_b, dt_bias, a_log, d_skip, ssm_norm_w,
           w_ssm_out, w_ret_out, w_out, norm2_w, w_router, router_bias, w_exp_gate, w_exp_up, w_exp_down,
           w_sh_gate, w_sh_up, w_sh_down, final_norm_w):
    assert w_ada.shape[0] == 1, "the final rmsnorm is fused into the single layer's last kernel"
    tm = min(TOKEN_TILE, x.shape[1])
    x1, h2, yk, wts, gate2 = _layer(
        x, c, positions, w_ada[0], b_ada[0], norm1_w[0], w_in[0], conv_w[0], conv_b[0], dt_bias[0], a_log[0],
        d_skip[0], ssm_norm_w[0], w_ssm_out[0], w_ret_out[0], w_out[0], norm2_w[0], w_router[0],
        router_bias[0], w_exp_gate[0], w_exp_up[0], w_exp_down[0], w_sh_gate[0], w_sh_up[0], w_sh_down[0])
    return _final(x1, h2, yk, wts, gate2, w_sh_gate[0], w_sh_up[0], w_sh_down[0], final_norm_w, tm)
```

```python
import functools

import numpy as np
import jax
import jax.numpy as jnp
from jax import lax
from jax.experimental import pallas as pl
from jax.experimental.pallas import tpu as pltpu
from jax.experimental.pallas import tpu_sc as plsc

F32 = jnp.float32
BF16 = jnp.bfloat16

EPS = 1e-6
D_MODEL = 1024
SSM_D_INNER = 2048
SSM_HEADDIM = 64
SSM_HEADS = 32
SSM_GROUPS = 4
SSM_STATE = 128
SSM_CONV = 4
SSM_CONV_DIM = SSM_D_INNER + 2 * SSM_GROUPS * SSM_STATE
GROUP_WIDTH = SSM_D_INNER // SSM_GROUPS
RET_HEADS = 4
RET_QK_DIM = 256
RET_V_DIM = 512
RET_QK_WIDTH = RET_HEADS * RET_QK_DIM
RET_V_WIDTH = RET_HEADS * RET_V_DIM
ROPE_BASE = 10000.0
ROPE_HALF = RET_QK_DIM // 2
N_EXPERTS = 256
TOP_K = 8
N_ROUTE_GROUPS = 8
TOPK_ROUTE_GROUPS = 4
ROUTE_GROUP_SIZE = N_EXPERTS // N_ROUTE_GROUPS
EXPERT_DIM = 256
ROUTED_SCALE = 2.5

TOKEN_TILE = 512
CHUNK = 128
SSD_STRIP = 256
LANES = 128
SUBLANES = 8
DT_PAD = LANES
EXPERT_BLOCK = 512
ROUTE_TILE = 256
VMEM_LIMIT = 56 * 1024 * 1024
PACKED = D_MODEL // 2
SC_CORES = 2
SC_SUBCORES = 16
SC_WINDOW = 64


LOG2E = 1.4426950408889634


def _silu(x):
    return x / (1.0 + jnp.exp2(x * -LOG2E))


def _sigmoid(x):
    return 1.0 / (1.0 + jnp.exp2(x * -LOG2E))


def _split3(a):
    hi = a.astype(BF16)
    r1 = a - hi.astype(F32)
    mid = r1.astype(BF16)
    lo = (r1 - mid.astype(F32)).astype(BF16)
    return hi, mid, lo


def _pack_rows(x):
    w = x.shape[1] // 2
    hi = lax.bitcast_convert_type(x[:, :w].astype(BF16).astype(F32), jnp.uint32)
    lo = lax.bitcast_convert_type(x[:, w:].astype(BF16).astype(F32), jnp.uint32)
    return lax.bitcast_convert_type(hi | (lo >> 16), jnp.int32)


def _unpack_rows(p):
    u = lax.bitcast_convert_type(p, jnp.uint32)
    hi = lax.bitcast_convert_type(u & jnp.uint32(0xFFFF0000), F32)
    lo = lax.bitcast_convert_type(u << 16, F32)
    return jnp.concatenate([hi, lo], axis=1).astype(BF16)


def _dot(a, b):
    return jnp.dot(a, b, preferred_element_type=F32)


def _dot_nt(a, b):
    return lax.dot_general(a, b, (((1,), (1,)), ((), ())), preferred_element_type=F32)


def _params(sem):
    return pltpu.CompilerParams(dimension_semantics=sem, vmem_limit_bytes=VMEM_LIMIT)


def _mod_kernel(c_ref, w_ref, b_ref, o_ref):
    o_ref[...] = _dot(_silu(c_ref[...]), w_ref[...]) + b_ref[...]


def _mod(c, w_ada, b_ada):
    bsz, d = c.shape
    n = w_ada.shape[1]
    return pl.pallas_call(
        _mod_kernel,
        grid=(n // d,),
        in_specs=[pl.BlockSpec((bsz, d), lambda j: (0, 0)),
                  pl.BlockSpec((d, d), lambda j: (0, j)),
                  pl.BlockSpec((1, d), lambda j: (0, j))],
        out_specs=pl.BlockSpec((bsz, d), lambda j: (0, j)),
        out_shape=jax.ShapeDtypeStruct((bsz, n), F32),
        compiler_params=_params(("arbitrary",)),
        name="mod",
    )(c, w_ada, b_ada.reshape(1, n))


def _rms_mod(x, nw, scale, shift):
    ms = jnp.mean(x * x, axis=-1, keepdims=True)
    return (x * lax.rsqrt(ms + EPS) * nw) * (1.0 + scale) + shift


def _inproj_kernel(x_ref, nw_ref, sc_ref, sh_ref, w_ref, *o_refs, widths):
    hb = _rms_mod(x_ref[0], nw_ref[...], sc_ref[0], sh_ref[0]).astype(BF16)
    off = 0
    for o_ref, wd in zip(o_refs, widths):
        for c0 in range(0, wd, 512):
            cw = min(512, wd - c0)
            o_ref[0, :, c0:c0 + cw] = _dot(hb, w_ref[:, off + c0:off + c0 + cw]).astype(o_ref.dtype)
        off += wd


def _inproj(x, nw, scale, shift, w, widths, dtypes, tm):
    bsz, s, d = x.shape
    vec = pl.BlockSpec((1, 1, d), lambda b, i: (b, 0, 0))
    return pl.pallas_call(
        functools.partial(_inproj_kernel, widths=widths),
        grid=(bsz, s // tm),
        in_specs=[pl.BlockSpec((1, tm, d), lambda b, i: (b, i, 0)),
                  pl.BlockSpec((1, d), lambda b, i: (0, 0)),
                  vec, vec,
                  pl.BlockSpec(w.shape, lambda b, i: (0, 0))],
        out_specs=[pl.BlockSpec((1, tm, wd), lambda b, i: (b, i, 0)) for wd in widths],
        out_shape=[jax.ShapeDtypeStruct((bsz, s, wd), dt) for wd, dt in zip(widths, dtypes)],
        compiler_params=_params(("parallel", "parallel")),
        name="inproj",
    )(x, nw, scale, shift, w)


def _rope_kernel(pos_ref, inv_ref, cos_ref, sin_ref):
    ang = pos_ref[0].astype(F32) * inv_ref[...]
    cos_ref[0] = jnp.cos(ang)
    sin_ref[0] = jnp.sin(ang)


def _rope(positions, tm):
    bsz, s = positions.shape
    inv = (1.0 / (ROPE_BASE ** (jnp.arange(ROPE_HALF, dtype=F32) / ROPE_HALF))).reshape(1, ROPE_HALF)
    spec = pl.BlockSpec((1, tm, ROPE_HALF), lambda b, i: (b, i, 0))
    return pl.pallas_call(
        _rope_kernel,
        grid=(bsz, s // tm),
        in_specs=[pl.BlockSpec((1, tm, 1), lambda b, i: (b, i, 0)),
                  pl.BlockSpec((1, ROPE_HALF), lambda b, i: (0, 0))],
        out_specs=[spec, spec],
        out_shape=[jax.ShapeDtypeStruct((bsz, s, ROPE_HALF), F32)] * 2,
        compiler_params=_params(("parallel", "parallel")),
        name="rope",
    )(positions.reshape(bsz, s, 1), inv)


def _ssd_kernel(xbc_ref, z_ref, dt_ref, cw_ref, cb_ref, dtb_ref, a_ref, dx_ref, nw_ref, r_ref, tril_ref, shift_ref,
                o_ref, tail_ref, state_ref, xs_ref, bc_ref, y_ref):
    L = CHUNK

    @pl.when(pl.program_id(1) == 0)
    def _():
        tail_ref[...] = jnp.zeros_like(tail_ref)
        state_ref[...] = jnp.zeros_like(state_ref)

    row8 = lax.broadcasted_iota(jnp.int32, (SUBLANES, 1), 0)
    for c0 in range(0, SSM_CONV_DIM, SSD_STRIP):
        cs = slice(c0, c0 + SSD_STRIP)
        u_b = xbc_ref[0, :, cs]
        u = u_b.astype(F32)
        tail = tail_ref[:, cs]
        acc = u * cw_ref[SSM_CONV - 1:SSM_CONV, cs] + cb_ref[:, cs]
        for d in range(1, SSM_CONV):
            sh = _dot(shift_ref[d - 1], u_b)
            head = sh[0:SUBLANES] + jnp.where(row8 < d, pltpu.roll(tail, d, axis=0), 0.0)
            sh = jnp.concatenate([head, sh[SUBLANES:]], axis=0)
            acc = acc + sh * cw_ref[SSM_CONV - 1 - d:SSM_CONV - d, cs]
        tail_ref[:, cs] = u[L - SUBLANES:L]
        act = _silu(acc)
        if c0 < SSM_D_INNER:
            xs_ref[:, cs] = act
        else:
            bc_ref[:, c0 - SSM_D_INNER:c0 - SSM_D_INNER + SSD_STRIP] = act.astype(BF16)

    dt_raw = dt_ref[0] + dtb_ref[...]
    dtv = jnp.maximum(dt_raw, 0.0) + jnp.log1p(jnp.exp(-jnp.abs(dt_raw)))
    a = dtv * (a_ref[...] * LOG2E)
    tril = tril_ref[...]
    a_hi, a_mid, a_lo = _split3(a)
    acum = _dot(tril, a_hi) + _dot(tril, a_mid) + _dot(tril, a_lo)
    col_t = (acum - jnp.log(dtv) * LOG2E).T
    last = acum[L - 1:L, :]
    e1_b = jnp.exp2(acum).astype(BF16)
    e2_b = (jnp.exp2(last - acum) * dtv).astype(BF16)
    l_hi, l_mid, l_lo = _split3(jnp.broadcast_to(jnp.exp2(last), (SUBLANES, LANES)))
    r = r_ref[...]
    decay_x = (_dot(l_hi, r) + _dot(l_mid, r) + _dot(l_lo, r))[0:1, :]

    causal = (lax.broadcasted_iota(jnp.int32, (L, L), 0) >= lax.broadcasted_iota(jnp.int32, (L, L), 1))
    lane_lo = lax.broadcasted_iota(jnp.int32, (L, LANES), 1) < SSM_HEADDIM
    heads_per_group = SSM_HEADS // SSM_GROUPS
    bc_off = SSM_GROUPS * SSM_STATE
    for g in range(SSM_GROUPS):
        bg_b = bc_ref[:, g * SSM_STATE:(g + 1) * SSM_STATE]
        cg_b = bc_ref[:, bc_off + g * SSM_STATE:bc_off + (g + 1) * SSM_STATE]
        cb = _dot_nt(cg_b, bg_b)
        bgt_b = bg_b.astype(F32).T.astype(BF16)
        for pair in range(heads_per_group // 2):
            p = g * (heads_per_group // 2) + pair
            xp = xs_ref[:, p * LANES:(p + 1) * LANES].astype(BF16)
            y_pair = None
            for sub in range(2):
                h = 2 * p + sub
                seg = jnp.broadcast_to(acum[:, h:h + 1], (L, L)) - col_t[h:h + 1, :]
                w = cb * jnp.exp2(jnp.where(causal, seg, -jnp.inf))
                xh = jnp.where(lane_lo if sub == 0 else jnp.logical_not(lane_lo), xp, jnp.zeros_like(xp))
                yh = _dot(w.astype(BF16), xh)
                y_pair = yh if y_pair is None else y_pair + yh
            y_ref[:, p * LANES:(p + 1) * LANES] = y_pair
        sq = jnp.zeros((L, 1), F32)
        for j in range(GROUP_WIDTH // SSD_STRIP):
            ls = slice(j * SSD_STRIP, (j + 1) * SSD_STRIP)
            cs = slice(g * GROUP_WIDTH + j * SSD_STRIP, g * GROUP_WIDTH + (j + 1) * SSD_STRIP)
            xs_j = xs_ref[:, cs]
            st_j = state_ref[g, :, ls]
            y_j = y_ref[:, cs] + _dot(cg_b, st_j.astype(BF16)) * _dot(e1_b, r_ref[:, cs]) + xs_j * dx_ref[:, cs]
            y_j = y_j * _silu(z_ref[0, :, cs].astype(F32))
            sq = sq + jnp.sum(y_j * y_j, axis=-1, keepdims=True)
            y_ref[:, cs] = y_j
            upd = _dot(bgt_b, (xs_j * _dot(e2_b, r_ref[:, cs])).astype(BF16))
            state_ref[g, :, ls] = st_j * decay_x[:, cs] + upd
        scale = lax.rsqrt(sq * (1.0 / GROUP_WIDTH) + EPS)
        for j in range(GROUP_WIDTH // SSD_STRIP):
            cs = slice(g * GROUP_WIDTH + j * SSD_STRIP, g * GROUP_WIDTH + (j + 1) * SSD_STRIP)
            o_ref[0, :, cs] = (y_ref[:, cs] * scale * nw_ref[:, cs]).astype(o_ref.dtype)


def _ssd(xbc, z, dt, conv_w, conv_b, dt_bias, a_log, d_skip, ssm_norm_w):
    bsz, s, _ = xbc.shape
    pad = DT_PAD - SSM_HEADS
    dtb = jnp.pad(dt_bias.astype(F32), (0, pad)).reshape(1, DT_PAD)
    a_neg = jnp.pad(-jnp.exp(a_log.astype(F32)), (0, pad)).reshape(1, DT_PAD)
    dx = jnp.repeat(d_skip.astype(F32), SSM_HEADDIM).reshape(1, SSM_D_INNER)
    expand = (jnp.arange(DT_PAD)[:, None] == (jnp.arange(SSM_D_INNER)[None, :] // SSM_HEADDIM)).astype(BF16)
    tril = (jnp.arange(CHUNK)[:, None] >= jnp.arange(CHUNK)[None, :]).astype(BF16)
    shift = jnp.stack([(jnp.arange(CHUNK)[:, None] - d == jnp.arange(CHUNK)[None, :]).astype(BF16)
                       for d in range(1, SSM_CONV)])

    def seq(width):
        return pl.BlockSpec((1, CHUNK, width), lambda b, c: (b, c, 0))

    def const(shape):
        return pl.BlockSpec(shape, lambda b, c: (0,) * len(shape))

    return pl.pallas_call(
        _ssd_kernel,
        grid=(bsz, s // CHUNK),
        in_specs=[seq(SSM_CONV_DIM), seq(SSM_D_INNER), seq(DT_PAD),
                  const((SSM_CONV, SSM_CONV_DIM)), const((1, SSM_CONV_DIM)), const((1, DT_PAD)), const((1, DT_PAD)),
                  const((1, SSM_D_INNER)), const((1, SSM_D_INNER)), const((DT_PAD, SSM_D_INNER)),
                  const((CHUNK, CHUNK)), const((SSM_CONV - 1, CHUNK, CHUNK))],
        out_specs=seq(SSM_D_INNER),
        out_shape=jax.ShapeDtypeStruct((bsz, s, SSM_D_INNER), BF16),
        scratch_shapes=[pltpu.VMEM((SUBLANES, SSM_CONV_DIM), F32),
                        pltpu.VMEM((SSM_GROUPS, SSM_STATE, GROUP_WIDTH), F32),
                        pltpu.VMEM((CHUNK, SSM_D_INNER), F32),
                        pltpu.VMEM((CHUNK, 2 * SSM_GROUPS * SSM_STATE), BF16),
                        pltpu.VMEM((CHUNK, SSM_D_INNER), F32)],
        compiler_params=_params(("parallel", "arbitrary")),
        name="ssd",
    )(xbc, z, dt, conv_w.astype(F32), conv_b.astype(F32).reshape(1, SSM_CONV_DIM), dtb, a_neg, dx,
      ssm_norm_w.astype(F32).reshape(1, SSM_D_INNER), expand, tril, shift)


def _ret_consts():
    lg = np.log1p(-(2.0 ** (-5.0 - np.arange(RET_HEADS, dtype=np.float64))))
    idx = np.arange(CHUNK, dtype=np.float64)
    rel = idx[:, None] - idx[None, :]
    intra = np.where(rel >= 0, np.exp(np.maximum(rel, 0.0)[None] * lg[:, None, None]), 0.0)
    qd = np.exp((idx + 1.0)[None, :] * lg[:, None])
    kd = np.exp((CHUNK - 1.0 - idx)[None, :] * lg[:, None]) * (RET_QK_DIM ** -0.5)
    qd = np.broadcast_to(qd[:, :, None], (RET_HEADS, CHUNK, RET_QK_DIM))
    kd = np.broadcast_to(kd[:, :, None], (RET_HEADS, CHUNK, RET_QK_DIM))
    chunk_decay = tuple(float(v) for v in np.exp(CHUNK * lg))
    return (jnp.asarray(intra, F32), jnp.asarray(qd, F32), jnp.asarray(kd, F32), chunk_decay)


def _ret_kernel(q_ref, k_ref, v_ref, g_ref, cos_ref, sin_ref, intra_ref, qd_ref, kd_ref, o_ref, state_ref, *,
                chunk_decay):
    @pl.when(pl.program_id(1) == 0)
    def _():
        state_ref[...] = jnp.zeros_like(state_ref)

    cos = cos_ref[0]
    sin = sin_ref[0]

    def rot(t):
        t1, t2 = t[:, :ROPE_HALF], t[:, ROPE_HALF:]
        return jnp.concatenate([t1 * cos - t2 * sin, t1 * sin + t2 * cos], axis=1)

    for h in range(RET_HEADS):
        qs = slice(h * RET_QK_DIM, (h + 1) * RET_QK_DIM)
        vs = slice(h * RET_V_DIM, (h + 1) * RET_V_DIM)
        qr = rot(q_ref[0, :, qs].astype(F32))
        kr = rot(k_ref[0, :, qs].astype(F32))
        vh = v_ref[0, :, vs]
        s = _dot_nt(qr.astype(BF16), (kr * (RET_QK_DIM ** -0.5)).astype(BF16)) * intra_ref[h]
        st = state_ref[h]
        y = _dot(s.astype(BF16), vh) + _dot((qr * qd_ref[h]).astype(BF16), st.astype(BF16))
        state_ref[h] = st * chunk_decay[h] + _dot((kr * kd_ref[h]).T.astype(BF16), vh)
        mu = jnp.mean(y, axis=-1, keepdims=True)
        yc = y - mu
        var = jnp.mean(yc * yc, axis=-1, keepdims=True)
        o_ref[0, :, vs] = (_silu(g_ref[0, :, vs].astype(F32)) * (yc * lax.rsqrt(var + EPS))).astype(o_ref.dtype)


def _ret(q, k, v, g, cos, sin):
    bsz, s, _ = q.shape
    intra, qd, kd, chunk_decay = _ret_consts()

    def seq(width):
        return pl.BlockSpec((1, CHUNK, width), lambda b, c: (b, c, 0))

    def const(shape):
        return pl.BlockSpec(shape, lambda b, c: (0,) * len(shape))

    return pl.pallas_call(
        functools.partial(_ret_kernel, chunk_decay=chunk_decay),
        grid=(bsz, s // CHUNK),
        in_specs=[seq(RET_QK_WIDTH), seq(RET_QK_WIDTH), seq(RET_V_WIDTH), seq(RET_V_WIDTH),
                  seq(ROPE_HALF), seq(ROPE_HALF),
                  const(intra.shape), const(qd.shape), const(kd.shape)],
        out_specs=seq(RET_V_WIDTH),
        out_shape=jax.ShapeDtypeStruct((bsz, s, RET_V_WIDTH), BF16),
        scratch_shapes=[pltpu.VMEM((RET_HEADS, RET_QK_DIM, RET_V_DIM), F32)],
        compiler_params=_params(("parallel", "arbitrary")),
        name="ret",
    )(q, k, v, g, cos, sin, intra, qd, kd)


def _merge_kernel(ys_ref, yr_ref, gs_ref, gr_ref, x_ref, g1_ref, sc2_ref, sh2_ref, n2w_ref,
                  wso_ref, wro_ref, wo_ref, wrh_ref, wrl_ref, x1_ref, h2_ref, sct_ref):
    y_ssm = _dot(ys_ref[0], wso_ref[...])
    y_ret = _dot(yr_ref[0], wro_ref[...])
    merged = _sigmoid(gs_ref[0].astype(F32)) * y_ssm + _sigmoid(gr_ref[0].astype(F32)) * y_ret
    x1 = x_ref[0] + g1_ref[0] * _dot(merged.astype(BF16), wo_ref[...])
    x1_ref[0] = x1
    h2 = _rms_mod(x1, n2w_ref[...], sc2_ref[0], sh2_ref[0])
    h_hi = h2.astype(BF16)
    h2_ref[0] = _pack_rows(h2)
    h_lo = (h2 - h_hi.astype(F32)).astype(BF16)
    wrh = wrh_ref[...]
    logits_t = _dot_nt(wrh, h_hi) + _dot_nt(wrh, h_lo) + _dot_nt(wrl_ref[...], h_hi)
    sct_ref[...] = _sigmoid(logits_t)


def _merge(ys, yr, gs, gr, x, gate1, scale2, shift2, norm2_w, w_ssm_out, w_ret_out, w_out, w_router, tm):
    bsz, s, d = x.shape
    nt = s // tm
    wrt = w_router.astype(F32).T
    wrh = wrt.astype(BF16)
    wrl = (wrt - wrh.astype(F32)).astype(BF16)

    def seq(width):
        return pl.BlockSpec((1, tm, width), lambda b, i: (b, i, 0))

    def const(shape):
        return pl.BlockSpec(shape, lambda b, i: (0,) * len(shape))

    vec = pl.BlockSpec((1, 1, d), lambda b, i: (b, 0, 0))
    return pl.pallas_call(
        _merge_kernel,
        grid=(bsz, nt),
        in_specs=[seq(SSM_D_INNER), seq(RET_V_WIDTH), seq(d), seq(d), seq(d), vec, vec, vec, const((1, d)),
                  const((SSM_D_INNER, d)), const((RET_V_WIDTH, d)), const((d, d)),
                  const((N_EXPERTS, d)), const((N_EXPERTS, d))],
        out_specs=[seq(d), seq(PACKED), pl.BlockSpec((N_EXPERTS, tm), lambda b, i: (0, b * nt + i))],
        out_shape=[jax.ShapeDtypeStruct((bsz, s, d), F32), jax.ShapeDtypeStruct((bsz, s, PACKED), jnp.int32),
                   jax.ShapeDtypeStruct((N_EXPERTS, bsz * s), F32)],
        compiler_params=_params(("parallel", "parallel")),
        name="merge",
    )(ys, yr, gs, gr, x, gate1, scale2, shift2, norm2_w.astype(F32).reshape(1, d),
      w_ssm_out.astype(BF16), w_ret_out.astype(BF16), w_out.astype(BF16), wrh, wrl)


def _route_kernel(sc_ref, bias_ref, upper_ref, ones_ref, idx_ref, wts_ref, rank_ref, cnt_ref, carry_ref):
    @pl.when(pl.program_id(0) == 0)
    def _():
        carry_ref[...] = jnp.zeros_like(carry_ref)

    tl = sc_ref.shape[1]
    neg = -jnp.inf
    scores = sc_ref[...]
    sel = scores + bias_ref[...]

    def first_argmax(vals, iota, n):
        m = jnp.max(vals, axis=0, keepdims=True)
        return jnp.min(jnp.where(vals == m, iota, float(n)), axis=0, keepdims=True), m

    io_g = lax.broadcasted_iota(jnp.int32, (ROUTE_GROUP_SIZE, tl), 0).astype(F32)
    rows = []
    for g in range(N_ROUTE_GROUPS):
        blk = sel[g * ROUTE_GROUP_SIZE:(g + 1) * ROUTE_GROUP_SIZE]
        i1, m1 = first_argmax(blk, io_g, ROUTE_GROUP_SIZE)
        m2 = jnp.max(jnp.where(io_g == i1, neg, blk), axis=0, keepdims=True)
        rows.append(m1 + m2)
    gsc = jnp.concatenate(rows, axis=0)
    io_8 = lax.broadcasted_iota(jnp.int32, (N_ROUTE_GROUPS, tl), 0).astype(F32)
    chosen = jnp.zeros((N_ROUTE_GROUPS, tl), F32)
    for _ in range(TOPK_ROUTE_GROUPS):
        i, _m = first_argmax(gsc, io_8, N_ROUTE_GROUPS)
        hit = io_8 == i
        chosen = jnp.where(hit, 1.0, chosen)
        gsc = jnp.where(hit, neg, gsc)
    msel = jnp.concatenate(
        [jnp.where(chosen[g:g + 1] > 0.5, sel[g * ROUTE_GROUP_SIZE:(g + 1) * ROUTE_GROUP_SIZE], neg)
         for g in range(N_ROUTE_GROUPS)], axis=0)

    io_e = lax.broadcasted_iota(jnp.int32, (N_EXPERTS, tl), 0).astype(F32)
    multi = jnp.zeros((N_EXPERTS, tl), F32)
    idx_rows, sc_rows = [], []
    for _ in range(TOP_K):
        i, _m = first_argmax(msel, io_e, N_EXPERTS)
        hit = io_e == i
        sc_rows.append(jnp.sum(jnp.where(hit, scores, 0.0), axis=0, keepdims=True))
        msel = jnp.where(hit, neg, msel)
        multi = jnp.where(hit, 1.0, multi)
        idx_rows.append(i)
    den = sc_rows[0]
    for r in sc_rows[1:]:
        den = den + r
    den = den + 1e-20
    wts_ref[...] = jnp.concatenate([r / den * ROUTED_SCALE for r in sc_rows], axis=0)
    idx_ref[...] = jnp.concatenate(idx_rows, axis=0).astype(jnp.int32)

    multi_b = multi.astype(BF16)
    carry = carry_ref[...]
    rank_full = _dot(multi_b, upper_ref[...]) + carry
    rank_ref[...] = jnp.concatenate(
        [jnp.sum(jnp.where(io_e == i, rank_full, 0.0), axis=0, keepdims=True) for i in idx_rows],
        axis=0).astype(jnp.int32)
    carry = carry + _dot(multi_b, ones_ref[...])
    carry_ref[...] = carry
    cnt_ref[...] = carry[:, :LANES]


def _route(scores_t, router_bias):
    e, t = scores_t.shape
    tl = min(ROUTE_TILE, t)
    bias = jnp.broadcast_to(router_bias.astype(F32).reshape(e, 1), (e, tl))
    upper = (jnp.arange(tl)[:, None] < jnp.arange(tl)[None, :]).astype(BF16)
    ones = jnp.ones((tl, tl), BF16)
    tok = pl.BlockSpec((TOP_K, tl), lambda i: (0, i))

    def const(shape):
        return pl.BlockSpec(shape, lambda i: (0,) * len(shape))

    return pl.pallas_call(
        _route_kernel,
        grid=(t // tl,),
        in_specs=[pl.BlockSpec((e, tl), lambda i: (0, i)), const((e, tl)), const((tl, tl)), const((tl, tl))],
        out_specs=[tok, tok, tok, const((e, LANES))],
        out_shape=[jax.ShapeDtypeStruct((TOP_K, t), jnp.int32), jax.ShapeDtypeStruct((TOP_K, t), F32),
                   jax.ShapeDtypeStruct((TOP_K, t), jnp.int32), jax.ShapeDtypeStruct((e, LANES), F32)],
        scratch_shapes=[pltpu.VMEM((e, tl), F32)],
        compiler_params=_params(("arbitrary",)),
        name="route",
    )(scores_t, bias, upper, ones)


def _dest_kernel(idx_ref, rank_ref, start_ref, o_ref):
    tl = idx_ref.shape[1]
    io_e = lax.broadcasted_iota(jnp.int32, (N_EXPERTS, tl), 0).astype(F32)
    idx = idx_ref[...].astype(F32)
    start = start_ref[...]
    rows = [jnp.sum(jnp.where(io_e == idx[kk:kk + 1], start, 0.0), axis=0, keepdims=True) for kk in range(TOP_K)]
    o_ref[...] = jnp.concatenate(rows, axis=0).astype(jnp.int32) + rank_ref[...]


def _dest(idx_t, rank_t, padded_start):
    k, t = idx_t.shape
    tl = min(2 * ROUTE_TILE, t)
    start = jnp.broadcast_to(padded_start.astype(F32).reshape(N_EXPERTS, 1), (N_EXPERTS, tl))
    tok = pl.BlockSpec((k, tl), lambda i: (0, i))
    return pl.pallas_call(
        _dest_kernel,
        grid=(t // tl,),
        in_specs=[tok, tok, pl.BlockSpec((N_EXPERTS, tl), lambda i: (0, 0))],
        out_specs=tok,
        out_shape=jax.ShapeDtypeStruct((k, t), jnp.int32),
        compiler_params=_params(("parallel",)),
        name="dest",
    )(idx_t, rank_t, start)


def _expert_kernel(be_ref, nv_ref, slot_ref, nxt_ref, x_ref, wg_hbm, wu_hbm, wd_hbm, o_ref,
                   wg_buf, wu_buf, wd_buf, wgb_ref, wub_ref, wdb_ref, sem):
    b = pl.program_id(0)
    nv = nv_ref[b]

    def weight_copies(expert, slot):
        return (pltpu.make_async_copy(wg_hbm.at[expert], wg_buf.at[slot], sem.at[slot, 0]),
                pltpu.make_async_copy(wu_hbm.at[expert], wu_buf.at[slot], sem.at[slot, 1]),
                pltpu.make_async_copy(wd_hbm.at[expert], wd_buf.at[slot], sem.at[slot, 2]))

    @pl.when(b == 0)
    def _():
        for cp in weight_copies(be_ref[0], slot_ref[0]):
            cp.start()

    @pl.when(jnp.logical_or(b == 0, be_ref[b] != be_ref[jnp.maximum(b - 1, 0)]))
    def _():
        slot = slot_ref[b]
        for cp in weight_copies(be_ref[b], slot):
            cp.wait()
        nxt = nxt_ref[b]

        @pl.when(nxt >= 0)
        def _():
            for cp in weight_copies(nxt, 1 - slot):
                cp.start()

        wgb_ref[...] = wg_buf[slot].astype(BF16)
        wub_ref[...] = wu_buf[slot].astype(BF16)
        wdb_ref[...] = wd_buf[slot].astype(BF16)

    @pl.when(nv > 0)
    def _():
        row = lax.broadcasted_iota(jnp.int32, x_ref.shape, 0)
        x = _unpack_rows(jnp.where(row < nv, x_ref[...], 0))
        g = _dot(x, wgb_ref[...])
        u = _dot(x, wub_ref[...])
        o_ref[...] = _pack_rows(_dot((_silu(g) * u).astype(BF16), wdb_ref[...]))

    @pl.when(nv == 0)
    def _():
        o_ref[...] = jnp.zeros_like(o_ref)


def _experts(block_expert, block_valid, xs, w_gate, w_up, w_down):
    n_slots = xs.shape[0]
    n_blocks = n_slots // EXPERT_BLOCK
    d, f = w_gate.shape[-2:]
    ids = jnp.arange(n_blocks, dtype=jnp.int32)
    change = jnp.concatenate([jnp.ones((1,), bool), block_expert[1:] != block_expert[:-1]])
    run_slot = ((jnp.cumsum(change.astype(jnp.int32)) - 1) % 2).astype(jnp.int32)
    change_at = jnp.where(change, ids, n_blocks)
    next_change = jnp.concatenate([lax.cummin(change_at[::-1])[::-1][1:], jnp.full((1,), n_blocks, jnp.int32)])
    next_expert = jnp.where(next_change < n_blocks, block_expert[jnp.minimum(next_change, n_blocks - 1)], -1)
    hbm = pl.BlockSpec(memory_space=pl.ANY)
    grid_spec = pltpu.PrefetchScalarGridSpec(
        num_scalar_prefetch=4,
        grid=(n_blocks,),
        in_specs=[pl.BlockSpec((EXPERT_BLOCK, PACKED), lambda b, be, nv, sl, nx: (jnp.where(nv[b] > 0, b, 0), 0)),
                  hbm, hbm, hbm],
        out_specs=pl.BlockSpec((EXPERT_BLOCK, PACKED), lambda b, be, nv, sl, nx: (b, 0)),
        scratch_shapes=[pltpu.VMEM((2, d, f), w_gate.dtype), pltpu.VMEM((2, d, f), w_up.dtype),
                        pltpu.VMEM((2, f, d), w_down.dtype),
                        pltpu.VMEM((d, f), BF16), pltpu.VMEM((d, f), BF16), pltpu.VMEM((f, d), BF16),
                        pltpu.SemaphoreType.DMA((2, 3))],
    )
    return pl.pallas_call(
        _expert_kernel,
        grid_spec=grid_spec,
        out_shape=jax.ShapeDtypeStruct((n_slots, PACKED), jnp.int32),
        compiler_params=_params(("arbitrary",)),
        name="experts",
    )(block_expert, block_valid, run_slot, next_expert.astype(jnp.int32), xs, w_gate, w_up, w_down)


def _final_kernel(x1_ref, h2_ref, yk_ref, wts_ref, g2_ref, wsg_ref, wsu_ref, wsd_ref, fnw_ref, o_ref):
    h = _unpack_rows(h2_ref[0])
    act = (_silu(_dot(h, wsg_ref[...])) * _dot(h, wsu_ref[...])).astype(BF16)
    moe = _dot(act, wsd_ref[...])
    wts = wts_ref[...]
    for kk in range(TOP_K):
        moe = moe + _unpack_rows(yk_ref[kk]).astype(F32) * wts[:, kk:kk + 1]
    x2 = x1_ref[0] + g2_ref[0] * moe
    ms = jnp.mean(x2 * x2, axis=-1, keepdims=True)
    o_ref[0] = x2 * lax.rsqrt(ms + EPS) * fnw_ref[...]


def _final(x1, h2p, yk, wts, gate2, w_sh_gate, w_sh_up, w_sh_down, final_norm_w, tm):
    bsz, s, d = x1.shape
    nt = s // tm
    f = w_sh_gate.shape[-1]
    seq = pl.BlockSpec((1, tm, d), lambda b, i: (b, i, 0))

    def const(shape):
        return pl.BlockSpec(shape, lambda b, i: (0,) * len(shape))

    return pl.pallas_call(
        _final_kernel,
        grid=(bsz, nt),
        in_specs=[seq, pl.BlockSpec((1, tm, PACKED), lambda b, i: (b, i, 0)),
                  pl.BlockSpec((TOP_K, tm, PACKED), lambda b, i: (0, b * nt + i, 0)),
                  pl.BlockSpec((tm, TOP_K), lambda b, i: (b * nt + i, 0)),
                  pl.BlockSpec((1, 1, d), lambda b, i: (b, 0, 0)),
                  const((d, f)), const((d, f)), const((f, d)), const((1, d))],
        out_specs=seq,
        out_shape=jax.ShapeDtypeStruct((bsz, s, d), F32),
        compiler_params=_params(("parallel", "parallel")),
        name="final",
    )(x1, h2p, yk, wts, gate2, w_sh_gate.astype(BF16), w_sh_up.astype(BF16), w_sh_down.astype(BF16),
      final_norm_w.astype(F32).reshape(1, d))


def _gather_rows(table, idx):
    n = idx.shape[0]
    width = table.shape[1]
    workers = SC_CORES * SC_SUBCORES
    per_worker = n // workers
    steps = per_worker // SC_WINDOW
    assert per_worker * workers == n and steps * SC_WINDOW == per_worker and steps % 2 == 0
    mesh = plsc.VectorSubcoreMesh(core_axis_name="c", subcore_axis_name="s")
    win = lambda dtype, *shape: pltpu.VMEM(shape, dtype)

    @functools.partial(
        pl.kernel, mesh=mesh,
        out_type=jax.ShapeDtypeStruct((n, width), table.dtype),
        scratch_types=[win(jnp.int32, SC_WINDOW), win(jnp.int32, SC_WINDOW),
                       win(table.dtype, SC_WINDOW, width), win(table.dtype, SC_WINDOW, width),
                       pltpu.SemaphoreType.DMA, pltpu.SemaphoreType.DMA,
                       pltpu.SemaphoreType.DMA, pltpu.SemaphoreType.DMA],
        name="gather_rows",
    )
    def run(table_hbm, idx_hbm, out_hbm, idx_a, idx_b, rows_a, rows_b, gsem_a, gsem_b, osem_a, osem_b):
        base = (lax.axis_index("s") * SC_CORES + lax.axis_index("c")) * per_worker

        @pl.loop(0, steps, step=2)
        def _(i):
            off_a = pl.multiple_of(base + i * SC_WINDOW, SC_WINDOW)
            off_b = pl.multiple_of(off_a + SC_WINDOW, SC_WINDOW)
            pltpu.sync_copy(idx_hbm.at[pl.ds(off_a, SC_WINDOW)], idx_a)
            gather_a = pltpu.async_copy(table_hbm.at[idx_a], rows_a, gsem_a)
            pltpu.sync_copy(idx_hbm.at[pl.ds(off_b, SC_WINDOW)], idx_b)
            gather_b = pltpu.async_copy(table_hbm.at[idx_b], rows_b, gsem_b)
            gather_a.wait()
            store_a = pltpu.async_copy(rows_a, out_hbm.at[pl.ds(off_a, SC_WINDOW)], osem_a)
            gather_b.wait()
            store_b = pltpu.async_copy(rows_b, out_hbm.at[pl.ds(off_b, SC_WINDOW)], osem_b)
            store_a.wait()
            store_b.wait()

    return run(table, idx)


def _scatter_rows(rows, dest_kt, n_slots):
    t, width = rows.shape
    top_k = dest_kt.shape[0]
    workers = SC_CORES * SC_SUBCORES
    window = LANES
    per_worker = t // workers
    steps = per_worker // window
    assert per_worker * workers == t and steps * window == per_worker
    mesh = plsc.VectorSubcoreMesh(core_axis_name="c", subcore_axis_name="s")

    @functools.partial(
        pl.kernel, mesh=mesh,
        out_type=jax.ShapeDtypeStruct((n_slots, width), rows.dtype),
        scratch_types=[pltpu.VMEM((top_k, window), jnp.int32),
                       pltpu.VMEM((window, width), rows.dtype),
                       pltpu.SemaphoreType.DMA],
        name="scatter_rows",
    )
    def run(rows_hbm, dest_hbm, out_hbm, idx_v, rows_v, sem):
        base = (lax.axis_index("s") * SC_CORES + lax.axis_index("c")) * per_worker

        @pl.loop(0, steps)
        def _(i):
            off = pl.multiple_of(base + i * window, window)
            pltpu.sync_copy(rows_hbm.at[pl.ds(off, window)], rows_v)
            pltpu.sync_copy(dest_hbm.at[:, pl.ds(off, window)], idx_v)
            copies = [pltpu.async_copy(rows_v, out_hbm.at[idx_v.at[kk]], sem) for kk in range(top_k)]
            for cp in copies:
                cp.wait()

    return run(rows, dest_kt)


def _layer(x, c, positions, w_ada, b_ada, norm1_w, w_in, conv_w, conv_b, dt_bias, a_log, d_skip, ssm_norm_w,
           w_ssm_out, w_ret_out, w_out, norm2_w, w_router, router_bias, w_exp_gate, w_exp_up, w_exp_down,
           w_sh_gate, w_sh_up, w_sh_down):
    bsz, s, d = x.shape
    t = bsz * s
    tm = min(TOKEN_TILE, s)

    mod = _mod(c, w_ada, b_ada)
    shift1, scale1, gate1, shift2, scale2, gate2 = (mod[:, i * d:(i + 1) * d].reshape(bsz, 1, d) for i in range(6))

    bounds = np.cumsum((0, SSM_D_INNER, SSM_CONV_DIM, SSM_HEADS, RET_QK_WIDTH, RET_QK_WIDTH, RET_V_WIDTH,
                        RET_V_WIDTH, d, d))
    wz, wxbc, wdt, wq, wk, wv, wg, wgs, wgr = (w_in[:, bounds[i]:bounds[i + 1]] for i in range(9))
    nw1 = norm1_w.astype(F32).reshape(1, d)
    w_a = jnp.concatenate([wxbc, wz, jnp.pad(wdt, ((0, 0), (0, DT_PAD - SSM_HEADS)))], axis=1).astype(BF16)
    w_b = jnp.concatenate([wq, wk, wv, wg, wgs, wgr], axis=1).astype(BF16)
    xbc, z, dt = _inproj(x, nw1, scale1, shift1, w_a, (SSM_CONV_DIM, SSM_D_INNER, DT_PAD), (BF16, BF16, F32), tm)
    q, k, v, g, gs, gr = _inproj(x, nw1, scale1, shift1, w_b,
                                 (RET_QK_WIDTH, RET_QK_WIDTH, RET_V_WIDTH, RET_V_WIDTH, d, d), (BF16,) * 6, tm)

    ys = _ssd(xbc, z, dt, conv_w, conv_b, dt_bias, a_log, d_skip, ssm_norm_w)
    cos, sin = _rope(positions, tm)
    yr = _ret(q, k, v, g, cos, sin)
    x1, h2, scores_t = _merge(ys, yr, gs, gr, x, gate1, scale2, shift2, norm2_w, w_ssm_out, w_ret_out, w_out,
                              w_router, tm)

    idx_t, wts_t, rank_t, cnt = _route(scores_t, router_bias)

    counts = cnt[:, 0].astype(jnp.int32)
    padded = (counts + EXPERT_BLOCK - 1) // EXPERT_BLOCK * EXPERT_BLOCK
    padded_end = jnp.cumsum(padded)
    padded_start = padded_end - padded
    n_blocks = -(-(t * TOP_K) // EXPERT_BLOCK) + N_EXPERTS
    n_slots = n_blocks * EXPERT_BLOCK
    dest_t = _dest(idx_t, rank_t, padded_start)
    block_first = jnp.arange(n_blocks, dtype=jnp.int32) * EXPERT_BLOCK
    block_expert = jnp.minimum(jnp.sum(padded_end[None, :] <= block_first[:, None], axis=1), N_EXPERTS - 1)
    owner = (block_expert[:, None] == jnp.arange(N_EXPERTS, dtype=jnp.int32)[None, :]).astype(jnp.int32)
    region_end = jnp.sum(owner * (padded_start + counts)[None, :], axis=1)
    block_valid = jnp.clip(region_end - block_first, 0, EXPERT_BLOCK).astype(jnp.int32)
    xs = _scatter_rows(h2.reshape(t, PACKED), dest_t, n_slots)
    y_sorted = _experts(block_expert.astype(jnp.int32), block_valid, xs, w_exp_gate, w_exp_up, w_exp_down)
    yk = _gather_rows(y_sorted, dest_t.reshape(-1)).reshape(TOP_K, t, PACKED)
    return x1, h2, yk, wts_t.T, gate2


def kernel(x, c, positions, w_ada, b_ada, norm1_w, w_in, conv_w, conv_b, dt_bias, a_log, d_skip, ssm_norm_w,
           w_ssm_out, w_ret_out, w_out, norm2_w, w_router, router_bias, w_exp_gate, w_exp_up, w_exp_down,
           w_sh_gate, w_sh_up, w_sh_down, final_norm_w):
    assert w_ada.shape[0] == 1, "the final rmsnorm is fused into the single layer's last kernel"
    tm = min(TOKEN_TILE, x.shape[1])
    x1, h2, yk, wts, gate2 = _layer(
        x, c, positions, w_ada[0], b_ada[0], norm1_w[0], w_in[0], conv_w[0], conv_b[0], dt_bias[0], a_log[0],
        d_skip[0], ssm_norm_w[0], w_ssm_out[0], w_ret_out[0], w_out[0], norm2_w[0], w_router[0],
        router_bias[0], w_exp_gate[0], w_exp_up[0], w_exp_down[0], w_sh_gate[0], w_sh_up[0], w_sh_down[0])
    return _final(x1, h2, yk, wts, gate2, w_sh_gate[0], w_sh_up[0], w_sh_down[0], final_norm_w, tm)
```

```python
import functools

import numpy as np
import jax
import jax.numpy as jnp
from jax import lax
from jax.experimental import pallas as pl
from jax.experimental.pallas import tpu as pltpu
from jax.experimental.pallas import tpu_sc as plsc

F32 = jnp.float32
BF16 = jnp.bfloat16

EPS = 1e-6
D_MODEL = 1024
SSM_D_INNER = 2048
SSM_HEADDIM = 64
SSM_HEADS = 32
SSM_GROUPS = 4
SSM_STATE = 128
SSM_CONV = 4
SSM_CONV_DIM = SSM_D_INNER + 2 * SSM_GROUPS * SSM_STATE
GROUP_WIDTH = SSM_D_INNER // SSM_GROUPS
RET_HEADS = 4
RET_QK_DIM = 256
RET_V_DIM = 512
RET_QK_WIDTH = RET_HEADS * RET_QK_DIM
RET_V_WIDTH = RET_HEADS * RET_V_DIM
ROPE_BASE = 10000.0
ROPE_HALF = RET_QK_DIM // 2
N_EXPERTS = 256
TOP_K = 8
N_ROUTE_GROUPS = 8
TOPK_ROUTE_GROUPS = 4
ROUTE_GROUP_SIZE = N_EXPERTS // N_ROUTE_GROUPS
EXPERT_DIM = 256
ROUTED_SCALE = 2.5

TOKEN_TILE = 512
CHUNK = 128
SSD_STRIP = 256
LANES = 128
SUBLANES = 8
DT_PAD = LANES
EXPERT_BLOCK = 512
ROUTE_TILE = 256
VMEM_LIMIT = 56 * 1024 * 1024
PACKED = D_MODEL // 2
SC_CORES = 2
SC_SUBCORES = 16
SC_WINDOW = 64
COMBINE_PARTS = 4


LOG2E = 1.4426950408889634


def _silu(x):
    return x / (1.0 + jnp.exp2(x * -LOG2E))


def _sigmoid(x):
    return 1.0 / (1.0 + jnp.exp2(x * -LOG2E))


def _split3(a):
    hi = a.astype(BF16)
    r1 = a - hi.astype(F32)
    mid = r1.astype(BF16)
    lo = (r1 - mid.astype(F32)).astype(BF16)
    return hi, mid, lo


def _pack_rows(x):
    w = x.shape[1] // 2
    hi = lax.bitcast_convert_type(x[:, :w].astype(BF16).astype(F32), jnp.uint32)
    lo = lax.bitcast_convert_type(x[:, w:].astype(BF16).astype(F32), jnp.uint32)
    return lax.bitcast_convert_type(hi | (lo >> 16), jnp.int32)


def _unpack_rows(p):
    u = lax.bitcast_convert_type(p, jnp.uint32)
    hi = lax.bitcast_convert_type(u & jnp.uint32(0xFFFF0000), F32)
    lo = lax.bitcast_convert_type(u << 16, F32)
    return jnp.concatenate([hi, lo], axis=1).astype(BF16)


def _dot(a, b):
    return jnp.dot(a, b, preferred_element_type=F32)


def _dot_nt(a, b):
    return lax.dot_general(a, b, (((1,), (1,)), ((), ())), preferred_element_type=F32)


def _params(sem):
    return pltpu.CompilerParams(dimension_semantics=sem, vmem_limit_bytes=VMEM_LIMIT)


def _mod_kernel(c_ref, w_ref, b_ref, o_ref):
    o_ref[...] = _dot(_silu(c_ref[...]), w_ref[...]) + b_ref[...]


def _mod(c, w_ada, b_ada):
    bsz, d = c.shape
    n = w_ada.shape[1]
    return pl.pallas_call(
        _mod_kernel,
        grid=(n // d,),
        in_specs=[pl.BlockSpec((bsz, d), lambda j: (0, 0)),
                  pl.BlockSpec((d, d), lambda j: (0, j)),
                  pl.BlockSpec((1, d), lambda j: (0, j))],
        out_specs=pl.BlockSpec((bsz, d), lambda j: (0, j)),
        out_shape=jax.ShapeDtypeStruct((bsz, n), F32),
        compiler_params=_params(("arbitrary",)),
        name="mod",
    )(c, w_ada, b_ada.reshape(1, n))


def _rms_mod(x, nw, scale, shift):
    ms = jnp.mean(x * x, axis=-1, keepdims=True)
    return (x * lax.rsqrt(ms + EPS) * nw) * (1.0 + scale) + shift


def _inproj_kernel(x_ref, nw_ref, sc_ref, sh_ref, w_ref, *o_refs, widths):
    hb = _rms_mod(x_ref[0], nw_ref[...], sc_ref[0], sh_ref[0]).astype(BF16)
    off = 0
    for o_ref, wd in zip(o_refs, widths):
        for c0 in range(0, wd, 512):
            cw = min(512, wd - c0)
            o_ref[0, :, c0:c0 + cw] = _dot(hb, w_ref[:, off + c0:off + c0 + cw]).astype(o_ref.dtype)
        off += wd


def _inproj(x, nw, scale, shift, w, widths, dtypes, tm):
    bsz, s, d = x.shape
    vec = pl.BlockSpec((1, 1, d), lambda b, i: (b, 0, 0))
    return pl.pallas_call(
        functools.partial(_inproj_kernel, widths=widths),
        grid=(bsz, s // tm),
        in_specs=[pl.BlockSpec((1, tm, d), lambda b, i: (b, i, 0)),
                  pl.BlockSpec((1, d), lambda b, i: (0, 0)),
                  vec, vec,
                  pl.BlockSpec(w.shape, lambda b, i: (0, 0))],
        out_specs=[pl.BlockSpec((1, tm, wd), lambda b, i: (b, i, 0)) for wd in widths],
        out_shape=[jax.ShapeDtypeStruct((bsz, s, wd), dt) for wd, dt in zip(widths, dtypes)],
        compiler_params=_params(("parallel", "parallel")),
        name="inproj",
    )(x, nw, scale, shift, w)


def _rope_kernel(pos_ref, inv_ref, cos_ref, sin_ref):
    ang = pos_ref[0].astype(F32) * inv_ref[...]
    cos_ref[0] = jnp.cos(ang)
    sin_ref[0] = jnp.sin(ang)


def _rope(positions, tm):
    bsz, s = positions.shape
    inv = (1.0 / (ROPE_BASE ** (jnp.arange(ROPE_HALF, dtype=F32) / ROPE_HALF))).reshape(1, ROPE_HALF)
    spec = pl.BlockSpec((1, tm, ROPE_HALF), lambda b, i: (b, i, 0))
    return pl.pallas_call(
        _rope_kernel,
        grid=(bsz, s // tm),
        in_specs=[pl.BlockSpec((1, tm, 1), lambda b, i: (b, i, 0)),
                  pl.BlockSpec((1, ROPE_HALF), lambda b, i: (0, 0))],
        out_specs=[spec, spec],
        out_shape=[jax.ShapeDtypeStruct((bsz, s, ROPE_HALF), F32)] * 2,
        compiler_params=_params(("parallel", "parallel")),
        name="rope",
    )(positions.reshape(bsz, s, 1), inv)


def _ssd_kernel(xbc_ref, z_ref, dt_ref, cw_ref, cb_ref, dtb_ref, a_ref, dx_ref, nw_ref, r_ref, tril_ref, shift_ref,
                o_ref, tail_ref, state_ref, xs_ref, bc_ref, y_ref):
    L = CHUNK

    @pl.when(pl.program_id(1) == 0)
    def _():
        tail_ref[...] = jnp.zeros_like(tail_ref)
        state_ref[...] = jnp.zeros_like(state_ref)

    row8 = lax.broadcasted_iota(jnp.int32, (SUBLANES, 1), 0)
    for c0 in range(0, SSM_CONV_DIM, SSD_STRIP):
        cs = slice(c0, c0 + SSD_STRIP)
        u_b = xbc_ref[0, :, cs]
        u = u_b.astype(F32)
        tail = tail_ref[:, cs]
        acc = u * cw_ref[SSM_CONV - 1:SSM_CONV, cs] + cb_ref[:, cs]
        for d in range(1, SSM_CONV):
            sh = _dot(shift_ref[d - 1], u_b)
            head = sh[0:SUBLANES] + jnp.where(row8 < d, pltpu.roll(tail, d, axis=0), 0.0)
            sh = jnp.concatenate([head, sh[SUBLANES:]], axis=0)
            acc = acc + sh * cw_ref[SSM_CONV - 1 - d:SSM_CONV - d, cs]
        tail_ref[:, cs] = u[L - SUBLANES:L]
        act = _silu(acc)
        if c0 < SSM_D_INNER:
            xs_ref[:, cs] = act
        else:
            bc_ref[:, c0 - SSM_D_INNER:c0 - SSM_D_INNER + SSD_STRIP] = act.astype(BF16)

    dt_raw = dt_ref[0] + dtb_ref[...]
    dtv = jnp.maximum(dt_raw, 0.0) + jnp.log1p(jnp.exp(-jnp.abs(dt_raw)))
    a = dtv * (a_ref[...] * LOG2E)
    tril = tril_ref[...]
    a_hi, a_mid, a_lo = _split3(a)
    acum = _dot(tril, a_hi) + _dot(tril, a_mid) + _dot(tril, a_lo)
    col_t = (acum - jnp.log(dtv) * LOG2E).T
    last = acum[L - 1:L, :]
    e1_b = jnp.exp2(acum).astype(BF16)
    e2_b = (jnp.exp2(last - acum) * dtv).astype(BF16)
    l_hi, l_mid, l_lo = _split3(jnp.broadcast_to(jnp.exp2(last), (SUBLANES, LANES)))
    r = r_ref[...]
    decay_x = (_dot(l_hi, r) + _dot(l_mid, r) + _dot(l_lo, r))[0:1, :]

    causal = (lax.broadcasted_iota(jnp.int32, (L, L), 0) >= lax.broadcasted_iota(jnp.int32, (L, L), 1))
    lane_lo = lax.broadcasted_iota(jnp.int32, (L, LANES), 1) < SSM_HEADDIM
    heads_per_group = SSM_HEADS // SSM_GROUPS
    bc_off = SSM_GROUPS * SSM_STATE
    for g in range(SSM_GROUPS):
        bg_b = bc_ref[:, g * SSM_STATE:(g + 1) * SSM_STATE]
        cg_b = bc_ref[:, bc_off + g * SSM_STATE:bc_off + (g + 1) * SSM_STATE]
        cb = _dot_nt(cg_b, bg_b)
        bgt_b = bg_b.astype(F32).T.astype(BF16)
        for pair in range(heads_per_group // 2):
            p = g * (heads_per_group // 2) + pair
            xp = xs_ref[:, p * LANES:(p + 1) * LANES].astype(BF16)
            y_pair = None
            for sub in range(2):
                h = 2 * p + sub
                seg = jnp.broadcast_to(acum[:, h:h + 1], (L, L)) - col_t[h:h + 1, :]
                w = cb * jnp.exp2(jnp.where(causal, seg, -jnp.inf))
                xh = jnp.where(lane_lo if sub == 0 else jnp.logical_not(lane_lo), xp, jnp.zeros_like(xp))
                yh = _dot(w.astype(BF16), xh)
                y_pair = yh if y_pair is None else y_pair + yh
            y_ref[:, p * LANES:(p + 1) * LANES] = y_pair
        sq = jnp.zeros((L, 1), F32)
        for j in range(GROUP_WIDTH // SSD_STRIP):
            ls = slice(j * SSD_STRIP, (j + 1) * SSD_STRIP)
            cs = slice(g * GROUP_WIDTH + j * SSD_STRIP, g * GROUP_WIDTH + (j + 1) * SSD_STRIP)
            xs_j = xs_ref[:, cs]
            st_j = state_ref[g, :, ls]
            y_j = y_ref[:, cs] + _dot(cg_b, st_j.astype(BF16)) * _dot(e1_b, r_ref[:, cs]) + xs_j * dx_ref[:, cs]
            y_j = y_j * _silu(z_ref[0, :, cs].astype(F32))
            sq = sq + jnp.sum(y_j * y_j, axis=-1, keepdims=True)
            y_ref[:, cs] = y_j
            upd = _dot(bgt_b, (xs_j * _dot(e2_b, r_ref[:, cs])).astype(BF16))
            state_ref[g, :, ls] = st_j * decay_x[:, cs] + upd
        scale = lax.rsqrt(sq * (1.0 / GROUP_WIDTH) + EPS)
        for j in range(GROUP_WIDTH // SSD_STRIP):
            cs = slice(g * GROUP_WIDTH + j * SSD_STRIP, g * GROUP_WIDTH + (j + 1) * SSD_STRIP)
            o_ref[0, :, cs] = (y_ref[:, cs] * scale * nw_ref[:, cs]).astype(o_ref.dtype)


def _ssd(xbc, z, dt, conv_w, conv_b, dt_bias, a_log, d_skip, ssm_norm_w):
    bsz, s, _ = xbc.shape
    pad = DT_PAD - SSM_HEADS
    dtb = jnp.pad(dt_bias.astype(F32), (0, pad)).reshape(1, DT_PAD)
    a_neg = jnp.pad(-jnp.exp(a_log.astype(F32)), (0, pad)).reshape(1, DT_PAD)
    dx = jnp.repeat(d_skip.astype(F32), SSM_HEADDIM).reshape(1, SSM_D_INNER)
    expand = (jnp.arange(DT_PAD)[:, None] == (jnp.arange(SSM_D_INNER)[None, :] // SSM_HEADDIM)).astype(BF16)
    tril = (jnp.arange(CHUNK)[:, None] >= jnp.arange(CHUNK)[None, :]).astype(BF16)
    shift = jnp.stack([(jnp.arange(CHUNK)[:, None] - d == jnp.arange(CHUNK)[None, :]).astype(BF16)
                       for d in range(1, SSM_CONV)])

    def seq(width):
        return pl.BlockSpec((1, CHUNK, width), lambda b, c: (b, c, 0))

    def const(shape):
        return pl.BlockSpec(shape, lambda b, c: (0,) * len(shape))

    return pl.pallas_call(
        _ssd_kernel,
        grid=(bsz, s // CHUNK),
        in_specs=[seq(SSM_CONV_DIM), seq(SSM_D_INNER), seq(DT_PAD),
                  const((SSM_CONV, SSM_CONV_DIM)), const((1, SSM_CONV_DIM)), const((1, DT_PAD)), const((1, DT_PAD)),
                  const((1, SSM_D_INNER)), const((1, SSM_D_INNER)), const((DT_PAD, SSM_D_INNER)),
                  const((CHUNK, CHUNK)), const((SSM_CONV - 1, CHUNK, CHUNK))],
        out_specs=seq(SSM_D_INNER),
        out_shape=jax.ShapeDtypeStruct((bsz, s, SSM_D_INNER), BF16),
        scratch_shapes=[pltpu.VMEM((SUBLANES, SSM_CONV_DIM), F32),
                        pltpu.VMEM((SSM_GROUPS, SSM_STATE, GROUP_WIDTH), F32),
                        pltpu.VMEM((CHUNK, SSM_D_INNER), F32),
                        pltpu.VMEM((CHUNK, 2 * SSM_GROUPS * SSM_STATE), BF16),
                        pltpu.VMEM((CHUNK, SSM_D_INNER), F32)],
        compiler_params=_params(("parallel", "arbitrary")),
        name="ssd",
    )(xbc, z, dt, conv_w.astype(F32), conv_b.astype(F32).reshape(1, SSM_CONV_DIM), dtb, a_neg, dx,
      ssm_norm_w.astype(F32).reshape(1, SSM_D_INNER), expand, tril, shift)


def _ret_consts():
    lg = np.log1p(-(2.0 ** (-5.0 - np.arange(RET_HEADS, dtype=np.float64))))
    idx = np.arange(CHUNK, dtype=np.float64)
    rel = idx[:, None] - idx[None, :]
    intra = np.where(rel >= 0, np.exp(np.maximum(rel, 0.0)[None] * lg[:, None, None]), 0.0)
    qd = np.exp((idx + 1.0)[None, :] * lg[:, None])
    kd = np.exp((CHUNK - 1.0 - idx)[None, :] * lg[:, None]) * (RET_QK_DIM ** -0.5)
    qd = np.broadcast_to(qd[:, :, None], (RET_HEADS, CHUNK, RET_QK_DIM))
    kd = np.broadcast_to(kd[:, :, None], (RET_HEADS, CHUNK, RET_QK_DIM))
    chunk_decay = tuple(float(v) for v in np.exp(CHUNK * lg))
    return (jnp.asarray(intra, F32), jnp.asarray(qd, F32), jnp.asarray(kd, F32), chunk_decay)


def _ret_kernel(q_ref, k_ref, v_ref, g_ref, cos_ref, sin_ref, intra_ref, qd_ref, kd_ref, o_ref, state_ref, *,
                chunk_decay):
    @pl.when(pl.program_id(1) == 0)
    def _():
        state_ref[...] = jnp.zeros_like(state_ref)

    cos = cos_ref[0]
    sin = sin_ref[0]

    def rot(t):
        t1, t2 = t[:, :ROPE_HALF], t[:, ROPE_HALF:]
        return jnp.concatenate([t1 * cos - t2 * sin, t1 * sin + t2 * cos], axis=1)

    for h in range(RET_HEADS):
        qs = slice(h * RET_QK_DIM, (h + 1) * RET_QK_DIM)
        vs = slice(h * RET_V_DIM, (h + 1) * RET_V_DIM)
        qr = rot(q_ref[0, :, qs].astype(F32))
        kr = rot(k_ref[0, :, qs].astype(F32))
        vh = v_ref[0, :, vs]
        s = _dot_nt(qr.astype(BF16), (kr * (RET_QK_DIM ** -0.5)).astype(BF16)) * intra_ref[h]
        st = state_ref[h]
        y = _dot(s.astype(BF16), vh) + _dot((qr * qd_ref[h]).astype(BF16), st.astype(BF16))
        state_ref[h] = st * chunk_decay[h] + _dot((kr * kd_ref[h]).T.astype(BF16), vh)
        mu = jnp.mean(y, axis=-1, keepdims=True)
        yc = y - mu
        var = jnp.mean(yc * yc, axis=-1, keepdims=True)
        o_ref[0, :, vs] = (_silu(g_ref[0, :, vs].astype(F32)) * (yc * lax.rsqrt(var + EPS))).astype(o_ref.dtype)


def _ret(q, k, v, g, cos, sin):
    bsz, s, _ = q.shape
    intra, qd, kd, chunk_decay = _ret_consts()

    def seq(width):
        return pl.BlockSpec((1, CHUNK, width), lambda b, c: (b, c, 0))

    def const(shape):
        return pl.BlockSpec(shape, lambda b, c: (0,) * len(shape))

    return pl.pallas_call(
        functools.partial(_ret_kernel, chunk_decay=chunk_decay),
        grid=(bsz, s // CHUNK),
        in_specs=[seq(RET_QK_WIDTH), seq(RET_QK_WIDTH), seq(RET_V_WIDTH), seq(RET_V_WIDTH),
                  seq(ROPE_HALF), seq(ROPE_HALF),
                  const(intra.shape), const(qd.shape), const(kd.shape)],
        out_specs=seq(RET_V_WIDTH),
        out_shape=jax.ShapeDtypeStruct((bsz, s, RET_V_WIDTH), BF16),
        scratch_shapes=[pltpu.VMEM((RET_HEADS, RET_QK_DIM, RET_V_DIM), F32)],
        compiler_params=_params(("parallel", "arbitrary")),
        name="ret",
    )(q, k, v, g, cos, sin, intra, qd, kd)


def _merge_kernel(ys_ref, yr_ref, gs_ref, gr_ref, x_ref, g1_ref, sc2_ref, sh2_ref, n2w_ref,
                  wso_ref, wro_ref, wo_ref, wrh_ref, wrl_ref, x1_ref, h2_ref, sct_ref):
    y_ssm = _dot(ys_ref[0], wso_ref[...])
    y_ret = _dot(yr_ref[0], wro_ref[...])
    merged = _sigmoid(gs_ref[0].astype(F32)) * y_ssm + _sigmoid(gr_ref[0].astype(F32)) * y_ret
    x1 = x_ref[0] + g1_ref[0] * _dot(merged.astype(BF16), wo_ref[...])
    x1_ref[0] = x1
    h2 = _rms_mod(x1, n2w_ref[...], sc2_ref[0], sh2_ref[0])
    h_hi = h2.astype(BF16)
    h2_ref[0] = _pack_rows(h2)
    h_lo = (h2 - h_hi.astype(F32)).astype(BF16)
    wrh = wrh_ref[...]
    logits_t = _dot_nt(wrh, h_hi) + _dot_nt(wrh, h_lo) + _dot_nt(wrl_ref[...], h_hi)
    sct_ref[...] = _sigmoid(logits_t)


def _merge(ys, yr, gs, gr, x, gate1, scale2, shift2, norm2_w, w_ssm_out, w_ret_out, w_out, w_router, tm):
    bsz, s, d = x.shape
    nt = s // tm
    wrt = w_router.astype(F32).T
    wrh = wrt.astype(BF16)
    wrl = (wrt - wrh.astype(F32)).astype(BF16)

    def seq(width):
        return pl.BlockSpec((1, tm, width), lambda b, i: (b, i, 0))

    def const(shape):
        return pl.BlockSpec(shape, lambda b, i: (0,) * len(shape))

    vec = pl.BlockSpec((1, 1, d), lambda b, i: (b, 0, 0))
    return pl.pallas_call(
        _merge_kernel,
        grid=(bsz, nt),
        in_specs=[seq(SSM_D_INNER), seq(RET_V_WIDTH), seq(d), seq(d), seq(d), vec, vec, vec, const((1, d)),
                  const((SSM_D_INNER, d)), const((RET_V_WIDTH, d)), const((d, d)),
                  const((N_EXPERTS, d)), const((N_EXPERTS, d))],
        out_specs=[seq(d), seq(PACKED), pl.BlockSpec((N_EXPERTS, tm), lambda b, i: (0, b * nt + i))],
        out_shape=[jax.ShapeDtypeStruct((bsz, s, d), F32), jax.ShapeDtypeStruct((bsz, s, PACKED), jnp.int32),
                   jax.ShapeDtypeStruct((N_EXPERTS, bsz * s), F32)],
        compiler_params=_params(("parallel", "parallel")),
        name="merge",
    )(ys, yr, gs, gr, x, gate1, scale2, shift2, norm2_w.astype(F32).reshape(1, d),
      w_ssm_out.astype(BF16), w_ret_out.astype(BF16), w_out.astype(BF16), wrh, wrl)


def _route_kernel(sc_ref, bias_ref, upper_ref, ones_ref, idx_ref, wts_ref, rank_ref, cnt_ref, carry_ref):
    @pl.when(pl.program_id(0) == 0)
    def _():
        carry_ref[...] = jnp.zeros_like(carry_ref)

    tl = sc_ref.shape[1]
    neg = -jnp.inf
    scores = sc_ref[...]
    sel = scores + bias_ref[...]

    def first_argmax(vals, iota, n):
        m = jnp.max(vals, axis=0, keepdims=True)
        return jnp.min(jnp.where(vals == m, iota, float(n)), axis=0, keepdims=True), m

    io_g = lax.broadcasted_iota(jnp.int32, (ROUTE_GROUP_SIZE, tl), 0).astype(F32)
    rows = []
    for g in range(N_ROUTE_GROUPS):
        blk = sel[g * ROUTE_GROUP_SIZE:(g + 1) * ROUTE_GROUP_SIZE]
        i1, m1 = first_argmax(blk, io_g, ROUTE_GROUP_SIZE)
        m2 = jnp.max(jnp.where(io_g == i1, neg, blk), axis=0, keepdims=True)
        rows.append(m1 + m2)
    gsc = jnp.concatenate(rows, axis=0)
    io_8 = lax.broadcasted_iota(jnp.int32, (N_ROUTE_GROUPS, tl), 0).astype(F32)
    chosen = jnp.zeros((N_ROUTE_GROUPS, tl), F32)
    for _ in range(TOPK_ROUTE_GROUPS):
        i, _m = first_argmax(gsc, io_8, N_ROUTE_GROUPS)
        hit = io_8 == i
        chosen = jnp.where(hit, 1.0, chosen)
        gsc = jnp.where(hit, neg, gsc)
    msel = jnp.concatenate(
        [jnp.where(chosen[g:g + 1] > 0.5, sel[g * ROUTE_GROUP_SIZE:(g + 1) * ROUTE_GROUP_SIZE], neg)
         for g in range(N_ROUTE_GROUPS)], axis=0)

    io_e = lax.broadcasted_iota(jnp.int32, (N_EXPERTS, tl), 0).astype(F32)
    multi = jnp.zeros((N_EXPERTS, tl), F32)
    idx_rows, sc_rows = [], []
    for _ in range(TOP_K):
        i, _m = first_argmax(msel, io_e, N_EXPERTS)
        hit = io_e == i
        sc_rows.append(jnp.sum(jnp.where(hit, scores, 0.0), axis=0, keepdims=True))
        msel = jnp.where(hit, neg, msel)
        multi = jnp.where(hit, 1.0, multi)
        idx_rows.append(i)
    den = sc_rows[0]
    for r in sc_rows[1:]:
        den = den + r
    den = den + 1e-20
    wts_ref[...] = jnp.concatenate([r / den * ROUTED_SCALE for r in sc_rows], axis=0)
    idx_ref[...] = jnp.concatenate(idx_rows, axis=0).astype(jnp.int32)

    multi_b = multi.astype(BF16)
    carry = carry_ref[...]
    rank_full = _dot(multi_b, upper_ref[...]) + carry
    rank_ref[...] = jnp.concatenate(
        [jnp.sum(jnp.where(io_e == i, rank_full, 0.0), axis=0, keepdims=True) for i in idx_rows],
        axis=0).astype(jnp.int32)
    carry = carry + _dot(multi_b, ones_ref[...])
    carry_ref[...] = carry
    cnt_ref[...] = carry[:, :LANES]


def _route(scores_t, router_bias):
    e, t = scores_t.shape
    tl = min(ROUTE_TILE, t)
    bias = jnp.broadcast_to(router_bias.astype(F32).reshape(e, 1), (e, tl))
    upper = (jnp.arange(tl)[:, None] < jnp.arange(tl)[None, :]).astype(BF16)
    ones = jnp.ones((tl, tl), BF16)
    tok = pl.BlockSpec((TOP_K, tl), lambda i: (0, i))

    def const(shape):
        return pl.BlockSpec(shape, lambda i: (0,) * len(shape))

    return pl.pallas_call(
        _route_kernel,
        grid=(t // tl,),
        in_specs=[pl.BlockSpec((e, tl), lambda i: (0, i)), const((e, tl)), const((tl, tl)), const((tl, tl))],
        out_specs=[tok, tok, tok, const((e, LANES))],
        out_shape=[jax.ShapeDtypeStruct((TOP_K, t), jnp.int32), jax.ShapeDtypeStruct((TOP_K, t), F32),
                   jax.ShapeDtypeStruct((TOP_K, t), jnp.int32), jax.ShapeDtypeStruct((e, LANES), F32)],
        scratch_shapes=[pltpu.VMEM((e, tl), F32)],
        compiler_params=_params(("arbitrary",)),
        name="route",
    )(scores_t, bias, upper, ones)


def _dest_kernel(idx_ref, rank_ref, start_ref, o_ref):
    tl = idx_ref.shape[1]
    io_e = lax.broadcasted_iota(jnp.int32, (N_EXPERTS, tl), 0).astype(F32)
    idx = idx_ref[...].astype(F32)
    start = start_ref[...]
    rows = [jnp.sum(jnp.where(io_e == idx[kk:kk + 1], start, 0.0), axis=0, keepdims=True) for kk in range(TOP_K)]
    o_ref[...] = jnp.concatenate(rows, axis=0).astype(jnp.int32) + rank_ref[...]


def _dest(idx_t, rank_t, padded_start):
    k, t = idx_t.shape
    tl = min(2 * ROUTE_TILE, t)
    start = jnp.broadcast_to(padded_start.astype(F32).reshape(N_EXPERTS, 1), (N_EXPERTS, tl))
    tok = pl.BlockSpec((k, tl), lambda i: (0, i))
    return pl.pallas_call(
        _dest_kernel,
        grid=(t // tl,),
        in_specs=[tok, tok, pl.BlockSpec((N_EXPERTS, tl), lambda i: (0, 0))],
        out_specs=tok,
        out_shape=jax.ShapeDtypeStruct((k, t), jnp.int32),
        compiler_params=_params(("parallel",)),
        name="dest",
    )(idx_t, rank_t, start)


def _expert_kernel(be_ref, nv_ref, slot_ref, nxt_ref, x_ref, wg_hbm, wu_hbm, wd_hbm, o_ref,
                   wg_buf, wu_buf, wd_buf, wgb_ref, wub_ref, wdb_ref, sem):
    b = pl.program_id(0)
    nv = nv_ref[b]

    def weight_copies(expert, slot):
        return (pltpu.make_async_copy(wg_hbm.at[expert], wg_buf.at[slot], sem.at[slot, 0]),
                pltpu.make_async_copy(wu_hbm.at[expert], wu_buf.at[slot], sem.at[slot, 1]),
                pltpu.make_async_copy(wd_hbm.at[expert], wd_buf.at[slot], sem.at[slot, 2]))

    @pl.when(b == 0)
    def _():
        for cp in weight_copies(be_ref[0], slot_ref[0]):
            cp.start()

    @pl.when(jnp.logical_or(b == 0, be_ref[b] != be_ref[jnp.maximum(b - 1, 0)]))
    def _():
        slot = slot_ref[b]
        for cp in weight_copies(be_ref[b], slot):
            cp.wait()
        nxt = nxt_ref[b]

        @pl.when(nxt >= 0)
        def _():
            for cp in weight_copies(nxt, 1 - slot):
                cp.start()

        wgb_ref[...] = wg_buf[slot].astype(BF16)
        wub_ref[...] = wu_buf[slot].astype(BF16)
        wdb_ref[...] = wd_buf[slot].astype(BF16)

    @pl.when(nv > 0)
    def _():
        row = lax.broadcasted_iota(jnp.int32, x_ref.shape, 0)
        x = _unpack_rows(jnp.where(row < nv, x_ref[...], 0))
        g = _dot(x, wgb_ref[...])
        u = _dot(x, wub_ref[...])
        o_ref[...] = _pack_rows(_dot((_silu(g) * u).astype(BF16), wdb_ref[...]))

    @pl.when(nv == 0)
    def _():
        o_ref[...] = jnp.zeros_like(o_ref)


def _experts(block_expert, block_valid, xs, w_gate, w_up, w_down):
    n_slots = xs.shape[0]
    n_blocks = n_slots // EXPERT_BLOCK
    d, f = w_gate.shape[-2:]
    ids = jnp.arange(n_blocks, dtype=jnp.int32)
    change = jnp.concatenate([jnp.ones((1,), bool), block_expert[1:] != block_expert[:-1]])
    run_slot = ((jnp.cumsum(change.astype(jnp.int32)) - 1) % 2).astype(jnp.int32)
    change_at = jnp.where(change, ids, n_blocks)
    next_change = jnp.concatenate([lax.cummin(change_at[::-1])[::-1][1:], jnp.full((1,), n_blocks, jnp.int32)])
    next_expert = jnp.where(next_change < n_blocks, block_expert[jnp.minimum(next_change, n_blocks - 1)], -1)
    hbm = pl.BlockSpec(memory_space=pl.ANY)
    grid_spec = pltpu.PrefetchScalarGridSpec(
        num_scalar_prefetch=4,
        grid=(n_blocks,),
        in_specs=[pl.BlockSpec((EXPERT_BLOCK, PACKED), lambda b, be, nv, sl, nx: (jnp.where(nv[b] > 0, b, 0), 0)),
                  hbm, hbm, hbm],
        out_specs=pl.BlockSpec((EXPERT_BLOCK, PACKED), lambda b, be, nv, sl, nx: (b, 0)),
        scratch_shapes=[pltpu.VMEM((2, d, f), w_gate.dtype), pltpu.VMEM((2, d, f), w_up.dtype),
                        pltpu.VMEM((2, f, d), w_down.dtype),
                        pltpu.VMEM((d, f), BF16), pltpu.VMEM((d, f), BF16), pltpu.VMEM((f, d), BF16),
                        pltpu.SemaphoreType.DMA((2, 3))],
    )
    return pl.pallas_call(
        _expert_kernel,
        grid_spec=grid_spec,
        out_shape=jax.ShapeDtypeStruct((n_slots, PACKED), jnp.int32),
        compiler_params=_params(("arbitrary",)),
        name="experts",
    )(block_expert, block_valid, run_slot, next_expert.astype(jnp.int32), xs, w_gate, w_up, w_down)


def _final_kernel(x1_ref, h2_ref, yk_ref, wts_ref, g2_ref, wsg_ref, wsu_ref, wsd_ref, fnw_ref, o_ref):
    h = _unpack_rows(h2_ref[0])
    act = (_silu(_dot(h, wsg_ref[...])) * _dot(h, wsu_ref[...])).astype(BF16)
    moe = _dot(act, wsd_ref[...])
    wts = wts_ref[...]
    for kk in range(TOP_K):
        moe = moe + _unpack_rows(yk_ref[kk]).astype(F32) * wts[:, kk:kk + 1]
    x2 = x1_ref[0] + g2_ref[0] * moe
    ms = jnp.mean(x2 * x2, axis=-1, keepdims=True)
    o_ref[0] = x2 * lax.rsqrt(ms + EPS) * fnw_ref[...]


def _final(acc, h2p, yk, wts, gate2, w_sh_gate, w_sh_up, w_sh_down, final_norm_w, tm, row0):
    bsz, s, d = acc.shape
    nt = s // tm
    rows = yk.shape[1] // s
    f = w_sh_gate.shape[-1]
    seq = pl.BlockSpec((1, tm, d), lambda b, i: (row0 + b, i, 0))

    def const(shape):
        return pl.BlockSpec(shape, lambda b, i: (0,) * len(shape))

    return pl.pallas_call(
        _final_kernel,
        grid=(rows, nt),
        in_specs=[seq, pl.BlockSpec((1, tm, PACKED), lambda b, i: (row0 + b, i, 0)),
                  pl.BlockSpec((TOP_K, tm, PACKED), lambda b, i: (0, b * nt + i, 0)),
                  pl.BlockSpec((tm, TOP_K), lambda b, i: ((row0 + b) * nt + i, 0)),
                  pl.BlockSpec((1, 1, d), lambda b, i: (row0 + b, 0, 0)),
                  const((d, f)), const((d, f)), const((f, d)), const((1, d))],
        out_specs=seq,
        out_shape=jax.ShapeDtypeStruct((bsz, s, d), F32),
        input_output_aliases={0: 0},
        compiler_params=_params(("parallel", "parallel")),
        name="final",
    )(acc, h2p, yk, wts, gate2, w_sh_gate.astype(BF16), w_sh_up.astype(BF16), w_sh_down.astype(BF16),
      final_norm_w.astype(F32).reshape(1, d))


def _gather_rows(table, idx):
    n = idx.shape[0]
    width = table.shape[1]
    workers = SC_CORES * SC_SUBCORES
    per_worker = n // workers
    steps = per_worker // SC_WINDOW
    assert per_worker * workers == n and steps * SC_WINDOW == per_worker and steps % 2 == 0
    mesh = plsc.VectorSubcoreMesh(core_axis_name="c", subcore_axis_name="s")
    win = lambda dtype, *shape: pltpu.VMEM(shape, dtype)

    @functools.partial(
        pl.kernel, mesh=mesh,
        out_type=jax.ShapeDtypeStruct((n, width), table.dtype),
        scratch_types=[win(jnp.int32, SC_WINDOW), win(jnp.int32, SC_WINDOW),
                       win(table.dtype, SC_WINDOW, width), win(table.dtype, SC_WINDOW, width),
                       pltpu.SemaphoreType.DMA, pltpu.SemaphoreType.DMA,
                       pltpu.SemaphoreType.DMA, pltpu.SemaphoreType.DMA],
        name="gather_rows",
    )
    def run(table_hbm, idx_hbm, out_hbm, idx_a, idx_b, rows_a, rows_b, gsem_a, gsem_b, osem_a, osem_b):
        base = (lax.axis_index("s") * SC_CORES + lax.axis_index("c")) * per_worker

        @pl.loop(0, steps, step=2)
        def _(i):
            off_a = pl.multiple_of(base + i * SC_WINDOW, SC_WINDOW)
            off_b = pl.multiple_of(off_a + SC_WINDOW, SC_WINDOW)
            pltpu.sync_copy(idx_hbm.at[pl.ds(off_a, SC_WINDOW)], idx_a)
            gather_a = pltpu.async_copy(table_hbm.at[idx_a], rows_a, gsem_a)
            pltpu.sync_copy(idx_hbm.at[pl.ds(off_b, SC_WINDOW)], idx_b)
            gather_b = pltpu.async_copy(table_hbm.at[idx_b], rows_b, gsem_b)
            gather_a.wait()
            store_a = pltpu.async_copy(rows_a, out_hbm.at[pl.ds(off_a, SC_WINDOW)], osem_a)
            gather_b.wait()
            store_b = pltpu.async_copy(rows_b, out_hbm.at[pl.ds(off_b, SC_WINDOW)], osem_b)
            store_a.wait()
            store_b.wait()

    return run(table, idx)


def _scatter_rows(rows, dest_kt, n_slots):
    t, width = rows.shape
    top_k = dest_kt.shape[0]
    workers = SC_CORES * SC_SUBCORES
    window = LANES
    per_worker = t // workers
    steps = per_worker // window
    assert per_worker * workers == t and steps * window == per_worker
    mesh = plsc.VectorSubcoreMesh(core_axis_name="c", subcore_axis_name="s")

    @functools.partial(
        pl.kernel, mesh=mesh,
        out_type=jax.ShapeDtypeStruct((n_slots, width), rows.dtype),
        scratch_types=[pltpu.VMEM((top_k, window), jnp.int32),
                       pltpu.VMEM((window, width), rows.dtype),
                       pltpu.SemaphoreType.DMA],
        name="scatter_rows",
    )
    def run(rows_hbm, dest_hbm, out_hbm, idx_v, rows_v, sem):
        base = (lax.axis_index("s") * SC_CORES + lax.axis_index("c")) * per_worker

        @pl.loop(0, steps)
        def _(i):
            off = pl.multiple_of(base + i * window, window)
            pltpu.sync_copy(rows_hbm.at[pl.ds(off, window)], rows_v)
            pltpu.sync_copy(dest_hbm.at[:, pl.ds(off, window)], idx_v)
            copies = [pltpu.async_copy(rows_v, out_hbm.at[idx_v.at[kk]], sem) for kk in range(top_k)]
            for cp in copies:
                cp.wait()

    return run(rows, dest_kt)


def _layer(x, c, positions, w_ada, b_ada, norm1_w, w_in, conv_w, conv_b, dt_bias, a_log, d_skip, ssm_norm_w,
           w_ssm_out, w_ret_out, w_out, norm2_w, w_router, router_bias, w_exp_gate, w_exp_up, w_exp_down,
           w_sh_gate, w_sh_up, w_sh_down):
    bsz, s, d = x.shape
    t = bsz * s
    tm = min(TOKEN_TILE, s)

    mod = _mod(c, w_ada, b_ada)
    shift1, scale1, gate1, shift2, scale2, gate2 = (mod[:, i * d:(i + 1) * d].reshape(bsz, 1, d) for i in range(6))

    bounds = np.cumsum((0, SSM_D_INNER, SSM_CONV_DIM, SSM_HEADS, RET_QK_WIDTH, RET_QK_WIDTH, RET_V_WIDTH,
                        RET_V_WIDTH, d, d))
    wz, wxbc, wdt, wq, wk, wv, wg, wgs, wgr = (w_in[:, bounds[i]:bounds[i + 1]] for i in range(9))
    nw1 = norm1_w.astype(F32).reshape(1, d)
    w_a = jnp.concatenate([wxbc, wz, jnp.pad(wdt, ((0, 0), (0, DT_PAD - SSM_HEADS)))], axis=1).astype(BF16)
    w_b = jnp.concatenate([wq, wk, wv, wg, wgs, wgr], axis=1).astype(BF16)
    xbc, z, dt = _inproj(x, nw1, scale1, shift1, w_a, (SSM_CONV_DIM, SSM_D_INNER, DT_PAD), (BF16, BF16, F32), tm)
    q, k, v, g, gs, gr = _inproj(x, nw1, scale1, shift1, w_b,
                                 (RET_QK_WIDTH, RET_QK_WIDTH, RET_V_WIDTH, RET_V_WIDTH, d, d), (BF16,) * 6, tm)

    ys = _ssd(xbc, z, dt, conv_w, conv_b, dt_bias, a_log, d_skip, ssm_norm_w)
    cos, sin = _rope(positions, tm)
    yr = _ret(q, k, v, g, cos, sin)
    x1, h2, scores_t = _merge(ys, yr, gs, gr, x, gate1, scale2, shift2, norm2_w, w_ssm_out, w_ret_out, w_out,
                              w_router, tm)

    idx_t, wts_t, rank_t, cnt = _route(scores_t, router_bias)

    counts = cnt[:, 0].astype(jnp.int32)
    padded = (counts + EXPERT_BLOCK - 1) // EXPERT_BLOCK * EXPERT_BLOCK
    padded_end = jnp.cumsum(padded)
    padded_start = padded_end - padded
    n_blocks = -(-(t * TOP_K) // EXPERT_BLOCK) + N_EXPERTS
    n_slots = n_blocks * EXPERT_BLOCK
    dest_t = _dest(idx_t, rank_t, padded_start)
    block_first = jnp.arange(n_blocks, dtype=jnp.int32) * EXPERT_BLOCK
    block_expert = jnp.minimum(jnp.sum(padded_end[None, :] <= block_first[:, None], axis=1), N_EXPERTS - 1)
    owner = (block_expert[:, None] == jnp.arange(N_EXPERTS, dtype=jnp.int32)[None, :]).astype(jnp.int32)
    region_end = jnp.sum(owner * (padded_start + counts)[None, :], axis=1)
    block_valid = jnp.clip(region_end - block_first, 0, EXPERT_BLOCK).astype(jnp.int32)
    xs = _scatter_rows(h2.reshape(t, PACKED), dest_t, n_slots)
    y_sorted = _experts(block_expert.astype(jnp.int32), block_valid, xs, w_exp_gate, w_exp_up, w_exp_down)
    return x1, h2, y_sorted, dest_t, wts_t.T, gate2


def kernel(x, c, positions, w_ada, b_ada, norm1_w, w_in, conv_w, conv_b, dt_bias, a_log, d_skip, ssm_norm_w,
           w_ssm_out, w_ret_out, w_out, norm2_w, w_router, router_bias, w_exp_gate, w_exp_up, w_exp_down,
           w_sh_gate, w_sh_up, w_sh_down, final_norm_w):
    assert w_ada.shape[0] == 1, "the final rmsnorm is fused into the single layer's last kernel"
    tm = min(TOKEN_TILE, x.shape[1])
    x1, h2, y_sorted, dest_t, wts, gate2 = _layer(
        x, c, positions, w_ada[0], b_ada[0], norm1_w[0], w_in[0], conv_w[0], conv_b[0], dt_bias[0], a_log[0],
        d_skip[0], ssm_norm_w[0], w_ssm_out[0], w_ret_out[0], w_out[0], norm2_w[0], w_router[0],
        router_bias[0], w_exp_gate[0], w_exp_up[0], w_exp_down[0], w_sh_gate[0], w_sh_up[0], w_sh_down[0])
    bsz, s, _ = x.shape
    rows = bsz // COMBINE_PARTS if bsz % COMBINE_PARTS == 0 else bsz
    out = x1
    for row0 in range(0, bsz, rows):
        part = dest_t[:, row0 * s:(row0 + rows) * s]
        yk = _gather_rows(y_sorted, part.reshape(-1)).reshape(TOP_K, rows * s, PACKED)
        out = _final(out, h2, yk, wts, gate2, w_sh_gate[0], w_sh_up[0], w_sh_down[0], final_norm_w, tm, row0)
    return out
```

```python
import functools

import numpy as np
import jax
import jax.numpy as jnp
from jax import lax
from jax.experimental import pallas as pl
from jax.experimental.pallas import tpu as pltpu
from jax.experimental.pallas import tpu_sc as plsc

F32 = jnp.float32
BF16 = jnp.bfloat16

EPS = 1e-6
D_MODEL = 1024
SSM_D_INNER = 2048
SSM_HEADDIM = 64
SSM_HEADS = 32
SSM_GROUPS = 4
SSM_STATE = 128
SSM_CONV = 4
SSM_CONV_DIM = SSM_D_INNER + 2 * SSM_GROUPS * SSM_STATE
GROUP_WIDTH = SSM_D_INNER // SSM_GROUPS
RET_HEADS = 4
RET_QK_DIM = 256
RET_V_DIM = 512
RET_QK_WIDTH = RET_HEADS * RET_QK_DIM
RET_V_WIDTH = RET_HEADS * RET_V_DIM
ROPE_BASE = 10000.0
ROPE_HALF = RET_QK_DIM // 2
N_EXPERTS = 256
TOP_K = 8
N_ROUTE_GROUPS = 8
TOPK_ROUTE_GROUPS = 4
ROUTE_GROUP_SIZE = N_EXPERTS // N_ROUTE_GROUPS
EXPERT_DIM = 256
ROUTED_SCALE = 2.5

TOKEN_TILE = 512
CHUNK = 128
SSD_STRIP = 256
LANES = 128
SUBLANES = 8
DT_PAD = LANES
EXPERT_BLOCK = 512
ROUTE_TILE = 256
VMEM_LIMIT = 56 * 1024 * 1024
PACKED = D_MODEL // 2
SC_CORES = 2
SC_SUBCORES = 16
SC_WINDOW = 64


LOG2E = 1.4426950408889634


def _silu(x):
    return x / (1.0 + jnp.exp2(x * -LOG2E))


def _sigmoid(x):
    return 1.0 / (1.0 + jnp.exp2(x * -LOG2E))


def _split3(a):
    hi = a.astype(BF16)
    r1 = a - hi.astype(F32)
    mid = r1.astype(BF16)
    lo = (r1 - mid.astype(F32)).astype(BF16)
    return hi, mid, lo


def _pack_rows(x):
    w = x.shape[1] // 2
    hi = lax.bitcast_convert_type(x[:, :w].astype(BF16).astype(F32), jnp.uint32)
    lo = lax.bitcast_convert_type(x[:, w:].astype(BF16).astype(F32), jnp.uint32)
    return lax.bitcast_convert_type(hi | (lo >> 16), jnp.int32)


def _unpack_rows(p):
    u = lax.bitcast_convert_type(p, jnp.uint32)
    hi = lax.bitcast_convert_type(u & jnp.uint32(0xFFFF0000), F32)
    lo = lax.bitcast_convert_type(u << 16, F32)
    return jnp.concatenate([hi, lo], axis=1).astype(BF16)


def _dot(a, b):
    return jnp.dot(a, b, preferred_element_type=F32)


def _dot_nt(a, b):
    return lax.dot_general(a, b, (((1,), (1,)), ((), ())), preferred_element_type=F32)


def _params(sem):
    return pltpu.CompilerParams(dimension_semantics=sem, vmem_limit_bytes=VMEM_LIMIT)


def _mod_kernel(c_ref, w_ref, b_ref, o_ref):
    o_ref[...] = _dot(_silu(c_ref[...]), w_ref[...]) + b_ref[...]


def _mod(c, w_ada, b_ada):
    bsz, d = c.shape
    n = w_ada.shape[1]
    return pl.pallas_call(
        _mod_kernel,
        grid=(n // d,),
        in_specs=[pl.BlockSpec((bsz, d), lambda j: (0, 0)),
                  pl.BlockSpec((d, d), lambda j: (0, j)),
                  pl.BlockSpec((1, d), lambda j: (0, j))],
        out_specs=pl.BlockSpec((bsz, d), lambda j: (0, j)),
        out_shape=jax.ShapeDtypeStruct((bsz, n), F32),
        compiler_params=_params(("arbitrary",)),
        name="mod",
    )(c, w_ada, b_ada.reshape(1, n))


def _rms_mod(x, nw, scale, shift):
    ms = jnp.mean(x * x, axis=-1, keepdims=True)
    return (x * lax.rsqrt(ms + EPS) * nw) * (1.0 + scale) + shift


PROJ_STRIP = 512


def _project(hb, w_ref, off, o_ref, width, act):
    for c0 in range(0, width, PROJ_STRIP):
        cw = min(PROJ_STRIP, width - c0)
        v = _dot(hb, w_ref[:, off + c0:off + c0 + cw])
        o_ref[0, :, c0:c0 + cw] = (v if act is None else act(v)).astype(o_ref.dtype)


def _inproj_ssd_kernel(x_ref, nw_ref, sc_ref, sh_ref, w_ref, h_ref, xbc_ref, zs_ref, dt_ref):
    hb = _rms_mod(x_ref[0], nw_ref[...], sc_ref[0], sh_ref[0]).astype(BF16)
    h_ref[0] = hb
    _project(hb, w_ref, 0, xbc_ref, SSM_CONV_DIM, None)
    _project(hb, w_ref, SSM_CONV_DIM, zs_ref, SSM_D_INNER, _silu)
    _project(hb, w_ref, SSM_CONV_DIM + SSM_D_INNER, dt_ref, DT_PAD, None)


def _inproj_ret_kernel(h_ref, w_ref, q_ref, k_ref, v_ref, gs_ref, gate_s_ref, gate_r_ref):
    hb = h_ref[0]
    off = 0
    for o_ref, act in ((q_ref, None), (k_ref, None), (v_ref, None), (gs_ref, _silu), (gate_s_ref, None),
                       (gate_r_ref, None)):
        width = o_ref.shape[-1]
        _project(hb, w_ref, off, o_ref, width, act)
        off += width


def _inproj(body, rows, vecs, w, widths, dtypes, tm):
    bsz, s, d = rows.shape
    vec_specs = [pl.BlockSpec((1, d), lambda b, i: (0, 0)) if v.ndim == 2 else
                 pl.BlockSpec((1, 1, d), lambda b, i: (b, 0, 0)) for v in vecs]
    return pl.pallas_call(
        body,
        grid=(bsz, s // tm),
        in_specs=[pl.BlockSpec((1, tm, d), lambda b, i: (b, i, 0))] + vec_specs
                 + [pl.BlockSpec(w.shape, lambda b, i: (0, 0))],
        out_specs=[pl.BlockSpec((1, tm, wd), lambda b, i: (b, i, 0)) for wd in widths],
        out_shape=[jax.ShapeDtypeStruct((bsz, s, wd), dt) for wd, dt in zip(widths, dtypes)],
        compiler_params=_params(("parallel", "parallel")),
        name="inproj",
    )(rows, *vecs, w)


def _rope_kernel(pos_ref, inv_ref, cos_ref, sin_ref):
    ang = pos_ref[0].astype(F32) * inv_ref[...]
    cos_ref[0] = jnp.cos(ang)
    sin_ref[0] = jnp.sin(ang)


def _rope(positions, tm):
    bsz, s = positions.shape
    inv = (1.0 / (ROPE_BASE ** (jnp.arange(ROPE_HALF, dtype=F32) / ROPE_HALF))).reshape(1, ROPE_HALF)
    spec = pl.BlockSpec((1, tm, ROPE_HALF), lambda b, i: (b, i, 0))
    return pl.pallas_call(
        _rope_kernel,
        grid=(bsz, s // tm),
        in_specs=[pl.BlockSpec((1, tm, 1), lambda b, i: (b, i, 0)),
                  pl.BlockSpec((1, ROPE_HALF), lambda b, i: (0, 0))],
        out_specs=[spec, spec],
        out_shape=[jax.ShapeDtypeStruct((bsz, s, ROPE_HALF), F32)] * 2,
        compiler_params=_params(("parallel", "parallel")),
        name="rope",
    )(positions.reshape(bsz, s, 1), inv)


def _ssd_kernel(xbc_ref, zs_ref, dt_ref, cw_ref, cb_ref, dtb_ref, a_ref, dx_ref, nw_ref, r_ref, tril_ref, shift_ref,
                o_ref, tail_ref, state_ref, xs_ref, bc_ref, y_ref):
    L = CHUNK

    @pl.when(pl.program_id(1) == 0)
    def _():
        tail_ref[...] = jnp.zeros_like(tail_ref)
        state_ref[...] = jnp.zeros_like(state_ref)

    row8 = lax.broadcasted_iota(jnp.int32, (SUBLANES, 1), 0)
    for c0 in range(0, SSM_CONV_DIM, SSD_STRIP):
        cs = slice(c0, c0 + SSD_STRIP)
        u_b = xbc_ref[0, :, cs]
        u = u_b.astype(F32)
        tail = tail_ref[:, cs]
        acc = u * cw_ref[SSM_CONV - 1:SSM_CONV, cs] + cb_ref[:, cs]
        for d in range(1, SSM_CONV):
            sh = _dot(shift_ref[d - 1], u_b)
            head = sh[0:SUBLANES] + jnp.where(row8 < d, pltpu.roll(tail, d, axis=0), 0.0)
            sh = jnp.concatenate([head, sh[SUBLANES:]], axis=0)
            acc = acc + sh * cw_ref[SSM_CONV - 1 - d:SSM_CONV - d, cs]
        tail_ref[:, cs] = u[L - SUBLANES:L]
        act = _silu(acc)
        if c0 < SSM_D_INNER:
            xs_ref[:, cs] = act
        else:
            bc_ref[:, c0 - SSM_D_INNER:c0 - SSM_D_INNER + SSD_STRIP] = act.astype(BF16)

    dt_raw = dt_ref[0] + dtb_ref[...]
    dtv = jnp.maximum(dt_raw, 0.0) + jnp.log1p(jnp.exp(-jnp.abs(dt_raw)))
    a = dtv * (a_ref[...] * LOG2E)
    tril = tril_ref[...]
    a_hi, a_mid, a_lo = _split3(a)
    acum = _dot(tril, a_hi) + _dot(tril, a_mid) + _dot(tril, a_lo)
    col_t = (acum - jnp.log(dtv) * LOG2E).T
    last = acum[L - 1:L, :]
    e1_b = jnp.exp2(acum).astype(BF16)
    e2_b = (jnp.exp2(last - acum) * dtv).astype(BF16)
    l_hi, l_mid, l_lo = _split3(jnp.broadcast_to(jnp.exp2(last), (SUBLANES, LANES)))
    r = r_ref[...]
    decay_x = (_dot(l_hi, r) + _dot(l_mid, r) + _dot(l_lo, r))[0:1, :]

    causal = (lax.broadcasted_iota(jnp.int32, (L, L), 0) >= lax.broadcasted_iota(jnp.int32, (L, L), 1))
    lane_lo = lax.broadcasted_iota(jnp.int32, (L, LANES), 1) < SSM_HEADDIM
    heads_per_group = SSM_HEADS // SSM_GROUPS
    bc_off = SSM_GROUPS * SSM_STATE
    for g in range(SSM_GROUPS):
        bg_b = bc_ref[:, g * SSM_STATE:(g + 1) * SSM_STATE]
        cg_b = bc_ref[:, bc_off + g * SSM_STATE:bc_off + (g + 1) * SSM_STATE]
        cb = _dot_nt(cg_b, bg_b)
        bgt_b = bg_b.astype(F32).T.astype(BF16)
        for pair in range(heads_per_group // 2):
            p = g * (heads_per_group // 2) + pair
            xp = xs_ref[:, p * LANES:(p + 1) * LANES].astype(BF16)
            y_pair = None
            for sub in range(2):
                h = 2 * p + sub
                seg = jnp.broadcast_to(acum[:, h:h + 1], (L, L)) - col_t[h:h + 1, :]
                w = cb * jnp.exp2(jnp.where(causal, seg, -jnp.inf))
                xh = jnp.where(lane_lo if sub == 0 else jnp.logical_not(lane_lo), xp, jnp.zeros_like(xp))
                yh = _dot(w.astype(BF16), xh)
                y_pair = yh if y_pair is None else y_pair + yh
            y_ref[:, p * LANES:(p + 1) * LANES] = y_pair
        sq = jnp.zeros((L, 1), F32)
        for j in range(GROUP_WIDTH // SSD_STRIP):
            ls = slice(j * SSD_STRIP, (j + 1) * SSD_STRIP)
            cs = slice(g * GROUP_WIDTH + j * SSD_STRIP, g * GROUP_WIDTH + (j + 1) * SSD_STRIP)
            xs_j = xs_ref[:, cs]
            st_j = state_ref[g, :, ls]
            y_j = y_ref[:, cs] + _dot(cg_b, st_j.astype(BF16)) * _dot(e1_b, r_ref[:, cs]) + xs_j * dx_ref[:, cs]
            y_j = y_j * zs_ref[0, :, cs].astype(F32)
            sq = sq + jnp.sum(y_j * y_j, axis=-1, keepdims=True)
            y_ref[:, cs] = y_j
            upd = _dot(bgt_b, (xs_j * _dot(e2_b, r_ref[:, cs])).astype(BF16))
            state_ref[g, :, ls] = st_j * decay_x[:, cs] + upd
        scale = lax.rsqrt(sq * (1.0 / GROUP_WIDTH) + EPS)
        for j in range(GROUP_WIDTH // SSD_STRIP):
            cs = slice(g * GROUP_WIDTH + j * SSD_STRIP, g * GROUP_WIDTH + (j + 1) * SSD_STRIP)
            o_ref[0, :, cs] = (y_ref[:, cs] * scale * nw_ref[:, cs]).astype(o_ref.dtype)


def _ssd(xbc, zs, dt, conv_w, conv_b, dt_bias, a_log, d_skip, ssm_norm_w):
    bsz, s, _ = xbc.shape
    pad = DT_PAD - SSM_HEADS
    dtb = jnp.pad(dt_bias.astype(F32), (0, pad)).reshape(1, DT_PAD)
    a_neg = jnp.pad(-jnp.exp(a_log.astype(F32)), (0, pad)).reshape(1, DT_PAD)
    dx = jnp.repeat(d_skip.astype(F32), SSM_HEADDIM).reshape(1, SSM_D_INNER)
    expand = (jnp.arange(DT_PAD)[:, None] == (jnp.arange(SSM_D_INNER)[None, :] // SSM_HEADDIM)).astype(BF16)
    tril = (jnp.arange(CHUNK)[:, None] >= jnp.arange(CHUNK)[None, :]).astype(BF16)
    shift = jnp.stack([(jnp.arange(CHUNK)[:, None] - d == jnp.arange(CHUNK)[None, :]).astype(BF16)
                       for d in range(1, SSM_CONV)])

    def seq(width):
        return pl.BlockSpec((1, CHUNK, width), lambda b, c: (b, c, 0))

    def const(shape):
        return pl.BlockSpec(shape, lambda b, c: (0,) * len(shape))

    return pl.pallas_call(
        _ssd_kernel,
        grid=(bsz, s // CHUNK),
        in_specs=[seq(SSM_CONV_DIM), seq(SSM_D_INNER), seq(DT_PAD),
                  const((SSM_CONV, SSM_CONV_DIM)), const((1, SSM_CONV_DIM)), const((1, DT_PAD)), const((1, DT_PAD)),
                  const((1, SSM_D_INNER)), const((1, SSM_D_INNER)), const((DT_PAD, SSM_D_INNER)),
                  const((CHUNK, CHUNK)), const((SSM_CONV - 1, CHUNK, CHUNK))],
        out_specs=seq(SSM_D_INNER),
        out_shape=jax.ShapeDtypeStruct((bsz, s, SSM_D_INNER), BF16),
        scratch_shapes=[pltpu.VMEM((SUBLANES, SSM_CONV_DIM), F32),
                        pltpu.VMEM((SSM_GROUPS, SSM_STATE, GROUP_WIDTH), F32),
                        pltpu.VMEM((CHUNK, SSM_D_INNER), F32),
                        pltpu.VMEM((CHUNK, 2 * SSM_GROUPS * SSM_STATE), BF16),
                        pltpu.VMEM((CHUNK, SSM_D_INNER), F32)],
        compiler_params=_params(("parallel", "arbitrary")),
        name="ssd",
    )(xbc, zs, dt, conv_w.astype(F32), conv_b.astype(F32).reshape(1, SSM_CONV_DIM), dtb, a_neg, dx,
      ssm_norm_w.astype(F32).reshape(1, SSM_D_INNER), expand, tril, shift)


def _ret_consts():
    lg = np.log1p(-(2.0 ** (-5.0 - np.arange(RET_HEADS, dtype=np.float64))))
    idx = np.arange(CHUNK, dtype=np.float64)
    rel = idx[:, None] - idx[None, :]
    intra = np.where(rel >= 0, np.exp(np.maximum(rel, 0.0)[None] * lg[:, None, None]), 0.0)
    qd = np.exp((idx + 1.0)[None, :] * lg[:, None])
    kd = np.exp((CHUNK - 1.0 - idx)[None, :] * lg[:, None]) * (RET_QK_DIM ** -0.5)
    qd = np.broadcast_to(qd[:, :, None], (RET_HEADS, CHUNK, RET_QK_DIM))
    kd = np.broadcast_to(kd[:, :, None], (RET_HEADS, CHUNK, RET_QK_DIM))
    chunk_decay = tuple(float(v) for v in np.exp(CHUNK * lg))
    return (jnp.asarray(intra, F32), jnp.asarray(qd, F32), jnp.asarray(kd, F32), chunk_decay)


def _ret_kernel(q_ref, k_ref, v_ref, gs_ref, cos_ref, sin_ref, intra_ref, qd_ref, kd_ref, o_ref, state_ref, *,
                chunk_decay):
    @pl.when(pl.program_id(1) == 0)
    def _():
        state_ref[...] = jnp.zeros_like(state_ref)

    cos = cos_ref[0]
    sin = sin_ref[0]

    def rot(t):
        t1, t2 = t[:, :ROPE_HALF], t[:, ROPE_HALF:]
        return jnp.concatenate([t1 * cos - t2 * sin, t1 * sin + t2 * cos], axis=1)

    for h in range(RET_HEADS):
        qs = slice(h * RET_QK_DIM, (h + 1) * RET_QK_DIM)
        vs = slice(h * RET_V_DIM, (h + 1) * RET_V_DIM)
        qr = rot(q_ref[0, :, qs].astype(F32))
        kr = rot(k_ref[0, :, qs].astype(F32))
        vh = v_ref[0, :, vs]
        s = _dot_nt(qr.astype(BF16), (kr * (RET_QK_DIM ** -0.5)).astype(BF16)) * intra_ref[h]
        st = state_ref[h]
        y = _dot(s.astype(BF16), vh) + _dot((qr * qd_ref[h]).astype(BF16), st.astype(BF16))
        state_ref[h] = st * chunk_decay[h] + _dot((kr * kd_ref[h]).T.astype(BF16), vh)
        mu = jnp.mean(y, axis=-1, keepdims=True)
        yc = y - mu
        var = jnp.mean(yc * yc, axis=-1, keepdims=True)
        o_ref[0, :, vs] = (gs_ref[0, :, vs].astype(F32) * (yc * lax.rsqrt(var + EPS))).astype(o_ref.dtype)


def _ret(q, k, v, g, cos, sin):
    bsz, s, _ = q.shape
    intra, qd, kd, chunk_decay = _ret_consts()

    def seq(width):
        return pl.BlockSpec((1, CHUNK, width), lambda b, c: (b, c, 0))

    def const(shape):
        return pl.BlockSpec(shape, lambda b, c: (0,) * len(shape))

    return pl.pallas_call(
        functools.partial(_ret_kernel, chunk_decay=chunk_decay),
        grid=(bsz, s // CHUNK),
        in_specs=[seq(RET_QK_WIDTH), seq(RET_QK_WIDTH), seq(RET_V_WIDTH), seq(RET_V_WIDTH),
                  seq(ROPE_HALF), seq(ROPE_HALF),
                  const(intra.shape), const(qd.shape), const(kd.shape)],
        out_specs=seq(RET_V_WIDTH),
        out_shape=jax.ShapeDtypeStruct((bsz, s, RET_V_WIDTH), BF16),
        scratch_shapes=[pltpu.VMEM((RET_HEADS, RET_QK_DIM, RET_V_DIM), F32)],
        compiler_params=_params(("parallel", "arbitrary")),
        name="ret",
    )(q, k, v, g, cos, sin, intra, qd, kd)


def _merge_kernel(ys_ref, yr_ref, gs_ref, gr_ref, x_ref, g1_ref, sc2_ref, sh2_ref, n2w_ref,
                  wso_ref, wro_ref, wo_ref, wrh_ref, wrl_ref, x1_ref, h2_ref, sct_ref):
    y_ssm = _dot(ys_ref[0], wso_ref[...])
    y_ret = _dot(yr_ref[0], wro_ref[...])
    merged = _sigmoid(gs_ref[0].astype(F32)) * y_ssm + _sigmoid(gr_ref[0].astype(F32)) * y_ret
    x1 = x_ref[0] + g1_ref[0] * _dot(merged.astype(BF16), wo_ref[...])
    x1_ref[0] = x1
    h2 = _rms_mod(x1, n2w_ref[...], sc2_ref[0], sh2_ref[0])
    h_hi = h2.astype(BF16)
    h2_ref[0] = _pack_rows(h2)
    h_lo = (h2 - h_hi.astype(F32)).astype(BF16)
    wrh = wrh_ref[...]
    logits_t = _dot_nt(wrh, h_hi) + _dot_nt(wrh, h_lo) + _dot_nt(wrl_ref[...], h_hi)
    sct_ref[...] = _sigmoid(logits_t)


def _merge(ys, yr, gs, gr, x, gate1, scale2, shift2, norm2_w, w_ssm_out, w_ret_out, w_out, w_router, tm):
    bsz, s, d = x.shape
    nt = s // tm
    wrt = w_router.astype(F32).T
    wrh = wrt.astype(BF16)
    wrl = (wrt - wrh.astype(F32)).astype(BF16)

    def seq(width):
        return pl.BlockSpec((1, tm, width), lambda b, i: (b, i, 0))

    def const(shape):
        return pl.BlockSpec(shape, lambda b, i: (0,) * len(shape))

    vec = pl.BlockSpec((1, 1, d), lambda b, i: (b, 0, 0))
    return pl.pallas_call(
        _merge_kernel,
        grid=(bsz, nt),
        in_specs=[seq(SSM_D_INNER), seq(RET_V_WIDTH), seq(d), seq(d), seq(d), vec, vec, vec, const((1, d)),
                  const((SSM_D_INNER, d)), const((RET_V_WIDTH, d)), const((d, d)),
                  const((N_EXPERTS, d)), const((N_EXPERTS, d))],
        out_specs=[seq(d), seq(PACKED), pl.BlockSpec((N_EXPERTS, tm), lambda b, i: (0, b * nt + i))],
        out_shape=[jax.ShapeDtypeStruct((bsz, s, d), F32), jax.ShapeDtypeStruct((bsz, s, PACKED), jnp.int32),
                   jax.ShapeDtypeStruct((N_EXPERTS, bsz * s), F32)],
        compiler_params=_params(("parallel", "parallel")),
        name="merge",
    )(ys, yr, gs, gr, x, gate1, scale2, shift2, norm2_w.astype(F32).reshape(1, d),
      w_ssm_out.astype(BF16), w_ret_out.astype(BF16), w_out.astype(BF16), wrh, wrl)


def _route_kernel(sc_ref, bias_ref, upper_ref, ones_ref, idx_ref, wts_ref, rank_ref, cnt_ref, carry_ref):
    @pl.when(pl.program_id(0) == 0)
    def _():
        carry_ref[...] = jnp.zeros_like(carry_ref)

    tl = sc_ref.shape[1]
    neg = -jnp.inf
    scores = sc_ref[...]
    sel = scores + bias_ref[...]

    def first_argmax(vals, iota, n):
        m = jnp.max(vals, axis=0, keepdims=True)
        return jnp.min(jnp.where(vals == m, iota, float(n)), axis=0, keepdims=True), m

    io_g = lax.broadcasted_iota(jnp.int32, (ROUTE_GROUP_SIZE, tl), 0).astype(F32)
    rows = []
    for g in range(N_ROUTE_GROUPS):
        blk = sel[g * ROUTE_GROUP_SIZE:(g + 1) * ROUTE_GROUP_SIZE]
        i1, m1 = first_argmax(blk, io_g, ROUTE_GROUP_SIZE)
        m2 = jnp.max(jnp.where(io_g == i1, neg, blk), axis=0, keepdims=True)
        rows.append(m1 + m2)
    gsc = jnp.concatenate(rows, axis=0)
    io_8 = lax.broadcasted_iota(jnp.int32, (N_ROUTE_GROUPS, tl), 0).astype(F32)
    chosen = jnp.zeros((N_ROUTE_GROUPS, tl), F32)
    for _ in range(TOPK_ROUTE_GROUPS):
        i, _m = first_argmax(gsc, io_8, N_ROUTE_GROUPS)
        hit = io_8 == i
        chosen = jnp.where(hit, 1.0, chosen)
        gsc = jnp.where(hit, neg, gsc)
    msel = jnp.concatenate(
        [jnp.where(chosen[g:g + 1] > 0.5, sel[g * ROUTE_GROUP_SIZE:(g + 1) * ROUTE_GROUP_SIZE], neg)
         for g in range(N_ROUTE_GROUPS)], axis=0)

    io_e = lax.broadcasted_iota(jnp.int32, (N_EXPERTS, tl), 0).astype(F32)
    multi = jnp.zeros((N_EXPERTS, tl), F32)
    idx_rows, sc_rows = [], []
    for _ in range(TOP_K):
        i, _m = first_argmax(msel, io_e, N_EXPERTS)
        hit = io_e == i
        sc_rows.append(jnp.sum(jnp.where(hit, scores, 0.0), axis=0, keepdims=True))
        msel = jnp.where(hit, neg, msel)
        multi = jnp.where(hit, 1.0, multi)
        idx_rows.append(i)
    den = sc_rows[0]
    for r in sc_rows[1:]:
        den = den + r
    den = den + 1e-20
    wts_ref[...] = jnp.concatenate([r / den * ROUTED_SCALE for r in sc_rows], axis=0)
    idx_ref[...] = jnp.concatenate(idx_rows, axis=0).astype(jnp.int32)

    multi_b = multi.astype(BF16)
    carry = carry_ref[...]
    rank_full = _dot(multi_b, upper_ref[...]) + carry
    rank_ref[...] = jnp.concatenate(
        [jnp.sum(jnp.where(io_e == i, rank_full, 0.0), axis=0, keepdims=True) for i in idx_rows],
        axis=0).astype(jnp.int32)
    carry = carry + _dot(multi_b, ones_ref[...])
    carry_ref[...] = carry
    cnt_ref[...] = carry[:, :LANES]


def _route(scores_t, router_bias):
    e, t = scores_t.shape
    tl = min(ROUTE_TILE, t)
    bias = jnp.broadcast_to(router_bias.astype(F32).reshape(e, 1), (e, tl))
    upper = (jnp.arange(tl)[:, None] < jnp.arange(tl)[None, :]).astype(BF16)
    ones = jnp.ones((tl, tl), BF16)
    tok = pl.BlockSpec((TOP_K, tl), lambda i: (0, i))

    def const(shape):
        return pl.BlockSpec(shape, lambda i: (0,) * len(shape))

    return pl.pallas_call(
        _route_kernel,
        grid=(t // tl,),
        in_specs=[pl.BlockSpec((e, tl), lambda i: (0, i)), const((e, tl)), const((tl, tl)), const((tl, tl))],
        out_specs=[tok, tok, tok, const((e, LANES))],
        out_shape=[jax.ShapeDtypeStruct((TOP_K, t), jnp.int32), jax.ShapeDtypeStruct((TOP_K, t), F32),
                   jax.ShapeDtypeStruct((TOP_K, t), jnp.int32), jax.ShapeDtypeStruct((e, LANES), F32)],
        scratch_shapes=[pltpu.VMEM((e, tl), F32)],
        compiler_params=_params(("arbitrary",)),
        name="route",
    )(scores_t, bias, upper, ones)


def _dest_kernel(idx_ref, rank_ref, start_ref, o_ref):
    tl = idx_ref.shape[1]
    io_e = lax.broadcasted_iota(jnp.int32, (N_EXPERTS, tl), 0).astype(F32)
    idx = idx_ref[...].astype(F32)
    start = start_ref[...]
    rows = [jnp.sum(jnp.where(io_e == idx[kk:kk + 1], start, 0.0), axis=0, keepdims=True) for kk in range(TOP_K)]
    o_ref[...] = jnp.concatenate(rows, axis=0).astype(jnp.int32) + rank_ref[...]


def _dest(idx_t, rank_t, padded_start):
    k, t = idx_t.shape
    tl = min(2 * ROUTE_TILE, t)
    start = jnp.broadcast_to(padded_start.astype(F32).reshape(N_EXPERTS, 1), (N_EXPERTS, tl))
    tok = pl.BlockSpec((k, tl), lambda i: (0, i))
    return pl.pallas_call(
        _dest_kernel,
        grid=(t // tl,),
        in_specs=[tok, tok, pl.BlockSpec((N_EXPERTS, tl), lambda i: (0, 0))],
        out_specs=tok,
        out_shape=jax.ShapeDtypeStruct((k, t), jnp.int32),
        compiler_params=_params(("parallel",)),
        name="dest",
    )(idx_t, rank_t, start)


def _expert_kernel(be_ref, nv_ref, slot_ref, nxt_ref, x_ref, wg_hbm, wu_hbm, wd_hbm, o_ref,
                   wg_buf, wu_buf, wd_buf, wgb_ref, wub_ref, wdb_ref, sem):
    b = pl.program_id(0)
    nv = nv_ref[b]

    def weight_copies(expert, slot):
        return (pltpu.make_async_copy(wg_hbm.at[expert], wg_buf.at[slot], sem.at[slot, 0]),
                pltpu.make_async_copy(wu_hbm.at[expert], wu_buf.at[slot], sem.at[slot, 1]),
                pltpu.make_async_copy(wd_hbm.at[expert], wd_buf.at[slot], sem.at[slot, 2]))

    @pl.when(b == 0)
    def _():
        for cp in weight_copies(be_ref[0], slot_ref[0]):
            cp.start()

    @pl.when(jnp.logical_or(b == 0, be_ref[b] != be_ref[jnp.maximum(b - 1, 0)]))
    def _():
        slot = slot_ref[b]
        for cp in weight_copies(be_ref[b], slot):
            cp.wait()
        nxt = nxt_ref[b]

        @pl.when(nxt >= 0)
        def _():
            for cp in weight_copies(nxt, 1 - slot):
                cp.start()

        wgb_ref[...] = wg_buf[slot].astype(BF16)
        wub_ref[...] = wu_buf[slot].astype(BF16)
        wdb_ref[...] = wd_buf[slot].astype(BF16)

    def swiglu(rows):
        row = lax.broadcasted_iota(jnp.int32, (rows, x_ref.shape[1]), 0)
        x = _unpack_rows(jnp.where(row < nv, x_ref[0:rows, :], 0))
        g = _dot(x, wgb_ref[...])
        u = _dot(x, wub_ref[...])
        o_ref[0:rows, :] = _pack_rows(_dot((_silu(g) * u).astype(BF16), wdb_ref[...]))

    half = EXPERT_BLOCK // 2

    @pl.when(nv > half)
    def _():
        swiglu(EXPERT_BLOCK)

    @pl.when(jnp.logical_and(nv > 0, nv <= half))
    def _():
        swiglu(half)
        o_ref[half:, :] = jnp.zeros((EXPERT_BLOCK - half, o_ref.shape[1]), o_ref.dtype)

    @pl.when(nv == 0)
    def _():
        o_ref[...] = jnp.zeros_like(o_ref)


def _experts(block_expert, block_valid, xs, w_gate, w_up, w_down):
    n_slots = xs.shape[0]
    n_blocks = n_slots // EXPERT_BLOCK
    d, f = w_gate.shape[-2:]
    ids = jnp.arange(n_blocks, dtype=jnp.int32)
    change = jnp.concatenate([jnp.ones((1,), bool), block_expert[1:] != block_expert[:-1]])
    run_slot = ((jnp.cumsum(change.astype(jnp.int32)) - 1) % 2).astype(jnp.int32)
    change_at = jnp.where(change, ids, n_blocks)
    next_change = jnp.concatenate([lax.cummin(change_at[::-1])[::-1][1:], jnp.full((1,), n_blocks, jnp.int32)])
    next_expert = jnp.where(next_change < n_blocks, block_expert[jnp.minimum(next_change, n_blocks - 1)], -1)
    hbm = pl.BlockSpec(memory_space=pl.ANY)
    grid_spec = pltpu.PrefetchScalarGridSpec(
        num_scalar_prefetch=4,
        grid=(n_blocks,),
        in_specs=[pl.BlockSpec((EXPERT_BLOCK, PACKED), lambda b, be, nv, sl, nx: (jnp.where(nv[b] > 0, b, 0), 0)),
                  hbm, hbm, hbm],
        out_specs=pl.BlockSpec((EXPERT_BLOCK, PACKED), lambda b, be, nv, sl, nx: (b, 0)),
        scratch_shapes=[pltpu.VMEM((2, d, f), w_gate.dtype), pltpu.VMEM((2, d, f), w_up.dtype),
                        pltpu.VMEM((2, f, d), w_down.dtype),
                        pltpu.VMEM((d, f), BF16), pltpu.VMEM((d, f), BF16), pltpu.VMEM((f, d), BF16),
                        pltpu.SemaphoreType.DMA((2, 3))],
    )
    return pl.pallas_call(
        _expert_kernel,
        grid_spec=grid_spec,
        out_shape=jax.ShapeDtypeStruct((n_slots, PACKED), jnp.int32),
        compiler_params=_params(("arbitrary",)),
        name="experts",
    )(block_expert, block_valid, run_slot, next_expert.astype(jnp.int32), xs, w_gate, w_up, w_down)


def _final_kernel(x1_ref, h2_ref, yk_ref, wts_ref, g2_ref, wsg_ref, wsu_ref, wsd_ref, fnw_ref, o_ref):
    h = _unpack_rows(h2_ref[0])
    act = (_silu(_dot(h, wsg_ref[...])) * _dot(h, wsu_ref[...])).astype(BF16)
    moe = _dot(act, wsd_ref[...])
    wts = wts_ref[...]
    for kk in range(TOP_K):
        moe = moe + _unpack_rows(yk_ref[kk]).astype(F32) * wts[:, kk:kk + 1]
    x2 = x1_ref[0] + g2_ref[0] * moe
    ms = jnp.mean(x2 * x2, axis=-1, keepdims=True)
    o_ref[0] = x2 * lax.rsqrt(ms + EPS) * fnw_ref[...]


def _final(x1, h2p, yk, wts, gate2, w_sh_gate, w_sh_up, w_sh_down, final_norm_w, tm):
    bsz, s, d = x1.shape
    nt = s // tm
    f = w_sh_gate.shape[-1]
    seq = pl.BlockSpec((1, tm, d), lambda b, i: (b, i, 0))

    def const(shape):
        return pl.BlockSpec(shape, lambda b, i: (0,) * len(shape))

    return pl.pallas_call(
        _final_kernel,
        grid=(bsz, nt),
        in_specs=[seq, pl.BlockSpec((1, tm, PACKED), lambda b, i: (b, i, 0)),
                  pl.BlockSpec((TOP_K, tm, PACKED), lambda b, i: (0, b * nt + i, 0)),
                  pl.BlockSpec((tm, TOP_K), lambda b, i: (b * nt + i, 0)),
                  pl.BlockSpec((1, 1, d), lambda b, i: (b, 0, 0)),
                  const((d, f)), const((d, f)), const((f, d)), const((1, d))],
        out_specs=seq,
        out_shape=jax.ShapeDtypeStruct((bsz, s, d), F32),
        compiler_params=_params(("parallel", "parallel")),
        name="final",
    )(x1, h2p, yk, wts, gate2, w_sh_gate.astype(BF16), w_sh_up.astype(BF16), w_sh_down.astype(BF16),
      final_norm_w.astype(F32).reshape(1, d))


def _gather_rows(table, idx):
    n = idx.shape[0]
    width = table.shape[1]
    workers = SC_CORES * SC_SUBCORES
    per_worker = n // workers
    steps = per_worker // SC_WINDOW
    assert per_worker * workers == n and steps * SC_WINDOW == per_worker and steps % 2 == 0
    mesh = plsc.VectorSubcoreMesh(core_axis_name="c", subcore_axis_name="s")
    win = lambda dtype, *shape: pltpu.VMEM(shape, dtype)

    @functools.partial(
        pl.kernel, mesh=mesh,
        out_type=jax.ShapeDtypeStruct((n, width), table.dtype),
        scratch_types=[win(jnp.int32, SC_WINDOW), win(jnp.int32, SC_WINDOW),
                       win(table.dtype, SC_WINDOW, width), win(table.dtype, SC_WINDOW, width),
                       pltpu.SemaphoreType.DMA, pltpu.SemaphoreType.DMA,
                       pltpu.SemaphoreType.DMA, pltpu.SemaphoreType.DMA],
        name="gather_rows",
    )
    def run(table_hbm, idx_hbm, out_hbm, idx_a, idx_b, rows_a, rows_b, gsem_a, gsem_b, osem_a, osem_b):
        base = (lax.axis_index("s") * SC_CORES + lax.axis_index("c")) * per_worker

        @pl.loop(0, steps, step=2)
        def _(i):
            off_a = pl.multiple_of(base + i * SC_WINDOW, SC_WINDOW)
            off_b = pl.multiple_of(off_a + SC_WINDOW, SC_WINDOW)
            pltpu.sync_copy(idx_hbm.at[pl.ds(off_a, SC_WINDOW)], idx_a)
            gather_a = pltpu.async_copy(table_hbm.at[idx_a], rows_a, gsem_a)
            pltpu.sync_copy(idx_hbm.at[pl.ds(off_b, SC_WINDOW)], idx_b)
            gather_b = pltpu.async_copy(table_hbm.at[idx_b], rows_b, gsem_b)
            gather_a.wait()
            store_a = pltpu.async_copy(rows_a, out_hbm.at[pl.ds(off_a, SC_WINDOW)], osem_a)
            gather_b.wait()
            store_b = pltpu.async_copy(rows_b, out_hbm.at[pl.ds(off_b, SC_WINDOW)], osem_b)
            store_a.wait()
            store_b.wait()

    return run(table, idx)


def _scatter_rows(rows, dest_kt, n_slots):
    t, width = rows.shape
    top_k = dest_kt.shape[0]
    workers = SC_CORES * SC_SUBCORES
    window = LANES
    per_worker = t // workers
    steps = per_worker // window
    assert per_worker * workers == t and steps * window == per_worker
    mesh = plsc.VectorSubcoreMesh(core_axis_name="c", subcore_axis_name="s")

    @functools.partial(
        pl.kernel, mesh=mesh,
        out_type=jax.ShapeDtypeStruct((n_slots, width), rows.dtype),
        scratch_types=[pltpu.VMEM((top_k, window), jnp.int32),
                       pltpu.VMEM((window, width), rows.dtype),
                       pltpu.SemaphoreType.DMA],
        name="scatter_rows",
    )
    def run(rows_hbm, dest_hbm, out_hbm, idx_v, rows_v, sem):
        base = (lax.axis_index("s") * SC_CORES + lax.axis_index("c")) * per_worker

        @pl.loop(0, steps)
        def _(i):
            off = pl.multiple_of(base + i * window, window)
            pltpu.sync_copy(rows_hbm.at[pl.ds(off, window)], rows_v)
            pltpu.sync_copy(dest_hbm.at[:, pl.ds(off, window)], idx_v)
            copies = [pltpu.async_copy(rows_v, out_hbm.at[idx_v.at[kk]], sem) for kk in range(top_k)]
            for cp in copies:
                cp.wait()

    return run(rows, dest_kt)


def _layer(x, c, positions, w_ada, b_ada, norm1_w, w_in, conv_w, conv_b, dt_bias, a_log, d_skip, ssm_norm_w,
           w_ssm_out, w_ret_out, w_out, norm2_w, w_router, router_bias, w_exp_gate, w_exp_up, w_exp_down,
           w_sh_gate, w_sh_up, w_sh_down):
    bsz, s, d = x.shape
    t = bsz * s
    tm = min(TOKEN_TILE, s)

    mod = _mod(c, w_ada, b_ada)
    shift1, scale1, gate1, shift2, scale2, gate2 = (mod[:, i * d:(i + 1) * d].reshape(bsz, 1, d) for i in range(6))

    bounds = np.cumsum((0, SSM_D_INNER, SSM_CONV_DIM, SSM_HEADS, RET_QK_WIDTH, RET_QK_WIDTH, RET_V_WIDTH,
                        RET_V_WIDTH, d, d))
    wz, wxbc, wdt, wq, wk, wv, wg, wgs, wgr = (w_in[:, bounds[i]:bounds[i + 1]] for i in range(9))
    nw1 = norm1_w.astype(F32).reshape(1, d)
    w_a = jnp.concatenate([wxbc, wz, jnp.pad(wdt, ((0, 0), (0, DT_PAD - SSM_HEADS)))], axis=1).astype(BF16)
    w_b = jnp.concatenate([wq, wk, wv, wg, wgs, wgr], axis=1).astype(BF16)
    h, xbc, zs, dt = _inproj(_inproj_ssd_kernel, x, (nw1, scale1, shift1), w_a,
                             (d, SSM_CONV_DIM, SSM_D_INNER, DT_PAD), (BF16, BF16, BF16, F32), tm)
    q, k, v, g, gs, gr = _inproj(_inproj_ret_kernel, h, (), w_b,
                                 (RET_QK_WIDTH, RET_QK_WIDTH, RET_V_WIDTH, RET_V_WIDTH, d, d), (BF16,) * 6, tm)

    ys = _ssd(xbc, zs, dt, conv_w, conv_b, dt_bias, a_log, d_skip, ssm_norm_w)
    cos, sin = _rope(positions, tm)
    yr = _ret(q, k, v, g, cos, sin)
    x1, h2, scores_t = _merge(ys, yr, gs, gr, x, gate1, scale2, shift2, norm2_w, w_ssm_out, w_ret_out, w_out,
                              w_router, tm)

    idx_t, wts_t, rank_t, cnt = _route(scores_t, router_bias)

    counts = cnt[:, 0].astype(jnp.int32)
    padded = (counts + EXPERT_BLOCK - 1) // EXPERT_BLOCK * EXPERT_BLOCK
    padded_end = jnp.cumsum(padded)
    padded_start = padded_end - padded
    n_blocks = -(-(t * TOP_K) // EXPERT_BLOCK) + N_EXPERTS
    n_slots = n_blocks * EXPERT_BLOCK
    dest_t = _dest(idx_t, rank_t, padded_start)
    block_first = jnp.arange(n_blocks, dtype=jnp.int32) * EXPERT_BLOCK
    block_expert = jnp.minimum(jnp.sum(padded_end[None, :] <= block_first[:, None], axis=1), N_EXPERTS - 1)
    owner = (block_expert[:, None] == jnp.arange(N_EXPERTS, dtype=jnp.int32)[None, :]).astype(jnp.int32)
    region_end = jnp.sum(owner * (padded_start + counts)[None, :], axis=1)
    block_valid = jnp.clip(region_end - block_first, 0, EXPERT_BLOCK).astype(jnp.int32)
    xs = _scatter_rows(h2.reshape(t, PACKED), dest_t, n_slots)
    y_sorted = _experts(block_expert.astype(jnp.int32), block_valid, xs, w_exp_gate, w_exp_up, w_exp_down)
    yk = _gather_rows(y_sorted, dest_t.reshape(-1)).reshape(TOP_K, t, PACKED)
    return x1, h2, yk, wts_t.T, gate2


def kernel(x, c, positions, w_ada, b_ada, norm1_w, w_in, conv_w, conv_b, dt_bias, a_log, d_skip, ssm_norm_w,
           w_ssm_out, w_ret_out, w_out, norm2_w, w_router, router_bias, w_exp_gate, w_exp_up, w_exp_down,
           w_sh_gate, w_sh_up, w_sh_down, final_norm_w):
    assert w_ada.shape[0] == 1, "the final rmsnorm is fused into the single layer's last kernel"
    tm = min(TOKEN_TILE, x.shape[1])
    x1, h2, yk, wts, gate2 = _layer(
        x, c, positions, w_ada[0], b_ada[0], norm1_w[0], w_in[0], conv_w[0], conv_b[0], dt_bias[0], a_log[0],
        d_skip[0], ssm_norm_w[0], w_ssm_out[0], w_ret_out[0], w_out[0], norm2_w[0], w_router[0],
        router_bias[0], w_exp_gate[0], w_exp_up[0], w_exp_down[0], w_sh_gate[0], w_sh_up[0], w_sh_down[0])
    return _final(x1, h2, yk, wts, gate2, w_sh_gate[0], w_sh_up[0], w_sh_down[0], final_norm_w, tm)
```

```python
import functools

import numpy as np
import jax
import jax.numpy as jnp
from jax import lax
from jax.experimental import pallas as pl
from jax.experimental.pallas import tpu as pltpu
from jax.experimental.pallas import tpu_sc as plsc

F32 = jnp.float32
BF16 = jnp.bfloat16

EPS = 1e-6
D_MODEL = 1024
SSM_D_INNER = 2048
SSM_HEADDIM = 64
SSM_HEADS = 32
SSM_GROUPS = 4
SSM_STATE = 128
SSM_CONV = 4
SSM_CONV_DIM = SSM_D_INNER + 2 * SSM_GROUPS * SSM_STATE
GROUP_WIDTH = SSM_D_INNER // SSM_GROUPS
RET_HEADS = 4
RET_QK_DIM = 256
RET_V_DIM = 512
RET_QK_WIDTH = RET_HEADS * RET_QK_DIM
RET_V_WIDTH = RET_HEADS * RET_V_DIM
ROPE_BASE = 10000.0
ROPE_HALF = RET_QK_DIM // 2
N_EXPERTS = 256
TOP_K = 8
N_ROUTE_GROUPS = 8
TOPK_ROUTE_GROUPS = 4
ROUTE_GROUP_SIZE = N_EXPERTS // N_ROUTE_GROUPS
EXPERT_DIM = 256
ROUTED_SCALE = 2.5

TOKEN_TILE = 512
CHUNK = 128
SCAN_STEP_CHUNKS = 4
SSD_STRIP = 256
LANES = 128
SUBLANES = 8
DT_PAD = LANES
EXPERT_BLOCK = 512
EXPERT_STEP_BLOCKS = 2
ROUTE_TILE = 512
VMEM_LIMIT = 56 * 1024 * 1024
PACKED = D_MODEL // 2
SC_CORES = 2
SC_SUBCORES = 16
SC_WINDOW = 64


LOG2E = 1.4426950408889634


def _silu(x):
    return x / (1.0 + jnp.exp2(x * -LOG2E))


def _sigmoid(x):
    return 1.0 / (1.0 + jnp.exp2(x * -LOG2E))


def _split3(a):
    hi = a.astype(BF16)
    r1 = a - hi.astype(F32)
    mid = r1.astype(BF16)
    lo = (r1 - mid.astype(F32)).astype(BF16)
    return hi, mid, lo


def _pack_rows(x):
    w = x.shape[1] // 2
    hi = lax.bitcast_convert_type(x[:, :w].astype(BF16).astype(F32), jnp.uint32)
    lo = lax.bitcast_convert_type(x[:, w:].astype(BF16).astype(F32), jnp.uint32)
    return lax.bitcast_convert_type(hi | (lo >> 16), jnp.int32)


def _unpack_rows(p):
    u = lax.bitcast_convert_type(p, jnp.uint32)
    hi = lax.bitcast_convert_type(u & jnp.uint32(0xFFFF0000), F32)
    lo = lax.bitcast_convert_type(u << 16, F32)
    return jnp.concatenate([hi, lo], axis=1).astype(BF16)


def _dot(a, b):
    return jnp.dot(a, b, preferred_element_type=F32)


def _dot_nt(a, b):
    return lax.dot_general(a, b, (((1,), (1,)), ((), ())), preferred_element_type=F32)


def _params(sem):
    return pltpu.CompilerParams(dimension_semantics=sem, vmem_limit_bytes=VMEM_LIMIT)


def _mod_kernel(c_ref, w_ref, b_ref, o_ref):
    o_ref[...] = _dot(_silu(c_ref[...]), w_ref[...]) + b_ref[...]


def _mod(c, w_ada, b_ada):
    bsz, d = c.shape
    n = w_ada.shape[1]
    return pl.pallas_call(
        _mod_kernel,
        grid=(n // d,),
        in_specs=[pl.BlockSpec((bsz, d), lambda j: (0, 0)),
                  pl.BlockSpec((d, d), lambda j: (0, j)),
                  pl.BlockSpec((1, d), lambda j: (0, j))],
        out_specs=pl.BlockSpec((bsz, d), lambda j: (0, j)),
        out_shape=jax.ShapeDtypeStruct((bsz, n), F32),
        compiler_params=_params(("arbitrary",)),
        name="mod",
    )(c, w_ada, b_ada.reshape(1, n))


def _rms_mod(x, nw, scale, shift):
    ms = jnp.mean(x * x, axis=-1, keepdims=True)
    return (x * lax.rsqrt(ms + EPS) * nw) * (1.0 + scale) + shift


PROJ_STRIP = 512


def _project(hb, w_ref, off, o_ref, width, act):
    for c0 in range(0, width, PROJ_STRIP):
        cw = min(PROJ_STRIP, width - c0)
        v = _dot(hb, w_ref[:, off + c0:off + c0 + cw])
        o_ref[0, :, c0:c0 + cw] = (v if act is None else act(v)).astype(o_ref.dtype)


def _inproj_ssd_kernel(x_ref, nw_ref, sc_ref, sh_ref, w_ref, h_ref, xbc_ref, zs_ref, dt_ref):
    hb = _rms_mod(x_ref[0], nw_ref[...], sc_ref[0], sh_ref[0]).astype(BF16)
    h_ref[0] = hb
    _project(hb, w_ref, 0, xbc_ref, SSM_CONV_DIM, None)
    _project(hb, w_ref, SSM_CONV_DIM, zs_ref, SSM_D_INNER, _silu)
    _project(hb, w_ref, SSM_CONV_DIM + SSM_D_INNER, dt_ref, DT_PAD, None)


def _inproj_ret_kernel(h_ref, w_ref, q_ref, k_ref, v_ref, gs_ref, gate_s_ref, gate_r_ref):
    hb = h_ref[0]
    off = 0
    for o_ref, act in ((q_ref, None), (k_ref, None), (v_ref, None), (gs_ref, _silu), (gate_s_ref, None),
                       (gate_r_ref, None)):
        width = o_ref.shape[-1]
        _project(hb, w_ref, off, o_ref, width, act)
        off += width


def _inproj(body, rows, vecs, w, widths, dtypes, tm):
    bsz, s, d = rows.shape
    vec_specs = [pl.BlockSpec((1, d), lambda b, i: (0, 0)) if v.ndim == 2 else
                 pl.BlockSpec((1, 1, d), lambda b, i: (b, 0, 0)) for v in vecs]
    return pl.pallas_call(
        body,
        grid=(bsz, s // tm),
        in_specs=[pl.BlockSpec((1, tm, d), lambda b, i: (b, i, 0))] + vec_specs
                 + [pl.BlockSpec(w.shape, lambda b, i: (0, 0))],
        out_specs=[pl.BlockSpec((1, tm, wd), lambda b, i: (b, i, 0)) for wd in widths],
        out_shape=[jax.ShapeDtypeStruct((bsz, s, wd), dt) for wd, dt in zip(widths, dtypes)],
        compiler_params=_params(("parallel", "parallel")),
        name="inproj",
    )(rows, *vecs, w)


def _rope_kernel(pos_ref, inv_ref, cos_ref, sin_ref):
    ang = pos_ref[0].astype(F32) * inv_ref[...]
    cos_ref[0] = jnp.cos(ang)
    sin_ref[0] = jnp.sin(ang)


def _rope(positions, tm):
    bsz, s = positions.shape
    inv = (1.0 / (ROPE_BASE ** (jnp.arange(ROPE_HALF, dtype=F32) / ROPE_HALF))).reshape(1, ROPE_HALF)
    spec = pl.BlockSpec((1, tm, ROPE_HALF), lambda b, i: (b, i, 0))
    return pl.pallas_call(
        _rope_kernel,
        grid=(bsz, s // tm),
        in_specs=[pl.BlockSpec((1, tm, 1), lambda b, i: (b, i, 0)),
                  pl.BlockSpec((1, ROPE_HALF), lambda b, i: (0, 0))],
        out_specs=[spec, spec],
        out_shape=[jax.ShapeDtypeStruct((bsz, s, ROPE_HALF), F32)] * 2,
        compiler_params=_params(("parallel", "parallel")),
        name="rope",
    )(positions.reshape(bsz, s, 1), inv)


def _ssd_kernel(xbc_ref, zs_ref, dt_ref, cw_ref, cb_ref, dtb_ref, a_ref, dx_ref, nw_ref, r_ref, tril_ref, shift_ref,
                o_ref, tail_ref, state_ref, xs_ref, bc_ref, y_ref):
    @pl.when(pl.program_id(1) == 0)
    def _():
        tail_ref[...] = jnp.zeros_like(tail_ref)
        state_ref[...] = jnp.zeros_like(state_ref)

    for cc in range(SCAN_STEP_CHUNKS):
        rows = pl.ds(cc * CHUNK, CHUNK)
        _ssd_chunk(xbc_ref.at[:, rows, :], zs_ref.at[:, rows, :], dt_ref.at[:, rows, :], cw_ref, cb_ref, dtb_ref,
                   a_ref, dx_ref, nw_ref, r_ref, tril_ref, shift_ref, o_ref.at[:, rows, :], tail_ref, state_ref,
                   xs_ref, bc_ref, y_ref)


def _ssd_chunk(xbc_ref, zs_ref, dt_ref, cw_ref, cb_ref, dtb_ref, a_ref, dx_ref, nw_ref, r_ref, tril_ref, shift_ref,
               o_ref, tail_ref, state_ref, xs_ref, bc_ref, y_ref):
    L = CHUNK

    row8 = lax.broadcasted_iota(jnp.int32, (SUBLANES, 1), 0)
    for c0 in range(0, SSM_CONV_DIM, SSD_STRIP):
        cs = slice(c0, c0 + SSD_STRIP)
        u_b = xbc_ref[0, :, cs]
        u = u_b.astype(F32)
        tail = tail_ref[:, cs]
        acc = u * cw_ref[SSM_CONV - 1:SSM_CONV, cs] + cb_ref[:, cs]
        for d in range(1, SSM_CONV):
            sh = _dot(shift_ref[d - 1], u_b)
            head = sh[0:SUBLANES] + jnp.where(row8 < d, pltpu.roll(tail, d, axis=0), 0.0)
            sh = jnp.concatenate([head, sh[SUBLANES:]], axis=0)
            acc = acc + sh * cw_ref[SSM_CONV - 1 - d:SSM_CONV - d, cs]
        tail_ref[:, cs] = u[L - SUBLANES:L]
        act = _silu(acc)
        if c0 < SSM_D_INNER:
            xs_ref[:, cs] = act
        else:
            bc_ref[:, c0 - SSM_D_INNER:c0 - SSM_D_INNER + SSD_STRIP] = act.astype(BF16)

    dt_raw = dt_ref[0] + dtb_ref[...]
    dtv = jnp.maximum(dt_raw, 0.0) + jnp.log1p(jnp.exp(-jnp.abs(dt_raw)))
    a = dtv * (a_ref[...] * LOG2E)
    tril = tril_ref[...]
    a_hi, a_mid, a_lo = _split3(a)
    acum = _dot(tril, a_hi) + _dot(tril, a_mid) + _dot(tril, a_lo)
    col_t = (acum - jnp.log(dtv) * LOG2E).T
    last = acum[L - 1:L, :]
    e1_b = jnp.exp2(acum).astype(BF16)
    e2_b = (jnp.exp2(last - acum) * dtv).astype(BF16)
    l_hi, l_mid, l_lo = _split3(jnp.broadcast_to(jnp.exp2(last), (SUBLANES, LANES)))
    r = r_ref[...]
    decay_x = (_dot(l_hi, r) + _dot(l_mid, r) + _dot(l_lo, r))[0:1, :]

    causal = (lax.broadcasted_iota(jnp.int32, (L, L), 0) >= lax.broadcasted_iota(jnp.int32, (L, L), 1))
    lane_lo = lax.broadcasted_iota(jnp.int32, (L, LANES), 1) < SSM_HEADDIM
    heads_per_group = SSM_HEADS // SSM_GROUPS
    bc_off = SSM_GROUPS * SSM_STATE
    for g in range(SSM_GROUPS):
        bg_b = bc_ref[:, g * SSM_STATE:(g + 1) * SSM_STATE]
        cg_b = bc_ref[:, bc_off + g * SSM_STATE:bc_off + (g + 1) * SSM_STATE]
        cb = _dot_nt(cg_b, bg_b)
        bgt_b = bg_b.astype(F32).T.astype(BF16)
        for pair in range(heads_per_group // 2):
            p = g * (heads_per_group // 2) + pair
            xp = xs_ref[:, p * LANES:(p + 1) * LANES].astype(BF16)
            y_pair = None
            for sub in range(2):
                h = 2 * p + sub
                seg = jnp.broadcast_to(acum[:, h:h + 1], (L, L)) - col_t[h:h + 1, :]
                w = cb * jnp.exp2(jnp.where(causal, seg, -jnp.inf))
                xh = jnp.where(lane_lo if sub == 0 else jnp.logical_not(lane_lo), xp, jnp.zeros_like(xp))
                yh = _dot(w.astype(BF16), xh)
                y_pair = yh if y_pair is None else y_pair + yh
            y_ref[:, p * LANES:(p + 1) * LANES] = y_pair
        sq = jnp.zeros((L, 1), F32)
        for j in range(GROUP_WIDTH // SSD_STRIP):
            ls = slice(j * SSD_STRIP, (j + 1) * SSD_STRIP)
            cs = slice(g * GROUP_WIDTH + j * SSD_STRIP, g * GROUP_WIDTH + (j + 1) * SSD_STRIP)
            xs_j = xs_ref[:, cs]
            st_j = state_ref[g, :, ls]
            y_j = y_ref[:, cs] + _dot(cg_b, st_j.astype(BF16)) * _dot(e1_b, r_ref[:, cs]) + xs_j * dx_ref[:, cs]
            y_j = y_j * zs_ref[0, :, cs].astype(F32)
            sq = sq + jnp.sum(y_j * y_j, axis=-1, keepdims=True)
            y_ref[:, cs] = y_j
            upd = _dot(bgt_b, (xs_j * _dot(e2_b, r_ref[:, cs])).astype(BF16))
            state_ref[g, :, ls] = st_j * decay_x[:, cs] + upd
        scale = lax.rsqrt(sq * (1.0 / GROUP_WIDTH) + EPS)
        for j in range(GROUP_WIDTH // SSD_STRIP):
            cs = slice(g * GROUP_WIDTH + j * SSD_STRIP, g * GROUP_WIDTH + (j + 1) * SSD_STRIP)
            o_ref[0, :, cs] = (y_ref[:, cs] * scale * nw_ref[:, cs]).astype(o_ref.dtype)


def _ssd(xbc, zs, dt, conv_w, conv_b, dt_bias, a_log, d_skip, ssm_norm_w):
    bsz, s, _ = xbc.shape
    pad = DT_PAD - SSM_HEADS
    dtb = jnp.pad(dt_bias.astype(F32), (0, pad)).reshape(1, DT_PAD)
    a_neg = jnp.pad(-jnp.exp(a_log.astype(F32)), (0, pad)).reshape(1, DT_PAD)
    dx = jnp.repeat(d_skip.astype(F32), SSM_HEADDIM).reshape(1, SSM_D_INNER)
    expand = (jnp.arange(DT_PAD)[:, None] == (jnp.arange(SSM_D_INNER)[None, :] // SSM_HEADDIM)).astype(BF16)
    tril = (jnp.arange(CHUNK)[:, None] >= jnp.arange(CHUNK)[None, :]).astype(BF16)
    shift = jnp.stack([(jnp.arange(CHUNK)[:, None] - d == jnp.arange(CHUNK)[None, :]).astype(BF16)
                       for d in range(1, SSM_CONV)])

    def seq(width):
        return pl.BlockSpec((1, SCAN_STEP_CHUNKS * CHUNK, width), lambda b, c: (b, c, 0))

    def const(shape):
        return pl.BlockSpec(shape, lambda b, c: (0,) * len(shape))

    return pl.pallas_call(
        _ssd_kernel,
        grid=(bsz, s // (SCAN_STEP_CHUNKS * CHUNK)),
        in_specs=[seq(SSM_CONV_DIM), seq(SSM_D_INNER), seq(DT_PAD),
                  const((SSM_CONV, SSM_CONV_DIM)), const((1, SSM_CONV_DIM)), const((1, DT_PAD)), const((1, DT_PAD)),
                  const((1, SSM_D_INNER)), const((1, SSM_D_INNER)), const((DT_PAD, SSM_D_INNER)),
                  const((CHUNK, CHUNK)), const((SSM_CONV - 1, CHUNK, CHUNK))],
        out_specs=seq(SSM_D_INNER),
        out_shape=jax.ShapeDtypeStruct((bsz, s, SSM_D_INNER), BF16),
        scratch_shapes=[pltpu.VMEM((SUBLANES, SSM_CONV_DIM), F32),
                        pltpu.VMEM((SSM_GROUPS, SSM_STATE, GROUP_WIDTH), F32),
                        pltpu.VMEM((CHUNK, SSM_D_INNER), F32),
                        pltpu.VMEM((CHUNK, 2 * SSM_GROUPS * SSM_STATE), BF16),
                        pltpu.VMEM((CHUNK, SSM_D_INNER), F32)],
        compiler_params=_params(("parallel", "arbitrary")),
        name="ssd",
    )(xbc, zs, dt, conv_w.astype(F32), conv_b.astype(F32).reshape(1, SSM_CONV_DIM), dtb, a_neg, dx,
      ssm_norm_w.astype(F32).reshape(1, SSM_D_INNER), expand, tril, shift)


def _ret_consts():
    lg = np.log1p(-(2.0 ** (-5.0 - np.arange(RET_HEADS, dtype=np.float64))))
    idx = np.arange(CHUNK, dtype=np.float64)
    rel = idx[:, None] - idx[None, :]
    intra = np.where(rel >= 0, np.exp(np.maximum(rel, 0.0)[None] * lg[:, None, None]), 0.0)
    qd = np.exp((idx + 1.0)[None, :] * lg[:, None])
    kd = np.exp((CHUNK - 1.0 - idx)[None, :] * lg[:, None]) * (RET_QK_DIM ** -0.5)
    qd = np.broadcast_to(qd[:, :, None], (RET_HEADS, CHUNK, RET_QK_DIM))
    kd = np.broadcast_to(kd[:, :, None], (RET_HEADS, CHUNK, RET_QK_DIM))
    chunk_decay = tuple(float(v) for v in np.exp(CHUNK * lg))
    return (jnp.asarray(intra, F32), jnp.asarray(qd, F32), jnp.asarray(kd, F32), chunk_decay)


def _ret_kernel(q_ref, k_ref, v_ref, gs_ref, cos_ref, sin_ref, intra_ref, qd_ref, kd_ref, o_ref, state_ref, *,
                chunk_decay):
    @pl.when(pl.program_id(1) == 0)
    def _():
        state_ref[...] = jnp.zeros_like(state_ref)

    for cc in range(SCAN_STEP_CHUNKS):
        rows = pl.ds(cc * CHUNK, CHUNK)
        _ret_chunk(*(ref.at[:, rows, :] for ref in (q_ref, k_ref, v_ref, gs_ref, cos_ref, sin_ref)),
                   intra_ref, qd_ref, kd_ref, o_ref.at[:, rows, :], state_ref, chunk_decay)


def _ret_chunk(q_ref, k_ref, v_ref, gs_ref, cos_ref, sin_ref, intra_ref, qd_ref, kd_ref, o_ref, state_ref,
               chunk_decay):
    cos = cos_ref[0]
    sin = sin_ref[0]

    def rot(t):
        t1, t2 = t[:, :ROPE_HALF], t[:, ROPE_HALF:]
        return jnp.concatenate([t1 * cos - t2 * sin, t1 * sin + t2 * cos], axis=1)

    for h in range(RET_HEADS):
        qs = slice(h * RET_QK_DIM, (h + 1) * RET_QK_DIM)
        vs = slice(h * RET_V_DIM, (h + 1) * RET_V_DIM)
        qr = rot(q_ref[0, :, qs].astype(F32))
        kr = rot(k_ref[0, :, qs].astype(F32))
        vh = v_ref[0, :, vs]
        s = _dot_nt(qr.astype(BF16), (kr * (RET_QK_DIM ** -0.5)).astype(BF16)) * intra_ref[h]
        st = state_ref[h]
        y = _dot(s.astype(BF16), vh) + _dot((qr * qd_ref[h]).astype(BF16), st.astype(BF16))
        state_ref[h] = st * chunk_decay[h] + _dot((kr * kd_ref[h]).T.astype(BF16), vh)
        mu = jnp.mean(y, axis=-1, keepdims=True)
        yc = y - mu
        var = jnp.mean(yc * yc, axis=-1, keepdims=True)
        o_ref[0, :, vs] = (gs_ref[0, :, vs].astype(F32) * (yc * lax.rsqrt(var + EPS))).astype(o_ref.dtype)


def _ret(q, k, v, g, cos, sin):
    bsz, s, _ = q.shape
    intra, qd, kd, chunk_decay = _ret_consts()

    def seq(width):
        return pl.BlockSpec((1, SCAN_STEP_CHUNKS * CHUNK, width), lambda b, c: (b, c, 0))

    def const(shape):
        return pl.BlockSpec(shape, lambda b, c: (0,) * len(shape))

    return pl.pallas_call(
        functools.partial(_ret_kernel, chunk_decay=chunk_decay),
        grid=(bsz, s // (SCAN_STEP_CHUNKS * CHUNK)),
        in_specs=[seq(RET_QK_WIDTH), seq(RET_QK_WIDTH), seq(RET_V_WIDTH), seq(RET_V_WIDTH),
                  seq(ROPE_HALF), seq(ROPE_HALF),
                  const(intra.shape), const(qd.shape), const(kd.shape)],
        out_specs=seq(RET_V_WIDTH),
        out_shape=jax.ShapeDtypeStruct((bsz, s, RET_V_WIDTH), BF16),
        scratch_shapes=[pltpu.VMEM((RET_HEADS, RET_QK_DIM, RET_V_DIM), F32)],
        compiler_params=_params(("parallel", "arbitrary")),
        name="ret",
    )(q, k, v, g, cos, sin, intra, qd, kd)


def _merge_kernel(ys_ref, yr_ref, gs_ref, gr_ref, x_ref, g1_ref, sc2_ref, sh2_ref, n2w_ref,
                  wso_ref, wro_ref, wo_ref, wrh_ref, wrl_ref, x1_ref, h2_ref, sct_ref):
    y_ssm = _dot(ys_ref[0], wso_ref[...])
    y_ret = _dot(yr_ref[0], wro_ref[...])
    merged = _sigmoid(gs_ref[0].astype(F32)) * y_ssm + _sigmoid(gr_ref[0].astype(F32)) * y_ret
    x1 = x_ref[0] + g1_ref[0] * _dot(merged.astype(BF16), wo_ref[...])
    x1_ref[0] = x1
    h2 = _rms_mod(x1, n2w_ref[...], sc2_ref[0], sh2_ref[0])
    h_hi = h2.astype(BF16)
    h2_ref[0] = _pack_rows(h2)
    h_lo = (h2 - h_hi.astype(F32)).astype(BF16)
    wrh = wrh_ref[...]
    logits_t = _dot_nt(wrh, h_hi) + _dot_nt(wrh, h_lo) + _dot_nt(wrl_ref[...], h_hi)
    sct_ref[...] = _sigmoid(logits_t)


def _merge(ys, yr, gs, gr, x, gate1, scale2, shift2, norm2_w, w_ssm_out, w_ret_out, w_out, w_router, tm):
    bsz, s, d = x.shape
    nt = s // tm
    wrt = w_router.astype(F32).T
    wrh = wrt.astype(BF16)
    wrl = (wrt - wrh.astype(F32)).astype(BF16)

    def seq(width):
        return pl.BlockSpec((1, tm, width), lambda b, i: (b, i, 0))

    def const(shape):
        return pl.BlockSpec(shape, lambda b, i: (0,) * len(shape))

    vec = pl.BlockSpec((1, 1, d), lambda b, i: (b, 0, 0))
    return pl.pallas_call(
        _merge_kernel,
        grid=(bsz, nt),
        in_specs=[seq(SSM_D_INNER), seq(RET_V_WIDTH), seq(d), seq(d), seq(d), vec, vec, vec, const((1, d)),
                  const((SSM_D_INNER, d)), const((RET_V_WIDTH, d)), const((d, d)),
                  const((N_EXPERTS, d)), const((N_EXPERTS, d))],
        out_specs=[seq(d), seq(PACKED), pl.BlockSpec((N_EXPERTS, tm), lambda b, i: (0, b * nt + i))],
        out_shape=[jax.ShapeDtypeStruct((bsz, s, d), F32), jax.ShapeDtypeStruct((bsz, s, PACKED), jnp.int32),
                   jax.ShapeDtypeStruct((N_EXPERTS, bsz * s), F32)],
        compiler_params=_params(("parallel", "parallel")),
        name="merge",
    )(ys, yr, gs, gr, x, gate1, scale2, shift2, norm2_w.astype(F32).reshape(1, d),
      w_ssm_out.astype(BF16), w_ret_out.astype(BF16), w_out.astype(BF16), wrh, wrl)


def _route_kernel(sc_ref, bias_ref, upper_ref, ones_ref, idx_ref, wts_ref, rank_ref, cnt_ref, carry_ref):
    @pl.when(pl.program_id(0) == 0)
    def _():
        carry_ref[...] = jnp.zeros_like(carry_ref)

    tl = sc_ref.shape[1]
    neg = -jnp.inf
    scores = sc_ref[...]
    sel = scores + bias_ref[...]

    def first_argmax(vals, iota, n):
        m = jnp.max(vals, axis=0, keepdims=True)
        return jnp.min(jnp.where(vals == m, iota, float(n)), axis=0, keepdims=True), m

    io_g = lax.broadcasted_iota(jnp.int32, (ROUTE_GROUP_SIZE, tl), 0).astype(F32)
    rows = []
    for g in range(N_ROUTE_GROUPS):
        blk = sel[g * ROUTE_GROUP_SIZE:(g + 1) * ROUTE_GROUP_SIZE]
        i1, m1 = first_argmax(blk, io_g, ROUTE_GROUP_SIZE)
        m2 = jnp.max(jnp.where(io_g == i1, neg, blk), axis=0, keepdims=True)
        rows.append(m1 + m2)
    gsc = jnp.concatenate(rows, axis=0)
    io_8 = lax.broadcasted_iota(jnp.int32, (N_ROUTE_GROUPS, tl), 0).astype(F32)
    chosen = jnp.zeros((N_ROUTE_GROUPS, tl), F32)
    for _ in range(TOPK_ROUTE_GROUPS):
        i, _m = first_argmax(gsc, io_8, N_ROUTE_GROUPS)
        hit = io_8 == i
        chosen = jnp.where(hit, 1.0, chosen)
        gsc = jnp.where(hit, neg, gsc)
    msel = jnp.concatenate(
        [jnp.where(chosen[g:g + 1] > 0.5, sel[g * ROUTE_GROUP_SIZE:(g + 1) * ROUTE_GROUP_SIZE], neg)
         for g in range(N_ROUTE_GROUPS)], axis=0)

    io_e = lax.broadcasted_iota(jnp.int32, (N_EXPERTS, tl), 0).astype(F32)
    multi = jnp.zeros((N_EXPERTS, tl), F32)
    idx_rows, sc_rows = [], []
    for _ in range(TOP_K):
        i, _m = first_argmax(msel, io_e, N_EXPERTS)
        hit = io_e == i
        sc_rows.append(jnp.sum(jnp.where(hit, scores, 0.0), axis=0, keepdims=True))
        msel = jnp.where(hit, neg, msel)
        multi = jnp.where(hit, 1.0, multi)
        idx_rows.append(i)
    den = sc_rows[0]
    for r in sc_rows[1:]:
        den = den + r
    den = den + 1e-20
    wts_ref[...] = jnp.concatenate([r / den * ROUTED_SCALE for r in sc_rows], axis=0)
    idx_ref[...] = jnp.concatenate(idx_rows, axis=0).astype(jnp.int32)

    multi_b = multi.astype(BF16)
    carry = carry_ref[...]
    rank_full = _dot(multi_b, upper_ref[...]) + carry
    rank_ref[...] = jnp.concatenate(
        [jnp.sum(jnp.where(io_e == i, rank_full, 0.0), axis=0, keepdims=True) for i in idx_rows],
        axis=0).astype(jnp.int32)
    carry = carry + _dot(multi_b, ones_ref[...])
    carry_ref[...] = carry
    cnt_ref[...] = carry[:, :LANES]


def _route(scores_t, router_bias):
    e, t = scores_t.shape
    tl = min(ROUTE_TILE, t)
    bias = jnp.broadcast_to(router_bias.astype(F32).reshape(e, 1), (e, tl))
    upper = (jnp.arange(tl)[:, None] < jnp.arange(tl)[None, :]).astype(BF16)
    ones = jnp.ones((tl, tl), BF16)
    tok = pl.BlockSpec((TOP_K, tl), lambda i: (0, i))

    def const(shape):
        return pl.BlockSpec(shape, lambda i: (0,) * len(shape))

    return pl.pallas_call(
        _route_kernel,
        grid=(t // tl,),
        in_specs=[pl.BlockSpec((e, tl), lambda i: (0, i)), const((e, tl)), const((tl, tl)), const((tl, tl))],
        out_specs=[tok, tok, tok, const((e, LANES))],
        out_shape=[jax.ShapeDtypeStruct((TOP_K, t), jnp.int32), jax.ShapeDtypeStruct((TOP_K, t), F32),
                   jax.ShapeDtypeStruct((TOP_K, t), jnp.int32), jax.ShapeDtypeStruct((e, LANES), F32)],
        scratch_shapes=[pltpu.VMEM((e, tl), F32)],
        compiler_params=_params(("arbitrary",)),
        name="route",
    )(scores_t, bias, upper, ones)


def _dest_kernel(idx_ref, rank_ref, start_ref, o_ref):
    tl = idx_ref.shape[1]
    io_e = lax.broadcasted_iota(jnp.int32, (N_EXPERTS, tl), 0).astype(F32)
    idx = idx_ref[...].astype(F32)
    start = start_ref[...]
    rows = [jnp.sum(jnp.where(io_e == idx[kk:kk + 1], start, 0.0), axis=0, keepdims=True) for kk in range(TOP_K)]
    o_ref[...] = jnp.concatenate(rows, axis=0).astype(jnp.int32) + rank_ref[...]


def _dest(idx_t, rank_t, padded_start):
    k, t = idx_t.shape
    tl = min(ROUTE_TILE, t)
    start = jnp.broadcast_to(padded_start.astype(F32).reshape(N_EXPERTS, 1), (N_EXPERTS, tl))
    tok = pl.BlockSpec((k, tl), lambda i: (0, i))
    return pl.pallas_call(
        _dest_kernel,
        grid=(t // tl,),
        in_specs=[tok, tok, pl.BlockSpec((N_EXPERTS, tl), lambda i: (0, 0))],
        out_specs=tok,
        out_shape=jax.ShapeDtypeStruct((k, t), jnp.int32),
        compiler_params=_params(("parallel",)),
        name="dest",
    )(idx_t, rank_t, start)


def _expert_kernel(be_ref, nv_ref, slot_ref, nxt_ref, x_ref, wg_hbm, wu_hbm, wd_hbm, o_ref,
                   wg_buf, wu_buf, wd_buf, wgb_ref, wub_ref, wdb_ref, sem):
    def weight_copies(expert, slot):
        return (pltpu.make_async_copy(wg_hbm.at[expert], wg_buf.at[slot], sem.at[slot, 0]),
                pltpu.make_async_copy(wu_hbm.at[expert], wu_buf.at[slot], sem.at[slot, 1]),
                pltpu.make_async_copy(wd_hbm.at[expert], wd_buf.at[slot], sem.at[slot, 2]))

    @pl.when(pl.program_id(0) == 0)
    def _():
        for cp in weight_copies(be_ref[0], slot_ref[0]):
            cp.start()

    half = EXPERT_BLOCK // 2
    for sub in range(EXPERT_STEP_BLOCKS):
        b = pl.program_id(0) * EXPERT_STEP_BLOCKS + sub
        base = sub * EXPERT_BLOCK
        nv = nv_ref[b]

        @pl.when(jnp.logical_or(b == 0, be_ref[b] != be_ref[jnp.maximum(b - 1, 0)]))
        def _():
            slot = slot_ref[b]
            for cp in weight_copies(be_ref[b], slot):
                cp.wait()
            nxt = nxt_ref[b]

            @pl.when(nxt >= 0)
            def _():
                for cp in weight_copies(nxt, 1 - slot):
                    cp.start()

            wgb_ref[...] = wg_buf[slot].astype(BF16)
            wub_ref[...] = wu_buf[slot].astype(BF16)
            wdb_ref[...] = wd_buf[slot].astype(BF16)

        def swiglu(rows, base=base, nv=nv):
            row = lax.broadcasted_iota(jnp.int32, (rows, x_ref.shape[1]), 0)
            x = _unpack_rows(jnp.where(row < nv, x_ref[base:base + rows, :], 0))
            g = _dot(x, wgb_ref[...])
            u = _dot(x, wub_ref[...])
            o_ref[base:base + rows, :] = _pack_rows(_dot((_silu(g) * u).astype(BF16), wdb_ref[...]))

        @pl.when(nv > half)
        def _():
            swiglu(EXPERT_BLOCK)

        @pl.when(jnp.logical_and(nv > 0, nv <= half))
        def _():
            swiglu(half)
            o_ref[base + half:base + EXPERT_BLOCK, :] = jnp.zeros((EXPERT_BLOCK - half, o_ref.shape[1]), o_ref.dtype)

        @pl.when(nv == 0)
        def _():
            o_ref[base:base + EXPERT_BLOCK, :] = jnp.zeros((EXPERT_BLOCK, o_ref.shape[1]), o_ref.dtype)


def _experts(block_expert, block_valid, xs, w_gate, w_up, w_down):
    n_slots = xs.shape[0]
    n_blocks = n_slots // EXPERT_BLOCK
    step_rows = EXPERT_STEP_BLOCKS * EXPERT_BLOCK
    assert n_blocks % EXPERT_STEP_BLOCKS == 0
    d, f = w_gate.shape[-2:]
    ids = jnp.arange(n_blocks, dtype=jnp.int32)
    change = jnp.concatenate([jnp.ones((1,), bool), block_expert[1:] != block_expert[:-1]])
    run_slot = ((jnp.cumsum(change.astype(jnp.int32)) - 1) % 2).astype(jnp.int32)
    change_at = jnp.where(change, ids, n_blocks)
    next_change = jnp.concatenate([lax.cummin(change_at[::-1])[::-1][1:], jnp.full((1,), n_blocks, jnp.int32)])
    next_expert = jnp.where(next_change < n_blocks, block_expert[jnp.minimum(next_change, n_blocks - 1)], -1)
    hbm = pl.BlockSpec(memory_space=pl.ANY)
    grid_spec = pltpu.PrefetchScalarGridSpec(
        num_scalar_prefetch=4,
        grid=(n_blocks // EXPERT_STEP_BLOCKS,),
        in_specs=[pl.BlockSpec((step_rows, PACKED),
                               lambda i, be, nv, sl, nx: (jnp.where(nv[i * EXPERT_STEP_BLOCKS] > 0, i, 0), 0)),
                  hbm, hbm, hbm],
        out_specs=pl.BlockSpec((step_rows, PACKED), lambda i, be, nv, sl, nx: (i, 0)),
        scratch_shapes=[pltpu.VMEM((2, d, f), w_gate.dtype), pltpu.VMEM((2, d, f), w_up.dtype),
                        pltpu.VMEM((2, f, d), w_down.dtype),
                        pltpu.VMEM((d, f), BF16), pltpu.VMEM((d, f), BF16), pltpu.VMEM((f, d), BF16),
                        pltpu.SemaphoreType.DMA((2, 3))],
    )
    return pl.pallas_call(
        _expert_kernel,
        grid_spec=grid_spec,
        out_shape=jax.ShapeDtypeStruct((n_slots, PACKED), jnp.int32),
        compiler_params=_params(("arbitrary",)),
        name="experts",
    )(block_expert, block_valid, run_slot, next_expert.astype(jnp.int32), xs, w_gate, w_up, w_down)


def _final_kernel(x1_ref, h2_ref, yk_ref, wts_ref, g2_ref, wsg_ref, wsu_ref, wsd_ref, fnw_ref, o_ref):
    h = _unpack_rows(h2_ref[0])
    act = (_silu(_dot(h, wsg_ref[...])) * _dot(h, wsu_ref[...])).astype(BF16)
    moe = _dot(act, wsd_ref[...])
    wts = wts_ref[...]
    for kk in range(TOP_K):
        moe = moe + _unpack_rows(yk_ref[kk]).astype(F32) * wts[:, kk:kk + 1]
    x2 = x1_ref[0] + g2_ref[0] * moe
    ms = jnp.mean(x2 * x2, axis=-1, keepdims=True)
    o_ref[0] = x2 * lax.rsqrt(ms + EPS) * fnw_ref[...]


def _final(x1, h2p, yk, wts, gate2, w_sh_gate, w_sh_up, w_sh_down, final_norm_w, tm):
    bsz, s, d = x1.shape
    nt = s // tm
    f = w_sh_gate.shape[-1]
    seq = pl.BlockSpec((1, tm, d), lambda b, i: (b, i, 0))

    def const(shape):
        return pl.BlockSpec(shape, lambda b, i: (0,) * len(shape))

    return pl.pallas_call(
        _final_kernel,
        grid=(bsz, nt),
        in_specs=[seq, pl.BlockSpec((1, tm, PACKED), lambda b, i: (b, i, 0)),
                  pl.BlockSpec((TOP_K, tm, PACKED), lambda b, i: (0, b * nt + i, 0)),
                  pl.BlockSpec((tm, TOP_K), lambda b, i: (b * nt + i, 0)),
                  pl.BlockSpec((1, 1, d), lambda b, i: (b, 0, 0)),
                  const((d, f)), const((d, f)), const((f, d)), const((1, d))],
        out_specs=seq,
        out_shape=jax.ShapeDtypeStruct((bsz, s, d), F32),
        compiler_params=_params(("parallel", "parallel")),
        name="final",
    )(x1, h2p, yk, wts, gate2, w_sh_gate.astype(BF16), w_sh_up.astype(BF16), w_sh_down.astype(BF16),
      final_norm_w.astype(F32).reshape(1, d))


def _gather_rows(table, idx):
    n = idx.shape[0]
    width = table.shape[1]
    workers = SC_CORES * SC_SUBCORES
    per_worker = n // workers
    steps = per_worker // SC_WINDOW
    assert per_worker * workers == n and steps * SC_WINDOW == per_worker and steps % 2 == 0
    mesh = plsc.VectorSubcoreMesh(core_axis_name="c", subcore_axis_name="s")
    win = lambda dtype, *shape: pltpu.VMEM(shape, dtype)

    @functools.partial(
        pl.kernel, mesh=mesh,
        out_type=jax.ShapeDtypeStruct((n, width), table.dtype),
        scratch_types=[win(jnp.int32, SC_WINDOW), win(jnp.int32, SC_WINDOW),
                       win(table.dtype, SC_WINDOW, width), win(table.dtype, SC_WINDOW, width),
                       pltpu.SemaphoreType.DMA, pltpu.SemaphoreType.DMA,
                       pltpu.SemaphoreType.DMA, pltpu.SemaphoreType.DMA],
        name="gather_rows",
    )
    def run(table_hbm, idx_hbm, out_hbm, idx_a, idx_b, rows_a, rows_b, gsem_a, gsem_b, osem_a, osem_b):
        base = (lax.axis_index("s") * SC_CORES + lax.axis_index("c")) * per_worker

        @pl.loop(0, steps, step=2)
        def _(i):
            off_a = pl.multiple_of(base + i * SC_WINDOW, SC_WINDOW)
            off_b = pl.multiple_of(off_a + SC_WINDOW, SC_WINDOW)
            pltpu.sync_copy(idx_hbm.at[pl.ds(off_a, SC_WINDOW)], idx_a)
            gather_a = pltpu.async_copy(table_hbm.at[idx_a], rows_a, gsem_a)
            pltpu.sync_copy(idx_hbm.at[pl.ds(off_b, SC_WINDOW)], idx_b)
            gather_b = pltpu.async_copy(table_hbm.at[idx_b], rows_b, gsem_b)
            gather_a.wait()
            store_a = pltpu.async_copy(rows_a, out_hbm.at[pl.ds(off_a, SC_WINDOW)], osem_a)
            gather_b.wait()
            store_b = pltpu.async_copy(rows_b, out_hbm.at[pl.ds(off_b, SC_WINDOW)], osem_b)
            store_a.wait()
            store_b.wait()

    return run(table, idx)


def _scatter_rows(rows, dest_kt, n_slots):
    t, width = rows.shape
    top_k = dest_kt.shape[0]
    workers = SC_CORES * SC_SUBCORES
    window = LANES
    per_worker = t // workers
    steps = per_worker // window
    assert per_worker * workers == t and steps * window == per_worker
    mesh = plsc.VectorSubcoreMesh(core_axis_name="c", subcore_axis_name="s")

    @functools.partial(
        pl.kernel, mesh=mesh,
        out_type=jax.ShapeDtypeStruct((n_slots, width), rows.dtype),
        scratch_types=[pltpu.VMEM((top_k, window), jnp.int32),
                       pltpu.VMEM((window, width), rows.dtype),
                       pltpu.SemaphoreType.DMA],
        name="scatter_rows",
    )
    def run(rows_hbm, dest_hbm, out_hbm, idx_v, rows_v, sem):
        base = (lax.axis_index("s") * SC_CORES + lax.axis_index("c")) * per_worker

        @pl.loop(0, steps)
        def _(i):
            off = pl.multiple_of(base + i * window, window)
            pltpu.sync_copy(rows_hbm.at[pl.ds(off, window)], rows_v)
            pltpu.sync_copy(dest_hbm.at[:, pl.ds(off, window)], idx_v)
            copies = [pltpu.async_copy(rows_v, out_hbm.at[idx_v.at[kk]], sem) for kk in range(top_k)]
            for cp in copies:
                cp.wait()

    return run(rows, dest_kt)


def _layer(x, c, positions, w_ada, b_ada, norm1_w, w_in, conv_w, conv_b, dt_bias, a_log, d_skip, ssm_norm_w,
           w_ssm_out, w_ret_out, w_out, norm2_w, w_router, router_bias, w_exp_gate, w_exp_up, w_exp_down,
           w_sh_gate, w_sh_up, w_sh_down):
    bsz, s, d = x.shape
    t = bsz * s
    tm = min(TOKEN_TILE, s)

    mod = _mod(c, w_ada, b_ada)
    shift1, scale1, gate1, shift2, scale2, gate2 = (mod[:, i * d:(i + 1) * d].reshape(bsz, 1, d) for i in range(6))

    bounds = np.cumsum((0, SSM_D_INNER, SSM_CONV_DIM, SSM_HEADS, RET_QK_WIDTH, RET_QK_WIDTH, RET_V_WIDTH,
                        RET_V_WIDTH, d, d))
    wz, wxbc, wdt, wq, wk, wv, wg, wgs, wgr = (w_in[:, bounds[i]:bounds[i + 1]] for i in range(9))
    nw1 = norm1_w.astype(F32).reshape(1, d)
    w_a = jnp.concatenate([wxbc, wz, jnp.pad(wdt, ((0, 0), (0, DT_PAD - SSM_HEADS)))], axis=1).astype(BF16)
    w_b = jnp.concatenate([wq, wk, wv, wg, wgs, wgr], axis=1).astype(BF16)
    h, xbc, zs, dt = _inproj(_inproj_ssd_kernel, x, (nw1, scale1, shift1), w_a,
                             (d, SSM_CONV_DIM, SSM_D_INNER, DT_PAD), (BF16, BF16, BF16, F32), tm)
    q, k, v, g, gs, gr = _inproj(_inproj_ret_kernel, h, (), w_b,
                                 (RET_QK_WIDTH, RET_QK_WIDTH, RET_V_WIDTH, RET_V_WIDTH, d, d), (BF16,) * 6, tm)

    ys = _ssd(xbc, zs, dt, conv_w, conv_b, dt_bias, a_log, d_skip, ssm_norm_w)
    cos, sin = _rope(positions, tm)
    yr = _ret(q, k, v, g, cos, sin)
    x1, h2, scores_t = _merge(ys, yr, gs, gr, x, gate1, scale2, shift2, norm2_w, w_ssm_out, w_ret_out, w_out,
                              w_router, tm)

    idx_t, wts_t, rank_t, cnt = _route(scores_t, router_bias)

    counts = cnt[:, 0].astype(jnp.int32)
    padded = (counts + EXPERT_BLOCK - 1) // EXPERT_BLOCK * EXPERT_BLOCK
    padded_end = jnp.cumsum(padded)
    padded_start = padded_end - padded
    n_blocks = -(-(t * TOP_K) // EXPERT_BLOCK) + N_EXPERTS
    n_slots = n_blocks * EXPERT_BLOCK
    dest_t = _dest(idx_t, rank_t, padded_start)
    block_first = jnp.arange(n_blocks, dtype=jnp.int32) * EXPERT_BLOCK
    block_expert = jnp.minimum(jnp.sum(padded_end[None, :] <= block_first[:, None], axis=1), N_EXPERTS - 1)
    owner = (block_expert[:, None] == jnp.arange(N_EXPERTS, dtype=jnp.int32)[None, :]).astype(jnp.int32)
    region_end = jnp.sum(owner * (padded_start + counts)[None, :], axis=1)
    block_valid = jnp.clip(region_end - block_first, 0, EXPERT_BLOCK).astype(jnp.int32)
    xs = _scatter_rows(h2.reshape(t, PACKED), dest_t, n_slots)
    y_sorted = _experts(block_expert.astype(jnp.int32), block_valid, xs, w_exp_gate, w_exp_up, w_exp_down)
    yk = _gather_rows(y_sorted, dest_t.reshape(-1)).reshape(TOP_K, t, PACKED)
    return x1, h2, yk, wts_t.T, gate2


def kernel(x, c, positions, w_ada, b_ada, norm1_w, w_in, conv_w, conv_b, dt_bias, a_log, d_skip, ssm_norm_w,
           w_ssm_out, w_ret_out, w_out, norm2_w, w_router, router_bias, w_exp_gate, w_exp_up, w_exp_down,
           w_sh_gate, w_sh_up, w_sh_down, final_norm_w):
    assert w_ada.shape[0] == 1, "the final rmsnorm is fused into the single layer's last kernel"
    tm = min(TOKEN_TILE, x.shape[1])
    x1, h2, yk, wts, gate2 = _layer(
        x, c, positions, w_ada[0], b_ada[0], norm1_w[0], w_in[0], conv_w[0], conv_b[0], dt_bias[0], a_log[0],
        d_skip[0], ssm_norm_w[0], w_ssm_out[0], w_ret_out[0], w_out[0], norm2_w[0], w_router[0],
        router_bias[0], w_exp_gate[0], w_exp_up[0], w_exp_down[0], w_sh_gate[0], w_sh_up[0], w_sh_down[0])
    return _final(x1, h2, yk, wts, gate2, w_sh_gate[0], w_sh_up[0], w_sh_down[0], final_norm_w, tm)
```

```python
import functools

import numpy as np
import jax
import jax.numpy as jnp
from jax import lax
from jax.experimental import pallas as pl
from jax.experimental.pallas import tpu as pltpu
from jax.experimental.pallas import tpu_sc as plsc

F32 = jnp.float32
BF16 = jnp.bfloat16

EPS = 1e-6
D_MODEL = 1024
SSM_D_INNER = 2048
SSM_HEADDIM = 64
SSM_HEADS = 32
SSM_GROUPS = 4
SSM_STATE = 128
SSM_CONV = 4
SSM_CONV_DIM = SSM_D_INNER + 2 * SSM_GROUPS * SSM_STATE
GROUP_WIDTH = SSM_D_INNER // SSM_GROUPS
RET_HEADS = 4
RET_QK_DIM = 256
RET_V_DIM = 512
RET_QK_WIDTH = RET_HEADS * RET_QK_DIM
RET_V_WIDTH = RET_HEADS * RET_V_DIM
ROPE_BASE = 10000.0
ROPE_HALF = RET_QK_DIM // 2
N_EXPERTS = 256
TOP_K = 8
N_ROUTE_GROUPS = 8
TOPK_ROUTE_GROUPS = 4
ROUTE_GROUP_SIZE = N_EXPERTS // N_ROUTE_GROUPS
EXPERT_DIM = 256
ROUTED_SCALE = 2.5

TOKEN_TILE = 512
CHUNK = 128
SSD_STEP_CHUNKS = 4
RET_STEP_CHUNKS = 8
SSD_STRIP = 256
LANES = 128
SUBLANES = 8
DT_PAD = LANES
EXPERT_BLOCK = 512
EXPERT_STEP_BLOCKS = 4
ROUTE_TILE = 256
DEST_TILE = 512
VMEM_LIMIT = 56 * 1024 * 1024
PACKED = D_MODEL // 2
SC_CORES = 2
SC_SUBCORES = 16
SC_WINDOW = 64


LOG2E = 1.4426950408889634


def _silu(x):
    return x / (1.0 + jnp.exp2(x * -LOG2E))


def _sigmoid(x):
    return 1.0 / (1.0 + jnp.exp2(x * -LOG2E))


def _split3(a):
    hi = a.astype(BF16)
    r1 = a - hi.astype(F32)
    mid = r1.astype(BF16)
    lo = (r1 - mid.astype(F32)).astype(BF16)
    return hi, mid, lo


def _pack_rows(x):
    w = x.shape[1] // 2
    hi = lax.bitcast_convert_type(x[:, :w].astype(BF16).astype(F32), jnp.uint32)
    lo = lax.bitcast_convert_type(x[:, w:].astype(BF16).astype(F32), jnp.uint32)
    return lax.bitcast_convert_type(hi | (lo >> 16), jnp.int32)


def _unpack_rows(p):
    u = lax.bitcast_convert_type(p, jnp.uint32)
    hi = lax.bitcast_convert_type(u & jnp.uint32(0xFFFF0000), F32)
    lo = lax.bitcast_convert_type(u << 16, F32)
    return jnp.concatenate([hi, lo], axis=1).astype(BF16)


def _dot(a, b):
    return jnp.dot(a, b, preferred_element_type=F32)


def _dot_nt(a, b):
    return lax.dot_general(a, b, (((1,), (1,)), ((), ())), preferred_element_type=F32)


def _params(sem):
    return pltpu.CompilerParams(dimension_semantics=sem, vmem_limit_bytes=VMEM_LIMIT)


def _mod_kernel(c_ref, w_ref, b_ref, o_ref):
    o_ref[...] = _dot(_silu(c_ref[...]), w_ref[...]) + b_ref[...]


def _mod(c, w_ada, b_ada):
    bsz, d = c.shape
    n = w_ada.shape[1]
    return pl.pallas_call(
        _mod_kernel,
        grid=(n // d,),
        in_specs=[pl.BlockSpec((bsz, d), lambda j: (0, 0)),
                  pl.BlockSpec((d, d), lambda j: (0, j)),
                  pl.BlockSpec((1, d), lambda j: (0, j))],
        out_specs=pl.BlockSpec((bsz, d), lambda j: (0, j)),
        out_shape=jax.ShapeDtypeStruct((bsz, n), F32),
        compiler_params=_params(("arbitrary",)),
        name="mod",
    )(c, w_ada, b_ada.reshape(1, n))


def _rms_mod(x, nw, scale, shift):
    ms = jnp.mean(x * x, axis=-1, keepdims=True)
    return (x * lax.rsqrt(ms + EPS) * nw) * (1.0 + scale) + shift


PROJ_STRIP = 512


def _project(hb, w_ref, off, o_ref, width, act):
    for c0 in range(0, width, PROJ_STRIP):
        cw = min(PROJ_STRIP, width - c0)
        v = _dot(hb, w_ref[:, off + c0:off + c0 + cw])
        o_ref[0, :, c0:c0 + cw] = (v if act is None else act(v)).astype(o_ref.dtype)


def _inproj_ssd_kernel(x_ref, nw_ref, sc_ref, sh_ref, w_ref, h_ref, xbc_ref, zs_ref, dt_ref):
    hb = _rms_mod(x_ref[0], nw_ref[...], sc_ref[0], sh_ref[0]).astype(BF16)
    h_ref[0] = hb
    _project(hb, w_ref, 0, xbc_ref, SSM_CONV_DIM, None)
    _project(hb, w_ref, SSM_CONV_DIM, zs_ref, SSM_D_INNER, _silu)
    _project(hb, w_ref, SSM_CONV_DIM + SSM_D_INNER, dt_ref, DT_PAD, None)


def _inproj_ret_kernel(h_ref, w_ref, q_ref, k_ref, v_ref, gs_ref, gate_s_ref, gate_r_ref):
    hb = h_ref[0]
    off = 0
    for o_ref, act in ((q_ref, None), (k_ref, None), (v_ref, None), (gs_ref, _silu), (gate_s_ref, None),
                       (gate_r_ref, None)):
        width = o_ref.shape[-1]
        _project(hb, w_ref, off, o_ref, width, act)
        off += width


def _inproj(body, rows, vecs, w, widths, dtypes, tm):
    bsz, s, d = rows.shape
    vec_specs = [pl.BlockSpec((1, d), lambda b, i: (0, 0)) if v.ndim == 2 else
                 pl.BlockSpec((1, 1, d), lambda b, i: (b, 0, 0)) for v in vecs]
    return pl.pallas_call(
        body,
        grid=(bsz, s // tm),
        in_specs=[pl.BlockSpec((1, tm, d), lambda b, i: (b, i, 0))] + vec_specs
                 + [pl.BlockSpec(w.shape, lambda b, i: (0, 0))],
        out_specs=[pl.BlockSpec((1, tm, wd), lambda b, i: (b, i, 0)) for wd in widths],
        out_shape=[jax.ShapeDtypeStruct((bsz, s, wd), dt) for wd, dt in zip(widths, dtypes)],
        compiler_params=_params(("parallel", "parallel")),
        name="inproj",
    )(rows, *vecs, w)


def _rope_kernel(pos_ref, inv_ref, cos_ref, sin_ref):
    ang = pos_ref[0].astype(F32) * inv_ref[...]
    cos_ref[0] = jnp.cos(ang)
    sin_ref[0] = jnp.sin(ang)


def _rope(positions, tm):
    bsz, s = positions.shape
    inv = (1.0 / (ROPE_BASE ** (jnp.arange(ROPE_HALF, dtype=F32) / ROPE_HALF))).reshape(1, ROPE_HALF)
    spec = pl.BlockSpec((1, tm, ROPE_HALF), lambda b, i: (b, i, 0))
    return pl.pallas_call(
        _rope_kernel,
        grid=(bsz, s // tm),
        in_specs=[pl.BlockSpec((1, tm, 1), lambda b, i: (b, i, 0)),
                  pl.BlockSpec((1, ROPE_HALF), lambda b, i: (0, 0))],
        out_specs=[spec, spec],
        out_shape=[jax.ShapeDtypeStruct((bsz, s, ROPE_HALF), F32)] * 2,
        compiler_params=_params(("parallel", "parallel")),
        name="rope",
    )(positions.reshape(bsz, s, 1), inv)


def _ssd_kernel(xbc_ref, zs_ref, dt_ref, cw_ref, cb_ref, dtb_ref, a_ref, dx_ref, nw_ref, r_ref, tril_ref, shift_ref,
                o_ref, tail_ref, state_ref, xs_ref, bc_ref, y_ref):
    @pl.when(pl.program_id(1) == 0)
    def _():
        tail_ref[...] = jnp.zeros_like(tail_ref)
        state_ref[...] = jnp.zeros_like(state_ref)

    for cc in range(SSD_STEP_CHUNKS):
        rows = pl.ds(cc * CHUNK, CHUNK)
        _ssd_chunk(xbc_ref.at[:, rows, :], zs_ref.at[:, rows, :], dt_ref.at[:, rows, :], cw_ref, cb_ref, dtb_ref,
                   a_ref, dx_ref, nw_ref, r_ref, tril_ref, shift_ref, o_ref.at[:, rows, :], tail_ref, state_ref,
                   xs_ref, bc_ref, y_ref)


def _ssd_chunk(xbc_ref, zs_ref, dt_ref, cw_ref, cb_ref, dtb_ref, a_ref, dx_ref, nw_ref, r_ref, tril_ref, shift_ref,
               o_ref, tail_ref, state_ref, xs_ref, bc_ref, y_ref):
    L = CHUNK

    row8 = lax.broadcasted_iota(jnp.int32, (SUBLANES, 1), 0)
    for c0 in range(0, SSM_CONV_DIM, SSD_STRIP):
        cs = slice(c0, c0 + SSD_STRIP)
        u_b = xbc_ref[0, :, cs]
        u = u_b.astype(F32)
        tail = tail_ref[:, cs]
        acc = u * cw_ref[SSM_CONV - 1:SSM_CONV, cs] + cb_ref[:, cs]
        for d in range(1, SSM_CONV):
            sh = _dot(shift_ref[d - 1], u_b)
            head = sh[0:SUBLANES] + jnp.where(row8 < d, pltpu.roll(tail, d, axis=0), 0.0)
            sh = jnp.concatenate([head, sh[SUBLANES:]], axis=0)
            acc = acc + sh * cw_ref[SSM_CONV - 1 - d:SSM_CONV - d, cs]
        tail_ref[:, cs] = u[L - SUBLANES:L]
        act = _silu(acc)
        if c0 < SSM_D_INNER:
            xs_ref[:, cs] = act
        else:
            bc_ref[:, c0 - SSM_D_INNER:c0 - SSM_D_INNER + SSD_STRIP] = act.astype(BF16)

    dt_raw = dt_ref[0] + dtb_ref[...]
    dtv = jnp.maximum(dt_raw, 0.0) + jnp.log1p(jnp.exp(-jnp.abs(dt_raw)))
    a = dtv * (a_ref[...] * LOG2E)
    tril = tril_ref[...]
    a_hi, a_mid, a_lo = _split3(a)
    acum = _dot(tril, a_hi) + _dot(tril, a_mid) + _dot(tril, a_lo)
    col_t = (acum - jnp.log(dtv) * LOG2E).T
    last = acum[L - 1:L, :]
    e1_b = jnp.exp2(acum).astype(BF16)
    e2_b = (jnp.exp2(last - acum) * dtv).astype(BF16)
    l_hi, l_mid, l_lo = _split3(jnp.broadcast_to(jnp.exp2(last), (SUBLANES, LANES)))
    r = r_ref[...]
    decay_x = (_dot(l_hi, r) + _dot(l_mid, r) + _dot(l_lo, r))[0:1, :]

    causal = (lax.broadcasted_iota(jnp.int32, (L, L), 0) >= lax.broadcasted_iota(jnp.int32, (L, L), 1))
    lane_lo = lax.broadcasted_iota(jnp.int32, (L, LANES), 1) < SSM_HEADDIM
    heads_per_group = SSM_HEADS // SSM_GROUPS
    bc_off = SSM_GROUPS * SSM_STATE
    for g in range(SSM_GROUPS):
        bg_b = bc_ref[:, g * SSM_STATE:(g + 1) * SSM_STATE]
        cg_b = bc_ref[:, bc_off + g * SSM_STATE:bc_off + (g + 1) * SSM_STATE]
        cb = _dot_nt(cg_b, bg_b)
        bgt_b = bg_b.astype(F32).T.astype(BF16)
        for pair in range(heads_per_group // 2):
            p = g * (heads_per_group // 2) + pair
            xp = xs_ref[:, p * LANES:(p + 1) * LANES].astype(BF16)
            y_pair = None
            for sub in range(2):
                h = 2 * p + sub
                seg = jnp.broadcast_to(acum[:, h:h + 1], (L, L)) - col_t[h:h + 1, :]
                w = cb * jnp.exp2(jnp.where(causal, seg, -jnp.inf))
                xh = jnp.where(lane_lo if sub == 0 else jnp.logical_not(lane_lo), xp, jnp.zeros_like(xp))
                yh = _dot(w.astype(BF16), xh)
                y_pair = yh if y_pair is None else y_pair + yh
            y_ref[:, p * LANES:(p + 1) * LANES] = y_pair
        sq = jnp.zeros((L, 1), F32)
        for j in range(GROUP_WIDTH // SSD_STRIP):
            ls = slice(j * SSD_STRIP, (j + 1) * SSD_STRIP)
            cs = slice(g * GROUP_WIDTH + j * SSD_STRIP, g * GROUP_WIDTH + (j + 1) * SSD_STRIP)
            xs_j = xs_ref[:, cs]
            st_j = state_ref[g, :, ls]
            y_j = y_ref[:, cs] + _dot(cg_b, st_j.astype(BF16)) * _dot(e1_b, r_ref[:, cs]) + xs_j * dx_ref[:, cs]
            y_j = y_j * zs_ref[0, :, cs].astype(F32)
            sq = sq + jnp.sum(y_j * y_j, axis=-1, keepdims=True)
            y_ref[:, cs] = y_j
            upd = _dot(bgt_b, (xs_j * _dot(e2_b, r_ref[:, cs])).astype(BF16))
            state_ref[g, :, ls] = st_j * decay_x[:, cs] + upd
        scale = lax.rsqrt(sq * (1.0 / GROUP_WIDTH) + EPS)
        for j in range(GROUP_WIDTH // SSD_STRIP):
            cs = slice(g * GROUP_WIDTH + j * SSD_STRIP, g * GROUP_WIDTH + (j + 1) * SSD_STRIP)
            o_ref[0, :, cs] = (y_ref[:, cs] * scale * nw_ref[:, cs]).astype(o_ref.dtype)


def _ssd(xbc, zs, dt, conv_w, conv_b, dt_bias, a_log, d_skip, ssm_norm_w):
    bsz, s, _ = xbc.shape
    pad = DT_PAD - SSM_HEADS
    dtb = jnp.pad(dt_bias.astype(F32), (0, pad)).reshape(1, DT_PAD)
    a_neg = jnp.pad(-jnp.exp(a_log.astype(F32)), (0, pad)).reshape(1, DT_PAD)
    dx = jnp.repeat(d_skip.astype(F32), SSM_HEADDIM).reshape(1, SSM_D_INNER)
    expand = (jnp.arange(DT_PAD)[:, None] == (jnp.arange(SSM_D_INNER)[None, :] // SSM_HEADDIM)).astype(BF16)
    tril = (jnp.arange(CHUNK)[:, None] >= jnp.arange(CHUNK)[None, :]).astype(BF16)
    shift = jnp.stack([(jnp.arange(CHUNK)[:, None] - d == jnp.arange(CHUNK)[None, :]).astype(BF16)
                       for d in range(1, SSM_CONV)])

    def seq(width):
        return pl.BlockSpec((1, SSD_STEP_CHUNKS * CHUNK, width), lambda b, c: (b, c, 0))

    def const(shape):
        return pl.BlockSpec(shape, lambda b, c: (0,) * len(shape))

    return pl.pallas_call(
        _ssd_kernel,
        grid=(bsz, s // (SSD_STEP_CHUNKS * CHUNK)),
        in_specs=[seq(SSM_CONV_DIM), seq(SSM_D_INNER), seq(DT_PAD),
                  const((SSM_CONV, SSM_CONV_DIM)), const((1, SSM_CONV_DIM)), const((1, DT_PAD)), const((1, DT_PAD)),
                  const((1, SSM_D_INNER)), const((1, SSM_D_INNER)), const((DT_PAD, SSM_D_INNER)),
                  const((CHUNK, CHUNK)), const((SSM_CONV - 1, CHUNK, CHUNK))],
        out_specs=seq(SSM_D_INNER),
        out_shape=jax.ShapeDtypeStruct((bsz, s, SSM_D_INNER), BF16),
        scratch_shapes=[pltpu.VMEM((SUBLANES, SSM_CONV_DIM), F32),
                        pltpu.VMEM((SSM_GROUPS, SSM_STATE, GROUP_WIDTH), F32),
                        pltpu.VMEM((CHUNK, SSM_D_INNER), F32),
                        pltpu.VMEM((CHUNK, 2 * SSM_GROUPS * SSM_STATE), BF16),
                        pltpu.VMEM((CHUNK, SSM_D_INNER), F32)],
        compiler_params=_params(("parallel", "arbitrary")),
        name="ssd",
    )(xbc, zs, dt, conv_w.astype(F32), conv_b.astype(F32).reshape(1, SSM_CONV_DIM), dtb, a_neg, dx,
      ssm_norm_w.astype(F32).reshape(1, SSM_D_INNER), expand, tril, shift)


def _ret_consts():
    lg = np.log1p(-(2.0 ** (-5.0 - np.arange(RET_HEADS, dtype=np.float64))))
    idx = np.arange(CHUNK, dtype=np.float64)
    rel = idx[:, None] - idx[None, :]
    intra = np.where(rel >= 0, np.exp(np.maximum(rel, 0.0)[None] * lg[:, None, None]), 0.0)
    qd = np.exp((idx + 1.0)[None, :] * lg[:, None])
    kd = np.exp((CHUNK - 1.0 - idx)[None, :] * lg[:, None]) * (RET_QK_DIM ** -0.5)
    qd = np.broadcast_to(qd[:, :, None], (RET_HEADS, CHUNK, RET_QK_DIM))
    kd = np.broadcast_to(kd[:, :, None], (RET_HEADS, CHUNK, RET_QK_DIM))
    chunk_decay = tuple(float(v) for v in np.exp(CHUNK * lg))
    return (jnp.asarray(intra, F32), jnp.asarray(qd, F32), jnp.asarray(kd, F32), chunk_decay)


def _ret_kernel(q_ref, k_ref, v_ref, gs_ref, cos_ref, sin_ref, intra_ref, qd_ref, kd_ref, o_ref, state_ref, *,
                chunk_decay):
    @pl.when(pl.program_id(1) == 0)
    def _():
        state_ref[...] = jnp.zeros_like(state_ref)

    for cc in range(RET_STEP_CHUNKS):
        rows = pl.ds(cc * CHUNK, CHUNK)
        _ret_chunk(*(ref.at[:, rows, :] for ref in (q_ref, k_ref, v_ref, gs_ref, cos_ref, sin_ref)),
                   intra_ref, qd_ref, kd_ref, o_ref.at[:, rows, :], state_ref, chunk_decay)


def _ret_chunk(q_ref, k_ref, v_ref, gs_ref, cos_ref, sin_ref, intra_ref, qd_ref, kd_ref, o_ref, state_ref,
               chunk_decay):
    cos = cos_ref[0]
    sin = sin_ref[0]

    def rot(t):
        t1, t2 = t[:, :ROPE_HALF], t[:, ROPE_HALF:]
        return jnp.concatenate([t1 * cos - t2 * sin, t1 * sin + t2 * cos], axis=1)

    for h in range(RET_HEADS):
        qs = slice(h * RET_QK_DIM, (h + 1) * RET_QK_DIM)
        vs = slice(h * RET_V_DIM, (h + 1) * RET_V_DIM)
        qr = rot(q_ref[0, :, qs].astype(F32))
        kr = rot(k_ref[0, :, qs].astype(F32))
        vh = v_ref[0, :, vs]
        s = _dot_nt(qr.astype(BF16), (kr * (RET_QK_DIM ** -0.5)).astype(BF16)) * intra_ref[h]
        st = state_ref[h]
        y = _dot(s.astype(BF16), vh) + _dot((qr * qd_ref[h]).astype(BF16), st.astype(BF16))
        state_ref[h] = st * chunk_decay[h] + _dot((kr * kd_ref[h]).T.astype(BF16), vh)
        mu = jnp.mean(y, axis=-1, keepdims=True)
        yc = y - mu
        var = jnp.mean(yc * yc, axis=-1, keepdims=True)
        o_ref[0, :, vs] = (gs_ref[0, :, vs].astype(F32) * (yc * lax.rsqrt(var + EPS))).astype(o_ref.dtype)


def _ret(q, k, v, g, cos, sin):
    bsz, s, _ = q.shape
    intra, qd, kd, chunk_decay = _ret_consts()

    def seq(width):
        return pl.BlockSpec((1, RET_STEP_CHUNKS * CHUNK, width), lambda b, c: (b, c, 0))

    def const(shape):
        return pl.BlockSpec(shape, lambda b, c: (0,) * len(shape))

    return pl.pallas_call(
        functools.partial(_ret_kernel, chunk_decay=chunk_decay),
        grid=(bsz, s // (RET_STEP_CHUNKS * CHUNK)),
        in_specs=[seq(RET_QK_WIDTH), seq(RET_QK_WIDTH), seq(RET_V_WIDTH), seq(RET_V_WIDTH),
                  seq(ROPE_HALF), seq(ROPE_HALF),
                  const(intra.shape), const(qd.shape), const(kd.shape)],
        out_specs=seq(RET_V_WIDTH),
        out_shape=jax.ShapeDtypeStruct((bsz, s, RET_V_WIDTH), BF16),
        scratch_shapes=[pltpu.VMEM((RET_HEADS, RET_QK_DIM, RET_V_DIM), F32)],
        compiler_params=_params(("parallel", "arbitrary")),
        name="ret",
    )(q, k, v, g, cos, sin, intra, qd, kd)


def _merge_kernel(ys_ref, yr_ref, gs_ref, gr_ref, x_ref, g1_ref, sc2_ref, sh2_ref, n2w_ref,
                  wso_ref, wro_ref, wo_ref, wrh_ref, wrl_ref, x1_ref, h2_ref, sct_ref):
    y_ssm = _dot(ys_ref[0], wso_ref[...])
    y_ret = _dot(yr_ref[0], wro_ref[...])
    merged = _sigmoid(gs_ref[0].astype(F32)) * y_ssm + _sigmoid(gr_ref[0].astype(F32)) * y_ret
    x1 = x_ref[0] + g1_ref[0] * _dot(merged.astype(BF16), wo_ref[...])
    x1_ref[0] = x1
    h2 = _rms_mod(x1, n2w_ref[...], sc2_ref[0], sh2_ref[0])
    h_hi = h2.astype(BF16)
    h2_ref[0] = _pack_rows(h2)
    h_lo = (h2 - h_hi.astype(F32)).astype(BF16)
    wrh = wrh_ref[...]
    logits_t = _dot_nt(wrh, h_hi) + _dot_nt(wrh, h_lo) + _dot_nt(wrl_ref[...], h_hi)
    sct_ref[...] = _sigmoid(logits_t)


def _merge(ys, yr, gs, gr, x, gate1, scale2, shift2, norm2_w, w_ssm_out, w_ret_out, w_out, w_router, tm):
    bsz, s, d = x.shape
    nt = s // tm
    wrt = w_router.astype(F32).T
    wrh = wrt.astype(BF16)
    wrl = (wrt - wrh.astype(F32)).astype(BF16)

    def seq(width):
        return pl.BlockSpec((1, tm, width), lambda b, i: (b, i, 0))

    def const(shape):
        return pl.BlockSpec(shape, lambda b, i: (0,) * len(shape))

    vec = pl.BlockSpec((1, 1, d), lambda b, i: (b, 0, 0))
    return pl.pallas_call(
        _merge_kernel,
        grid=(bsz, nt),
        in_specs=[seq(SSM_D_INNER), seq(RET_V_WIDTH), seq(d), seq(d), seq(d), vec, vec, vec, const((1, d)),
                  const((SSM_D_INNER, d)), const((RET_V_WIDTH, d)), const((d, d)),
                  const((N_EXPERTS, d)), const((N_EXPERTS, d))],
        out_specs=[seq(d), seq(PACKED), pl.BlockSpec((N_EXPERTS, tm), lambda b, i: (0, b * nt + i))],
        out_shape=[jax.ShapeDtypeStruct((bsz, s, d), F32), jax.ShapeDtypeStruct((bsz, s, PACKED), jnp.int32),
                   jax.ShapeDtypeStruct((N_EXPERTS, bsz * s), F32)],
        compiler_params=_params(("parallel", "parallel")),
        name="merge",
    )(ys, yr, gs, gr, x, gate1, scale2, shift2, norm2_w.astype(F32).reshape(1, d),
      w_ssm_out.astype(BF16), w_ret_out.astype(BF16), w_out.astype(BF16), wrh, wrl)


def _route_kernel(sc_ref, bias_ref, upper_ref, ones_ref, idx_ref, wts_ref, rank_ref, cnt_ref, carry_ref):
    @pl.when(pl.program_id(0) == 0)
    def _():
        carry_ref[...] = jnp.zeros_like(carry_ref)

    tl = sc_ref.shape[1]
    neg = -jnp.inf
    scores = sc_ref[...]
    sel = scores + bias_ref[...]

    def first_argmax(vals, iota, n):
        m = jnp.max(vals, axis=0, keepdims=True)
        return jnp.min(jnp.where(vals == m, iota, float(n)), axis=0, keepdims=True), m

    io_g = lax.broadcasted_iota(jnp.int32, (ROUTE_GROUP_SIZE, tl), 0).astype(F32)
    rows = []
    for g in range(N_ROUTE_GROUPS):
        blk = sel[g * ROUTE_GROUP_SIZE:(g + 1) * ROUTE_GROUP_SIZE]
        i1, m1 = first_argmax(blk, io_g, ROUTE_GROUP_SIZE)
        m2 = jnp.max(jnp.where(io_g == i1, neg, blk), axis=0, keepdims=True)
        rows.append(m1 + m2)
    gsc = jnp.concatenate(rows, axis=0)
    io_8 = lax.broadcasted_iota(jnp.int32, (N_ROUTE_GROUPS, tl), 0).astype(F32)
    chosen = jnp.zeros((N_ROUTE_GROUPS, tl), F32)
    for _ in range(TOPK_ROUTE_GROUPS):
        i, _m = first_argmax(gsc, io_8, N_ROUTE_GROUPS)
        hit = io_8 == i
        chosen = jnp.where(hit, 1.0, chosen)
        gsc = jnp.where(hit, neg, gsc)
    msel = jnp.concatenate(
        [jnp.where(chosen[g:g + 1] > 0.5, sel[g * ROUTE_GROUP_SIZE:(g + 1) * ROUTE_GROUP_SIZE], neg)
         for g in range(N_ROUTE_GROUPS)], axis=0)

    io_e = lax.broadcasted_iota(jnp.int32, (N_EXPERTS, tl), 0).astype(F32)
    multi = jnp.zeros((N_EXPERTS, tl), F32)
    idx_rows, sc_rows = [], []
    for _ in range(TOP_K):
        i, _m = first_argmax(msel, io_e, N_EXPERTS)
        hit = io_e == i
        sc_rows.append(jnp.sum(jnp.where(hit, scores, 0.0), axis=0, keepdims=True))
        msel = jnp.where(hit, neg, msel)
        multi = jnp.where(hit, 1.0, multi)
        idx_rows.append(i)
    den = sc_rows[0]
    for r in sc_rows[1:]:
        den = den + r
    den = den + 1e-20
    wts_ref[...] = jnp.concatenate([r / den * ROUTED_SCALE for r in sc_rows], axis=0)
    idx_ref[...] = jnp.concatenate(idx_rows, axis=0).astype(jnp.int32)

    multi_b = multi.astype(BF16)
    carry = carry_ref[...]
    rank_full = _dot(multi_b, upper_ref[...]) + carry
    rank_ref[...] = jnp.concatenate(
        [jnp.sum(jnp.where(io_e == i, rank_full, 0.0), axis=0, keepdims=True) for i in idx_rows],
        axis=0).astype(jnp.int32)
    carry = carry + _dot(multi_b, ones_ref[...])
    carry_ref[...] = carry
    cnt_ref[...] = carry[:, :LANES]


def _route(scores_t, router_bias):
    e, t = scores_t.shape
    tl = min(ROUTE_TILE, t)
    bias = jnp.broadcast_to(router_bias.astype(F32).reshape(e, 1), (e, tl))
    upper = (jnp.arange(tl)[:, None] < jnp.arange(tl)[None, :]).astype(BF16)
    ones = jnp.ones((tl, tl), BF16)
    tok = pl.BlockSpec((TOP_K, tl), lambda i: (0, i))

    def const(shape):
        return pl.BlockSpec(shape, lambda i: (0,) * len(shape))

    return pl.pallas_call(
        _route_kernel,
        grid=(t // tl,),
        in_specs=[pl.BlockSpec((e, tl), lambda i: (0, i)), const((e, tl)), const((tl, tl)), const((tl, tl))],
        out_specs=[tok, tok, tok, const((e, LANES))],
        out_shape=[jax.ShapeDtypeStruct((TOP_K, t), jnp.int32), jax.ShapeDtypeStruct((TOP_K, t), F32),
                   jax.ShapeDtypeStruct((TOP_K, t), jnp.int32), jax.ShapeDtypeStruct((e, LANES), F32)],
        scratch_shapes=[pltpu.VMEM((e, tl), F32)],
        compiler_params=_params(("arbitrary",)),
        name="route",
    )(scores_t, bias, upper, ones)


def _dest_kernel(idx_ref, rank_ref, start_ref, o_ref):
    tl = idx_ref.shape[1]
    io_e = lax.broadcasted_iota(jnp.int32, (N_EXPERTS, tl), 0).astype(F32)
    idx = idx_ref[...].astype(F32)
    start = start_ref[...]
    rows = [jnp.sum(jnp.where(io_e == idx[kk:kk + 1], start, 0.0), axis=0, keepdims=True) for kk in range(TOP_K)]
    o_ref[...] = jnp.concatenate(rows, axis=0).astype(jnp.int32) + rank_ref[...]


def _dest(idx_t, rank_t, padded_start):
    k, t = idx_t.shape
    tl = min(DEST_TILE, t)
    start = jnp.broadcast_to(padded_start.astype(F32).reshape(N_EXPERTS, 1), (N_EXPERTS, tl))
    tok = pl.BlockSpec((k, tl), lambda i: (0, i))
    return pl.pallas_call(
        _dest_kernel,
        grid=(t // tl,),
        in_specs=[tok, tok, pl.BlockSpec((N_EXPERTS, tl), lambda i: (0, 0))],
        out_specs=tok,
        out_shape=jax.ShapeDtypeStruct((k, t), jnp.int32),
        compiler_params=_params(("parallel",)),
        name="dest",
    )(idx_t, rank_t, start)


def _expert_kernel(be_ref, nv_ref, slot_ref, nxt_ref, x_ref, wg_hbm, wu_hbm, wd_hbm, o_ref,
                   wg_buf, wu_buf, wd_buf, wgb_ref, wub_ref, wdb_ref, sem):
    def weight_copies(expert, slot):
        return (pltpu.make_async_copy(wg_hbm.at[expert], wg_buf.at[slot], sem.at[slot, 0]),
                pltpu.make_async_copy(wu_hbm.at[expert], wu_buf.at[slot], sem.at[slot, 1]),
                pltpu.make_async_copy(wd_hbm.at[expert], wd_buf.at[slot], sem.at[slot, 2]))

    @pl.when(pl.program_id(0) == 0)
    def _():
        for cp in weight_copies(be_ref[0], slot_ref[0]):
            cp.start()

    half = EXPERT_BLOCK // 2
    for sub in range(EXPERT_STEP_BLOCKS):
        b = pl.program_id(0) * EXPERT_STEP_BLOCKS + sub
        base = sub * EXPERT_BLOCK
        nv = nv_ref[b]

        @pl.when(jnp.logical_or(b == 0, be_ref[b] != be_ref[jnp.maximum(b - 1, 0)]))
        def _():
            slot = slot_ref[b]
            for cp in weight_copies(be_ref[b], slot):
                cp.wait()
            nxt = nxt_ref[b]

            @pl.when(nxt >= 0)
            def _():
                for cp in weight_copies(nxt, 1 - slot):
                    cp.start()

            wgb_ref[...] = wg_buf[slot].astype(BF16)
            wub_ref[...] = wu_buf[slot].astype(BF16)
            wdb_ref[...] = wd_buf[slot].astype(BF16)

        def swiglu(rows, base=base, nv=nv):
            row = lax.broadcasted_iota(jnp.int32, (rows, x_ref.shape[1]), 0)
            x = _unpack_rows(jnp.where(row < nv, x_ref[base:base + rows, :], 0))
            g = _dot(x, wgb_ref[...])
            u = _dot(x, wub_ref[...])
            o_ref[base:base + rows, :] = _pack_rows(_dot((_silu(g) * u).astype(BF16), wdb_ref[...]))

        @pl.when(nv > half)
        def _():
            swiglu(EXPERT_BLOCK)

        @pl.when(jnp.logical_and(nv > 0, nv <= half))
        def _():
            swiglu(half)
            o_ref[base + half:base + EXPERT_BLOCK, :] = jnp.zeros((EXPERT_BLOCK - half, o_ref.shape[1]), o_ref.dtype)

        @pl.when(nv == 0)
        def _():
            o_ref[base:base + EXPERT_BLOCK, :] = jnp.zeros((EXPERT_BLOCK, o_ref.shape[1]), o_ref.dtype)


def _experts(block_expert, block_valid, xs, w_gate, w_up, w_down):
    n_slots = xs.shape[0]
    n_blocks = n_slots // EXPERT_BLOCK
    step_rows = EXPERT_STEP_BLOCKS * EXPERT_BLOCK
    assert n_blocks % EXPERT_STEP_BLOCKS == 0
    d, f = w_gate.shape[-2:]
    ids = jnp.arange(n_blocks, dtype=jnp.int32)
    change = jnp.concatenate([jnp.ones((1,), bool), block_expert[1:] != block_expert[:-1]])
    run_slot = ((jnp.cumsum(change.astype(jnp.int32)) - 1) % 2).astype(jnp.int32)
    change_at = jnp.where(change, ids, n_blocks)
    next_change = jnp.concatenate([lax.cummin(change_at[::-1])[::-1][1:], jnp.full((1,), n_blocks, jnp.int32)])
    next_expert = jnp.where(next_change < n_blocks, block_expert[jnp.minimum(next_change, n_blocks - 1)], -1)
    hbm = pl.BlockSpec(memory_space=pl.ANY)
    grid_spec = pltpu.PrefetchScalarGridSpec(
        num_scalar_prefetch=4,
        grid=(n_blocks // EXPERT_STEP_BLOCKS,),
        in_specs=[pl.BlockSpec((step_rows, PACKED),
                               lambda i, be, nv, sl, nx: (jnp.where(nv[i * EXPERT_STEP_BLOCKS] > 0, i, 0), 0)),
                  hbm, hbm, hbm],
        out_specs=pl.BlockSpec((step_rows, PACKED), lambda i, be, nv, sl, nx: (i, 0)),
        scratch_shapes=[pltpu.VMEM((2, d, f), w_gate.dtype), pltpu.VMEM((2, d, f), w_up.dtype),
                        pltpu.VMEM((2, f, d), w_down.dtype),
                        pltpu.VMEM((d, f), BF16), pltpu.VMEM((d, f), BF16), pltpu.VMEM((f, d), BF16),
                        pltpu.SemaphoreType.DMA((2, 3))],
    )
    return pl.pallas_call(
        _expert_kernel,
        grid_spec=grid_spec,
        out_shape=jax.ShapeDtypeStruct((n_slots, PACKED), jnp.int32),
        compiler_params=_params(("arbitrary",)),
        name="experts",
    )(block_expert, block_valid, run_slot, next_expert.astype(jnp.int32), xs, w_gate, w_up, w_down)


def _final_kernel(x1_ref, h2_ref, yk_ref, wts_ref, g2_ref, wsg_ref, wsu_ref, wsd_ref, fnw_ref, o_ref):
    h = _unpack_rows(h2_ref[0])
    act = (_silu(_dot(h, wsg_ref[...])) * _dot(h, wsu_ref[...])).astype(BF16)
    moe = _dot(act, wsd_ref[...])
    wts = wts_ref[...]
    for kk in range(TOP_K):
        moe = moe + _unpack_rows(yk_ref[kk]).astype(F32) * wts[:, kk:kk + 1]
    x2 = x1_ref[0] + g2_ref[0] * moe
    ms = jnp.mean(x2 * x2, axis=-1, keepdims=True)
    o_ref[0] = x2 * lax.rsqrt(ms + EPS) * fnw_ref[...]


def _final(x1, h2p, yk, wts, gate2, w_sh_gate, w_sh_up, w_sh_down, final_norm_w, tm):
    bsz, s, d = x1.shape
    nt = s // tm
    f = w_sh_gate.shape[-1]
    seq = pl.BlockSpec((1, tm, d), lambda b, i: (b, i, 0))

    def const(shape):
        return pl.BlockSpec(shape, lambda b, i: (0,) * len(shape))

    return pl.pallas_call(
        _final_kernel,
        grid=(bsz, nt),
        in_specs=[seq, pl.BlockSpec((1, tm, PACKED), lambda b, i: (b, i, 0)),
                  pl.BlockSpec((TOP_K, tm, PACKED), lambda b, i: (0, b * nt + i, 0)),
                  pl.BlockSpec((tm, TOP_K), lambda b, i: (b * nt + i, 0)),
                  pl.BlockSpec((1, 1, d), lambda b, i: (b, 0, 0)),
                  const((d, f)), const((d, f)), const((f, d)), const((1, d))],
        out_specs=seq,
        out_shape=jax.ShapeDtypeStruct((bsz, s, d), F32),
        compiler_params=_params(("parallel", "parallel")),
        name="final",
    )(x1, h2p, yk, wts, gate2, w_sh_gate.astype(BF16), w_sh_up.astype(BF16), w_sh_down.astype(BF16),
      final_norm_w.astype(F32).reshape(1, d))


def _gather_rows(table, idx):
    n = idx.shape[0]
    width = table.shape[1]
    workers = SC_CORES * SC_SUBCORES
    per_worker = n // workers
    steps = per_worker // SC_WINDOW
    assert per_worker * workers == n and steps * SC_WINDOW == per_worker and steps % 2 == 0
    mesh = plsc.VectorSubcoreMesh(core_axis_name="c", subcore_axis_name="s")
    win = lambda dtype, *shape: pltpu.VMEM(shape, dtype)

    @functools.partial(
        pl.kernel, mesh=mesh,
        out_type=jax.ShapeDtypeStruct((n, width), table.dtype),
        scratch_types=[win(jnp.int32, SC_WINDOW), win(jnp.int32, SC_WINDOW),
                       win(table.dtype, SC_WINDOW, width), win(table.dtype, SC_WINDOW, width),
                       pltpu.SemaphoreType.DMA, pltpu.SemaphoreType.DMA,
                       pltpu.SemaphoreType.DMA, pltpu.SemaphoreType.DMA],
        name="gather_rows",
    )
    def run(table_hbm, idx_hbm, out_hbm, idx_a, idx_b, rows_a, rows_b, gsem_a, gsem_b, osem_a, osem_b):
        base = (lax.axis_index("s") * SC_CORES + lax.axis_index("c")) * per_worker

        @pl.loop(0, steps, step=2)
        def _(i):
            off_a = pl.multiple_of(base + i * SC_WINDOW, SC_WINDOW)
            off_b = pl.multiple_of(off_a + SC_WINDOW, SC_WINDOW)
            pltpu.sync_copy(idx_hbm.at[pl.ds(off_a, SC_WINDOW)], idx_a)
            gather_a = pltpu.async_copy(table_hbm.at[idx_a], rows_a, gsem_a)
            pltpu.sync_copy(idx_hbm.at[pl.ds(off_b, SC_WINDOW)], idx_b)
            gather_b = pltpu.async_copy(table_hbm.at[idx_b], rows_b, gsem_b)
            gather_a.wait()
            store_a = pltpu.async_copy(rows_a, out_hbm.at[pl.ds(off_a, SC_WINDOW)], osem_a)
            gather_b.wait()
            store_b = pltpu.async_copy(rows_b, out_hbm.at[pl.ds(off_b, SC_WINDOW)], osem_b)
            store_a.wait()
            store_b.wait()

    return run(table, idx)


def _scatter_rows(rows, dest_kt, n_slots):
    t, width = rows.shape
    top_k = dest_kt.shape[0]
    workers = SC_CORES * SC_SUBCORES
    window = LANES
    per_worker = t // workers
    steps = per_worker // window
    assert per_worker * workers == t and steps * window == per_worker
    mesh = plsc.VectorSubcoreMesh(core_axis_name="c", subcore_axis_name="s")

    @functools.partial(
        pl.kernel, mesh=mesh,
        out_type=jax.ShapeDtypeStruct((n_slots, width), rows.dtype),
        scratch_types=[pltpu.VMEM((top_k, window), jnp.int32),
                       pltpu.VMEM((window, width), rows.dtype),
                       pltpu.SemaphoreType.DMA],
        name="scatter_rows",
    )
    def run(rows_hbm, dest_hbm, out_hbm, idx_v, rows_v, sem):
        base = (lax.axis_index("s") * SC_CORES + lax.axis_index("c")) * per_worker

        @pl.loop(0, steps)
        def _(i):
            off = pl.multiple_of(base + i * window, window)
            pltpu.sync_copy(rows_hbm.at[pl.ds(off, window)], rows_v)
            pltpu.sync_copy(dest_hbm.at[:, pl.ds(off, window)], idx_v)
            copies = [pltpu.async_copy(rows_v, out_hbm.at[idx_v.at[kk]], sem) for kk in range(top_k)]
            for cp in copies:
                cp.wait()

    return run(rows, dest_kt)


def _layer(x, c, positions, w_ada, b_ada, norm1_w, w_in, conv_w, conv_b, dt_bias, a_log, d_skip, ssm_norm_w,
           w_ssm_out, w_ret_out, w_out, norm2_w, w_router, router_bias, w_exp_gate, w_exp_up, w_exp_down,
           w_sh_gate, w_sh_up, w_sh_down):
    bsz, s, d = x.shape
    t = bsz * s
    tm = min(TOKEN_TILE, s)

    mod = _mod(c, w_ada, b_ada)
    shift1, scale1, gate1, shift2, scale2, gate2 = (mod[:, i * d:(i + 1) * d].reshape(bsz, 1, d) for i in range(6))

    bounds = np.cumsum((0, SSM_D_INNER, SSM_CONV_DIM, SSM_HEADS, RET_QK_WIDTH, RET_QK_WIDTH, RET_V_WIDTH,
                        RET_V_WIDTH, d, d))
    wz, wxbc, wdt, wq, wk, wv, wg, wgs, wgr = (w_in[:, bounds[i]:bounds[i + 1]] for i in range(9))
    nw1 = norm1_w.astype(F32).reshape(1, d)
    w_a = jnp.concatenate([wxbc, wz, jnp.pad(wdt, ((0, 0), (0, DT_PAD - SSM_HEADS)))], axis=1).astype(BF16)
    w_b = jnp.concatenate([wq, wk, wv, wg, wgs, wgr], axis=1).astype(BF16)
    h, xbc, zs, dt = _inproj(_inproj_ssd_kernel, x, (nw1, scale1, shift1), w_a,
                             (d, SSM_CONV_DIM, SSM_D_INNER, DT_PAD), (BF16, BF16, BF16, F32), tm)
    q, k, v, g, gs, gr = _inproj(_inproj_ret_kernel, h, (), w_b,
                                 (RET_QK_WIDTH, RET_QK_WIDTH, RET_V_WIDTH, RET_V_WIDTH, d, d), (BF16,) * 6, tm)

    ys = _ssd(xbc, zs, dt, conv_w, conv_b, dt_bias, a_log, d_skip, ssm_norm_w)
    cos, sin = _rope(positions, tm)
    yr = _ret(q, k, v, g, cos, sin)
    x1, h2, scores_t = _merge(ys, yr, gs, gr, x, gate1, scale2, shift2, norm2_w, w_ssm_out, w_ret_out, w_out,
                              w_router, tm)

    idx_t, wts_t, rank_t, cnt = _route(scores_t, router_bias)

    counts = cnt[:, 0].astype(jnp.int32)
    padded = (counts + EXPERT_BLOCK - 1) // EXPERT_BLOCK * EXPERT_BLOCK
    padded_end = jnp.cumsum(padded)
    padded_start = padded_end - padded
    n_blocks = -(-(t * TOP_K) // EXPERT_BLOCK) + N_EXPERTS
    n_slots = n_blocks * EXPERT_BLOCK
    dest_t = _dest(idx_t, rank_t, padded_start)
    block_first = jnp.arange(n_blocks, dtype=jnp.int32) * EXPERT_BLOCK
    block_expert = jnp.minimum(jnp.sum(padded_end[None, :] <= block_first[:, None], axis=1), N_EXPERTS - 1)
    owner = (block_expert[:, None] == jnp.arange(N_EXPERTS, dtype=jnp.int32)[None, :]).astype(jnp.int32)
    region_end = jnp.sum(owner * (padded_start + counts)[None, :], axis=1)
    block_valid = jnp.clip(region_end - block_first, 0, EXPERT_BLOCK).astype(jnp.int32)
    xs = _scatter_rows(h2.reshape(t, PACKED), dest_t, n_slots)
    y_sorted = _experts(block_expert.astype(jnp.int32), block_valid, xs, w_exp_gate, w_exp_up, w_exp_down)
    yk = _gather_rows(y_sorted, dest_t.reshape(-1)).reshape(TOP_K, t, PACKED)
    return x1, h2, yk, wts_t.T, gate2


def kernel(x, c, positions, w_ada, b_ada, norm1_w, w_in, conv_w, conv_b, dt_bias, a_log, d_skip, ssm_norm_w,
           w_ssm_out, w_ret_out, w_out, norm2_w, w_router, router_bias, w_exp_gate, w_exp_up, w_exp_down,
           w_sh_gate, w_sh_up, w_sh_down, final_norm_w):
    assert w_ada.shape[0] == 1, "the final rmsnorm is fused into the single layer's last kernel"
    tm = min(TOKEN_TILE, x.shape[1])
    x1, h2, yk, wts, gate2 = _layer(
        x, c, positions, w_ada[0], b_ada[0], norm1_w[0], w_in[0], conv_w[0], conv_b[0], dt_bias[0], a_log[0],
        d_skip[0], ssm_norm_w[0], w_ssm_out[0], w_ret_out[0], w_out[0], norm2_w[0], w_router[0],
        router_bias[0], w_exp_gate[0], w_exp_up[0], w_exp_down[0], w_sh_gate[0], w_sh_up[0], w_sh_down[0])
    return _final(x1, h2, yk, wts, gate2, w_sh_gate[0], w_sh_up[0], w_sh_down[0], final_norm_w, tm)
```
